```python
import math, functools
import jax, jax.numpy as jnp
from jax import lax
import numpy as np

D_MODEL = 1024
BATCH = 16
SEQ = 2048
DEPTH = 1
DEC_BATCH = 128
DEC_SEQ = 8
PAST_LEN = 16384
PAGE_SIZE = 128

HEAD_DIM = 64
N_HEADS = 8
N_KV = 2
GROUP = N_HEADS // N_KV
WINDOW = 128
BLOCK = WINDOW
Q_W = N_HEADS * HEAD_DIM
KV_W = N_KV * HEAD_DIM
N_BUCKETS = 32
MAX_EXACT = N_BUCKETS // 2
REL_MAX_DIST = 128
RW_N = 64
RW_HEADS = 8
RW = RW_HEADS * RW_N
LORA_W = 64
LORA_A = 64
LORA_G = 128
RW_SHIFT = 3 * RW + LORA_W + LORA_A + LORA_G
GATE_W = 2 * D_MODEL
IN_COLS = Q_W + 2 * KV_W + RW_SHIFT + GATE_W
D_FF = 2816
CONV_W = 3
NORM_EPS = 1e-6
GN_EPS = 64e-5
NEG = -1e30

kernel_name = 'swa_rwkv7_gated_hybrid_step'


def rmsnorm(x, g):
    xf = x.astype(jnp.float32)
    xf = xf * lax.rsqrt(jnp.mean(xf * xf, axis=-1, keepdims=True) + NORM_EPS)
    return (xf * g.astype(jnp.float32)).astype(x.dtype)


def t5_bucket(dist):
    n = jnp.maximum(dist, 0)
    nf = jnp.maximum(n, 1).astype(jnp.float32)
    large = MAX_EXACT + (jnp.log(nf / MAX_EXACT) / math.log(REL_MAX_DIST / MAX_EXACT)
                         * (N_BUCKETS - MAX_EXACT)).astype(jnp.int32)
    return jnp.where(n < MAX_EXACT, n, jnp.minimum(large, N_BUCKETS - 1))


def attend(q, k, v, dist, valid, rel_bias, sinks):
    s = jnp.einsum('...qhgd,...khd->...hgqk', q, k).astype(jnp.float32) * (HEAD_DIM ** -0.5)
    bias = rel_bias[t5_bucket(dist)].astype(jnp.float32)
    bias = jnp.transpose(bias, (2, 0, 1)).reshape(N_KV, GROUP, dist.shape[0], dist.shape[1])
    s = jnp.where(valid, s + bias, NEG)
    sink = jnp.broadcast_to(sinks.astype(jnp.float32).reshape(N_KV, GROUP, 1, 1), s.shape[:-1] + (1,))
    p = jax.nn.softmax(jnp.concatenate([s, sink], axis=-1), axis=-1)[..., :-1]
    return jnp.einsum('...hgqk,...khd->...qhgd', p.astype(v.dtype), v)


def banded_attention(q, k, v, rel_bias, sinks, win_buf):
    B, T = q.shape[:2]
    nb = T // BLOCK
    qb = q.reshape(B, nb, BLOCK, N_KV, GROUP, HEAD_DIM)
    kb = k.reshape(B, nb, BLOCK, N_KV, HEAD_DIM)
    vb = v.reshape(B, nb, BLOCK, N_KV, HEAD_DIM)

    def with_prev(t):
        prev = jnp.pad(t, ((0, 0), (1, 0), (0, 0), (0, 0), (0, 0)))[:, :-1]
        return jnp.concatenate([prev, t], axis=2)

    kc, vc = with_prev(kb), with_prev(vb)
    qi = jnp.arange(BLOCK) + BLOCK
    kj = jnp.arange(2 * BLOCK)
    dist = qi[:, None] - kj[None, :]
    key_abs = jnp.arange(nb)[:, None] * BLOCK - BLOCK + kj[None, :]
    valid = ((dist >= 0) & (dist < WINDOW))[None] & (key_abs >= 0)[:, None, :]
    o = attend(qb, kc, vc, dist, valid[:, None, None], rel_bias, sinks)
    return o.reshape(B, T, Q_W), k[:, T - win_buf:], v[:, T - win_buf:]


def window_cache_attention(q, k, v, cache_k, cache_v, rel_bias, sinks):
    B, T = q.shape[:2]
    wb = cache_k.shape[1]
    kc = jnp.concatenate([cache_k.astype(k.dtype), k], axis=1)
    vc = jnp.concatenate([cache_v.astype(v.dtype), v], axis=1)
    q_pos = PAST_LEN + jnp.arange(T)
    k_pos = jnp.concatenate([PAST_LEN - wb + jnp.arange(wb), PAST_LEN + jnp.arange(T)])
    dist = q_pos[:, None] - k_pos[None, :]
    valid = (dist >= 0) & (dist < WINDOW)
    o = attend(q, kc, vc, dist, valid, rel_bias, sinks)
    return o.reshape(B, T, Q_W), kc[:, T:], vc[:, T:]


def wkv_scan(r, w, k, v, av, bv, s0):
    def step(S, inp):
        r_t, w_t, k_t, v_t, a_t, b_t = inp
        sa = jnp.einsum('bhij,bhj->bhi', S, a_t)
        S = S * w_t[:, :, None, :] + sa[..., None] * b_t[:, :, None, :] + v_t[..., None] * k_t[:, :, None, :]
        return S, jnp.einsum('bhij,bhj->bhi', S, r_t)

    xs = tuple(jnp.moveaxis(t, 1, 0) for t in (r, w, k, v, av, bv))
    s_fin, ys = lax.scan(step, s0, xs)
    return jnp.moveaxis(ys, 0, 1), s_fin


def rwkv7_mixer(p, shift0, wkv0, mu_shift, w0, w2, a0, a2, g2, k_k, k_a, r_k, lnx_g, lnx_b):
    B, T, _ = p.shape
    f32 = jnp.float32
    prev = jnp.concatenate([shift0[:, None, :].astype(p.dtype), p[:, :-1]], axis=1)
    xs = p + mu_shift * (prev - p)
    r, k, v, lw, la, lg = jnp.split(xs, [RW, 2 * RW, 3 * RW, 3 * RW + LORA_W, 3 * RW + LORA_W + LORA_A], axis=-1)
    w_log = -jax.nn.softplus(-(w0 + jnp.tanh(lw) @ w2).astype(f32)) - 0.5
    decay = jnp.exp(-jnp.exp(w_log))
    a = jax.nn.sigmoid((a0 + la @ a2).astype(f32))
    g = jax.nn.sigmoid(lg) @ g2

    def hs(t):
        return t.astype(f32).reshape(B, T, RW_HEADS, RW_N)

    r, k, v, decay, a = hs(r), hs(k), hs(v), hs(decay), hs(a)
    kk = k * k_k.astype(f32).reshape(RW_HEADS, RW_N)
    kk = kk / jnp.maximum(jnp.sqrt(jnp.sum(kk * kk, axis=-1, keepdims=True)), 1e-12)
    k = k * (1.0 + (a - 1.0) * k_a.astype(f32).reshape(RW_HEADS, RW_N))
    y, s_new = wkv_scan(r, decay, k, v, -kk, kk * a, wkv0.astype(f32))
    mean = jnp.mean(y, axis=-1, keepdims=True)
    var = jnp.mean(jnp.square(y - mean), axis=-1, keepdims=True)
    y = (y - mean) * lax.rsqrt(var + GN_EPS)
    y = y * lnx_g.astype(f32).reshape(RW_HEADS, RW_N) + lnx_b.astype(f32).reshape(RW_HEADS, RW_N)
    y = y + jnp.sum(r * k * r_k.astype(f32), axis=-1, keepdims=True) * v
    out = y.reshape(B, T, RW).astype(p.dtype) * g
    return out, p[:, -1], s_new


def conv_ffn(h, conv0, w_up, conv_w, conv_b, w_down):
    T = h.shape[1]
    ug, uv = jnp.split(h @ w_up, 2, axis=-1)
    ucat = jnp.concatenate([conv0.astype(ug.dtype), ug], axis=1)
    c = conv_b
    for j in range(CONV_W):
        c = c + conv_w[j] * ucat[:, j:j + T]
    y = (jax.nn.gelu(c) * uv) @ w_down
    return y, ucat[:, T:]


def layer_forward(x, attn_fn, shift0, wkv0, conv0, norm1_g, w_in, mu_shift, w0, w2, a0, a2, g2,
                  k_k, k_a, r_k, lnx_g, lnx_b, w_pa, w_pb, w_o, norm2_g, w_up, conv_w, conv_b, w_down):
    B, T, _ = x.shape
    h = rmsnorm(x, norm1_g)
    proj = h @ w_in
    q, ka, va, rw, gates = jnp.split(proj, [Q_W, Q_W + KV_W, Q_W + 2 * KV_W, Q_W + 2 * KV_W + RW_SHIFT], axis=-1)
    q = q.reshape(B, T, N_KV, GROUP, HEAD_DIM)
    ka = ka.reshape(B, T, N_KV, HEAD_DIM)
    va = va.reshape(B, T, N_KV, HEAD_DIM)
    att, k_rows, v_rows = attn_fn(q, ka, va)
    rw_out, shift_new, wkv_new = rwkv7_mixer(rw, shift0, wkv0, mu_shift, w0, w2, a0, a2, g2,
                                             k_k, k_a, r_k, lnx_g, lnx_b)
    g_a, g_b = jnp.split(jax.nn.sigmoid(gates), 2, axis=-1)
    x = x + (g_a * (att @ w_pa) + g_b * (rw_out @ w_pb)) @ w_o
    y_ffn, conv_new = conv_ffn(rmsnorm(x, norm2_g), conv0, w_up, conv_w, conv_b, w_down)
    x = x + y_ffn
    return x, (k_rows, v_rows, shift_new, wkv_new, conv_new)


def setup_inputs(seed: int = 0) -> dict:
    key = jax.random.key(seed)
    ks = jax.random.split(key, 40)
    f32 = jnp.float32

    def nrm(k, shape, scale):
        return jax.random.normal(k, shape, f32) * scale

    wb = min(WINDOW, PAST_LEN)
    L = DEPTH
    return {
        'x_prompt': nrm(ks[0], (BATCH, SEQ, D_MODEL), 1.0),
        'x_sample': nrm(ks[1], (DEC_BATCH, DEC_SEQ, D_MODEL), 1.0),
        'cache_win_k': nrm(ks[2], (L, DEC_BATCH, wb, N_KV, HEAD_DIM), 1.0),
        'cache_win_v': nrm(ks[3], (L, DEC_BATCH, wb, N_KV, HEAD_DIM), 1.0),
        'state_shift': nrm(ks[4], (L, DEC_BATCH, RW_SHIFT), 1.0),
        'state_wkv': nrm(ks[5], (L, DEC_BATCH, RW_HEADS, RW_N, RW_N), 0.3),
        'state_conv': nrm(ks[6], (L, DEC_BATCH, CONV_W - 1, D_FF), 1.0),
        'rel_bias': nrm(ks[7], (N_BUCKETS, N_HEADS), 0.5),
        'norm1_g': 1.0 + nrm(ks[8], (L, D_MODEL), 0.05),
        'w_in': nrm(ks[9], (L, D_MODEL, IN_COLS), D_MODEL ** -0.5),
        'sinks': nrm(ks[10], (L, N_HEADS), 0.5),
        'mu_shift': jax.random.uniform(ks[11], (L, RW_SHIFT), f32),
        'w0': jax.random.uniform(ks[12], (L, RW), f32, -6.0, -1.0),
        'w2': nrm(ks[13], (L, LORA_W, RW), 0.5 * LORA_W ** -0.5),
        'a0': nrm(ks[14], (L, RW), 0.05),
        'a2': nrm(ks[15], (L, LORA_A, RW), LORA_A ** -0.5),
        'g2': nrm(ks[16], (L, LORA_G, RW), LORA_G ** -0.5),
        'k_k': 0.85 + nrm(ks[17], (L, RW), 0.05),
        'k_a': 1.0 + nrm(ks[18], (L, RW), 0.05),
        'r_k': nrm(ks[19], (L, RW_HEADS, RW_N), 0.1),
        'lnx_g': 1.0 + nrm(ks[20], (L, RW), 0.05),
        'lnx_b': nrm(ks[21], (L, RW), 0.02),
        'w_pa': nrm(ks[22], (L, Q_W, D_MODEL), Q_W ** -0.5),
        'w_pb': nrm(ks[23], (L, RW, D_MODEL), RW ** -0.5),
        'w_o': nrm(ks[24], (L, D_MODEL, D_MODEL), D_MODEL ** -0.5),
        'norm2_g': 1.0 + nrm(ks[25], (L, D_MODEL), 0.05),
        'w_up': nrm(ks[26], (L, D_MODEL, 2 * D_FF), D_MODEL ** -0.5),
        'conv_w': nrm(ks[27], (L, CONV_W, D_FF), CONV_W ** -0.5),
        'conv_b': nrm(ks[28], (L, D_FF), 0.02),
        'w_down': nrm(ks[29], (L, D_FF, D_MODEL), D_FF ** -0.5),
        'final_g': 1.0 + nrm(ks[30], (D_MODEL,), 0.05),
    }


def reference(x_prompt, x_sample, cache_win_k, cache_win_v, state_shift, state_wkv, state_conv,
              rel_bias, norm1_g, w_in, sinks, mu_shift, w0, w2, a0, a2, g2, k_k, k_a, r_k,
              lnx_g, lnx_b, w_pa, w_pb, w_o, norm2_g, w_up, conv_w, conv_b, w_down, final_g):
    xp, xs = x_prompt, x_sample
    bp = xp.shape[0]
    win_buf = cache_win_k.shape[2]
    pk, pv, psh, pwkv, pconv = [], [], [], [], []
    sk, sv, ssh, swkv, sconv = [], [], [], [], []
    for l in range(DEPTH):
        weights = (norm1_g[l], w_in[l], mu_shift[l], w0[l], w2[l], a0[l], a2[l], g2[l], k_k[l], k_a[l],
                   r_k[l], lnx_g[l], lnx_b[l], w_pa[l], w_pb[l], w_o[l], norm2_g[l], w_up[l], conv_w[l],
                   conv_b[l], w_down[l])
        prompt_attn = functools.partial(banded_attention, rel_bias=rel_bias, sinks=sinks[l], win_buf=win_buf)
        sample_attn = functools.partial(window_cache_attention, cache_k=cache_win_k[l], cache_v=cache_win_v[l],
                                        rel_bias=rel_bias, sinks=sinks[l])
        xp, st_p = layer_forward(xp, prompt_attn,
                                 jnp.zeros((bp, RW_SHIFT), xp.dtype),
                                 jnp.zeros((bp, RW_HEADS, RW_N, RW_N), jnp.float32),
                                 jnp.zeros((bp, CONV_W - 1, D_FF), xp.dtype),
                                 *weights)
        xs, st_s = layer_forward(xs, sample_attn, state_shift[l], state_wkv[l], state_conv[l], *weights)
        pk.append(st_p[0]); pv.append(st_p[1]); psh.append(st_p[2]); pwkv.append(st_p[3]); pconv.append(st_p[4])
        sk.append(st_s[0]); sv.append(st_s[1]); ssh.append(st_s[2]); swkv.append(st_s[3]); sconv.append(st_s[4])
    y_prompt = rmsnorm(xp, final_g)
    y_sample = rmsnorm(xs, final_g)
    p_win_k = jnp.stack(pk); p_win_v = jnp.stack(pv); p_shift = jnp.stack(psh)
    p_wkv = jnp.stack(pwkv); p_conv = jnp.stack(pconv)
    s_win_k = jnp.stack(sk); s_win_v = jnp.stack(sv); s_shift = jnp.stack(ssh)
    s_wkv = jnp.stack(swkv); s_conv = jnp.stack(sconv)
    return (y_prompt, y_sample, p_win_k, p_win_v, p_shift, p_wkv, p_conv, s_win_k, s_win_v, s_shift, s_wkv, s_conv)
```

```python
import functools
import math

import numpy as np
import jax
import jax.numpy as jnp
from jax import lax
from jax.experimental import pallas as pl
from jax.experimental.pallas import tpu as pltpu

F32 = jnp.float32
BF16 = jnp.bfloat16

HEAD_DIM = 64
N_HEADS = 8
N_KV = 2
WINDOW = 128
N_BUCKETS = 32
MAX_EXACT = N_BUCKETS // 2
REL_MAX_DIST = 128
RW_N = 64
RW_HEADS = 8
RW = RW_HEADS * RW_N
NORM_EPS = 1e-6
GN_EPS = 64e-5
NEG = -1e30
CONV_W = 3

Q_W = N_HEADS * HEAD_DIM
KV_W = N_KV * HEAD_DIM
LANES = 128
SUBLANES = 8
CHUNK = 64
VMEM_LIMIT = 56 * 1024 * 1024
FFN_COL_CHUNKS = 2


def _resident(shape):
    return pl.BlockSpec(shape, lambda *_: (0,) * len(shape), pipeline_mode=pl.Buffered(1))


def _cparams(sem):
    return pltpu.CompilerParams(dimension_semantics=sem, vmem_limit_bytes=VMEM_LIMIT)


def _sigmoid(x):
    return 1.0 / (1.0 + jnp.exp(-x))


def _dg(a, b, kind):
    if kind == "nn":
        dn = (((1,), (0,)), ((), ()))
    else:
        dn = (((1,), (1,)), ((), ()))
    return lax.dot_general(a, b, dn, preferred_element_type=F32)


def _split(x):
    hi = x.astype(BF16)
    lo = (x - hi.astype(F32)).astype(BF16)
    return hi, lo


def _mm(a, b, kind="nn", passes=1):
    if passes == 1:
        return _dg(a.astype(BF16), b.astype(BF16), kind)
    ah, al = _split(a)
    bh, bl = _split(b)
    return _dg(ah, bh, kind) + (_dg(ah, bl, kind) + _dg(al, bh, kind))


def _mm_exact_lhs(a_bf16, b, n_terms):
    out = None
    rem = b
    for _ in range(n_terms):
        piece = rem.astype(BF16)
        term = _dg(a_bf16, piece, "nn")
        out = term if out is None else out + term
        rem = rem - piece.astype(F32)
    return out


def _mm_exact_rhs(a, b_bf16, n_terms):
    out = None
    rem = a
    for _ in range(n_terms):
        piece = rem.astype(BF16)
        term = _dg(piece, b_bf16, "nn")
        out = term if out is None else out + term
        rem = rem - piece.astype(F32)
    return out


def _inproj_kernel(x_ref, g_ref, w_ref, qkv_ref, rw_ref, gate_ref, *, n_qkv, n_rw):
    x = x_ref[...]
    ms = jnp.mean(x * x, axis=-1, keepdims=True)
    h = (x * lax.rsqrt(ms + NORM_EPS) * g_ref[...]).astype(BF16)
    qkv_ref[...] = jnp.dot(h, w_ref[:, :n_qkv], preferred_element_type=F32)
    rw_ref[...] = jnp.dot(h, w_ref[:, n_qkv:n_qkv + n_rw], preferred_element_type=F32)
    gate_ref[...] = _sigmoid(jnp.dot(h, w_ref[:, n_qkv + n_rw:], preferred_element_type=F32))


def _in_proj(x2d, g, w_bf16, n_qkv, n_rw, tm):
    n, d = x2d.shape
    n_gate = w_bf16.shape[1] - n_qkv - n_rw
    return pl.pallas_call(
        functools.partial(_inproj_kernel, n_qkv=n_qkv, n_rw=n_rw),
        grid=(n // tm,),
        in_specs=[
            pl.BlockSpec((tm, d), lambda i: (i, 0)),
            _resident((1, d)),
            _resident(w_bf16.shape),
        ],
        out_specs=[
            pl.BlockSpec((tm, n_qkv), lambda i: (i, 0)),
            pl.BlockSpec((tm, n_rw), lambda i: (i, 0)),
            pl.BlockSpec((tm, n_gate), lambda i: (i, 0)),
        ],
        out_shape=[
            jax.ShapeDtypeStruct((n, n_qkv), F32),
            jax.ShapeDtypeStruct((n, n_rw), F32),
            jax.ShapeDtypeStruct((n, n_gate), F32),
        ],
        compiler_params=_cparams(("parallel",)),
        name="in_proj",
    )(x2d, g.reshape(1, d), w_bf16)


def _t5_bucket_np(dist):
    n = np.maximum(dist, 0)
    nf = np.maximum(n, 1).astype(np.float32)
    large = MAX_EXACT + (np.log(nf / MAX_EXACT) / math.log(REL_MAX_DIST / MAX_EXACT)
                         * (N_BUCKETS - MAX_EXACT)).astype(np.int32)
    return np.where(n < MAX_EXACT, n, np.minimum(large, N_BUCKETS - 1)).astype(np.int32)


def _attn_kernel(q_ref, k1_ref, k2_ref, v1_ref, v2_ref, bucket_ref, relb_ref, sink_ref, o_ref, bias_ref,
                 *, nq, nk, first_block_axis):
    bt = q_ref.shape[0]
    first_step = pl.program_id(0) == 0
    if first_block_axis is not None:
        first_step = jnp.logical_and(first_step, pl.program_id(1) == 0)

    @pl.when(first_step)
    def _():
        bucket = bucket_ref[...]
        for n in range(N_HEADS):
            acc = jnp.full((nq, nk), NEG, F32)
            for b in range(N_BUCKETS):
                acc = jnp.where(bucket == b, relb_ref[b, n], acc)
            c, half = divmod(n, 2)
            bias_ref[c, :, half * nk:(half + 1) * nk] = acc

    def padded(a_ref, b_ref):
        parts = [a_ref[...], b_ref[...]]
        n_now = a_ref.shape[1] + b_ref.shape[1]
        if n_now < nk:
            parts.append(jnp.zeros((bt, nk - n_now, LANES), F32))
        return jnp.concatenate(parts, axis=1)

    kk = padded(k1_ref, k2_ref)
    vv = padded(v1_ref, v2_ref)
    kk_r = pltpu.roll(kk, HEAD_DIM, 2)
    vv_r = pltpu.roll(vv, HEAD_DIM, 2)
    lane = lax.broadcasted_iota(jnp.int32, (1, 1, LANES), 2)
    lo = lane < HEAD_DIM

    def halves(x, x_r, kvh):
        src_lo, src_hi = (x, x_r) if kvh == 0 else (x_r, x)
        even = jnp.where(lo, src_lo, 0.0).astype(BF16)
        odd = jnp.where(lo, 0.0, src_hi).astype(BF16)
        return jnp.concatenate([even, odd], axis=1)

    kcat = [halves(kk, kk_r, h) for h in range(N_KV)]
    vcat = [halves(vv, vv_r, h) for h in range(N_KV)]

    key_idx = lax.broadcasted_iota(jnp.int32, (1, 1, 2 * nk), 2)
    scale = HEAD_DIM ** -0.5
    for c in range(N_HEADS // 2):
        kvh = (2 * c) // (N_HEADS // N_KV)
        qc = q_ref[:, :, c * LANES:(c + 1) * LANES].astype(BF16)
        s = jnp.einsum("bqd,bkd->bqk", qc, kcat[kvh], preferred_element_type=F32) * scale
        bias = bias_ref[c][None]
        readable = bias > 0.5 * NEG
        if first_block_axis is not None:
            no_prev = pl.program_id(first_block_axis) == 0
            prev_key = (key_idx % nk) < (nk // 2)
            readable = jnp.logical_and(readable, jnp.logical_not(jnp.logical_and(no_prev, prev_key)))
        s = jnp.where(readable, s + bias, NEG)
        outs = []
        es = []
        invs = []
        for half in range(2):
            n = 2 * c + half
            sh = s[:, :, half * nk:(half + 1) * nk]
            sink = sink_ref[n]
            m = jnp.maximum(jnp.max(sh, axis=-1, keepdims=True), sink)
            e = jnp.exp(sh - m)
            denom = jnp.sum(e, axis=-1, keepdims=True) + jnp.exp(sink - m)
            es.append(e.astype(BF16))
            invs.append(1.0 / denom)
        e_cat = jnp.concatenate(es, axis=2)
        o = jnp.einsum("bqk,bkd->bqd", e_cat, vcat[kvh], preferred_element_type=F32)
        inv = jnp.where(lo, invs[0], invs[1])
        o_ref[:, :, c * LANES:(c + 1) * LANES] = o * inv


def _attention_prompt(qkv, batch, seq, rel_bias, sinks):
    nblk = seq // WINDOW
    q3 = qkv.reshape(batch * nblk, WINDOW, qkv.shape[1])
    kcol = Q_W // LANES
    vcol = (Q_W + KV_W) // LANES
    nk = 2 * WINDOW
    qi = np.arange(WINDOW)[:, None] + WINDOW
    kj = np.arange(nk)[None, :]
    dist = qi - kj
    bucket = np.where((dist >= 0) & (dist < WINDOW), _t5_bucket_np(dist), -1).astype(np.int32)

    def cur(b, i):
        return (b * nblk + i, 0, 0)

    def prev(b, i):
        return (b * nblk + jnp.maximum(i - 1, 0), 0, 0)

    def col(f, c):
        return lambda b, i: f(b, i)[:2] + (c,)

    out = pl.pallas_call(
        functools.partial(_attn_kernel, nq=WINDOW, nk=nk, first_block_axis=1),
        grid=(batch, nblk),
        in_specs=[
            pl.BlockSpec((1, WINDOW, Q_W), cur),
            pl.BlockSpec((1, WINDOW, LANES), col(prev, kcol)),
            pl.BlockSpec((1, WINDOW, LANES), col(cur, kcol)),
            pl.BlockSpec((1, WINDOW, LANES), col(prev, vcol)),
            pl.BlockSpec((1, WINDOW, LANES), col(cur, vcol)),
            pl.BlockSpec(bucket.shape, lambda b, i: (0, 0)),
            pl.BlockSpec(memory_space=pltpu.SMEM),
            pl.BlockSpec(memory_space=pltpu.SMEM),
        ],
        out_specs=pl.BlockSpec((1, WINDOW, Q_W), cur),
        out_shape=jax.ShapeDtypeStruct((batch * nblk, WINDOW, Q_W), F32),
        scratch_shapes=[pltpu.VMEM((N_HEADS // 2, WINDOW, 2 * nk), F32)],
        compiler_params=_cparams(("arbitrary", "arbitrary")),
        name="attn_prompt",
    )(q3, q3, q3, q3, q3, jnp.asarray(bucket), rel_bias, sinks)
    return out.reshape(batch * seq, Q_W)


def _attention_sample(qkv, batch, seq, cache_k, cache_v, rel_bias, sinks, bt):
    wb = cache_k.shape[1]
    nk = 2 * WINDOW
    q3 = qkv.reshape(batch, seq, qkv.shape[1])
    ck = cache_k.reshape(batch, wb, KV_W)
    cv = cache_v.reshape(batch, wb, KV_W)
    kcol = Q_W // LANES
    vcol = (Q_W + KV_W) // LANES
    tq = np.arange(seq)[:, None]
    j = np.arange(nk)[None, :]
    dist = np.where(j < wb, tq + wb - j, tq - (j - wb))
    ok = (dist >= 0) & (dist < WINDOW) & (j < wb + seq)
    bucket = np.where(ok, _t5_bucket_np(dist), -1).astype(np.int32)
    out = pl.pallas_call(
        functools.partial(_attn_kernel, nq=seq, nk=nk, first_block_axis=None),
        grid=(batch // bt,),
        in_specs=[
            pl.BlockSpec((bt, seq, Q_W), lambda b: (b, 0, 0)),
            pl.BlockSpec((bt, wb, LANES), lambda b: (b, 0, 0)),
            pl.BlockSpec((bt, seq, LANES), lambda b: (b, 0, kcol)),
            pl.BlockSpec((bt, wb, LANES), lambda b: (b, 0, 0)),
            pl.BlockSpec((bt, seq, LANES), lambda b: (b, 0, vcol)),
            pl.BlockSpec(bucket.shape, lambda b: (0, 0)),
            pl.BlockSpec(memory_space=pltpu.SMEM),
            pl.BlockSpec(memory_space=pltpu.SMEM),
        ],
        out_specs=pl.BlockSpec((bt, seq, Q_W), lambda b: (b, 0, 0)),
        out_shape=jax.ShapeDtypeStruct((batch, seq, Q_W), F32),
        scratch_shapes=[pltpu.VMEM((N_HEADS // 2, seq, 2 * nk), F32)],
        compiler_params=_cparams(("arbitrary",)),
        name="attn_sample",
    )(q3, ck, q3, cv, q3, jnp.asarray(bucket), rel_bias, sinks)
    return out.reshape(batch * seq, Q_W)


P_SCORE = 3
P_TINV = 3
P_INTRA = 3
P_STATE = 3


def _wkv_kernel(p_ref, shift0_ref, s0_ref, mu_ref, w2a2_ref, w0_ref, a0_ref, g2_ref, kk_ref, ka_ref, rk_ref,
                lng_ref, lnb_ref, ltri_ref, ones_ref, out_ref, sout_ref, last_ref, sbd_ref,
                *, n_valid, zero_init, lora_w):
    C = CHUNK
    t_in = p_ref.shape[0]
    n_shift = p_ref.shape[1]
    n_pairs = RW // LANES
    c_idx = pl.program_id(1)

    @pl.when(c_idx == 0)
    def _():
        if zero_init:
            last_ref[...] = jnp.zeros_like(last_ref)
            sbd_ref[...] = jnp.zeros_like(sbd_ref)
        else:
            last_ref[...] = shift0_ref[0]
            sbd_ref[...] = s0_ref[0]

    p = p_ref[...]
    if t_in < C:
        p = jnp.concatenate([p, jnp.zeros((C - t_in, n_shift), F32)], axis=0)
    row = lax.broadcasted_iota(jnp.int32, (C, 1), 0)
    prev = jnp.where(row == 0, last_ref[...], pltpu.roll(p, 1, 0))
    last_ref[...] = p_ref[t_in - 1:t_in, :]
    xs = p + mu_ref[...] * (prev - p)

    r = xs[:, 0:RW]
    k = xs[:, RW:2 * RW]
    v = xs[:, 2 * RW:3 * RW]
    lwla = xs[:, 3 * RW:3 * RW + LANES]
    lg = xs[:, 3 * RW + LANES:3 * RW + 2 * LANES]
    lane = lax.broadcasted_iota(jnp.int32, (1, LANES), 1)
    lo = lane < RW_N
    lwla = jnp.where(lane < lora_w, jnp.tanh(lwla), lwla)
    wa = jnp.dot(lwla.astype(BF16), w2a2_ref[...], preferred_element_type=F32)
    logw = -math.exp(-0.5) * _sigmoid(w0_ref[...] + wa[:, :RW])
    a_sig = _sigmoid(a0_ref[...] + wa[:, RW:])
    g = jnp.dot(_sigmoid(lg).astype(BF16), g2_ref[...], preferred_element_type=F32)

    ones_bd = ones_ref[...]

    def headsum(x):
        stacked = jnp.concatenate([x[:, q * LANES:(q + 1) * LANES] for q in range(n_pairs)], axis=0)
        s = _mm_exact_rhs(stacked, ones_bd, 2)
        return jnp.concatenate([s[q * C:(q + 1) * C] for q in range(n_pairs)], axis=1)

    kk = k * kk_ref[...]
    kk = kk * (1.0 / jnp.maximum(jnp.sqrt(headsum(kk * kk)), 1e-12))
    k = k * (1.0 + (a_sig - 1.0) * ka_ref[...])
    a = -kk
    b = kk * a_sig
    if n_valid < C:
        ok = row < n_valid
        r, k, v, a, b = (jnp.where(ok, t, 0.0) for t in (r, k, v, a, b))
        logw = jnp.where(ok, logw, 0.0)

    cw = _mm_exact_lhs(ltri_ref[...], logw, 3)
    cw_last = cw[C - 1:C, :]
    w_incl = jnp.exp(cw)
    w_inv = jnp.exp(-cw)
    w_tail = jnp.exp(cw_last - cw)
    a_t = a * jnp.exp(cw - logw)
    r_t = r * w_incl
    b_t = b * w_inv
    k_t = k * w_inv
    b_h = b * w_tail
    k_h = k * w_tail
    w_c = jnp.exp(cw_last)

    def bd(y):
        return jnp.concatenate([jnp.where(lo, y, 0.0), jnp.where(lo, 0.0, y)], axis=0)

    s_idx = lane % C
    strict = s_idx < row
    incl = s_idx <= row
    row2 = lax.broadcasted_iota(jnp.int32, (2 * C, 1), 0)
    same_head = (row2 < RW_N) == lo
    zeros_sq = jnp.zeros((2 * C, LANES), F32)

    ys = []
    for q in range(n_pairs):
        sl = slice(q * LANES, (q + 1) * LANES)
        gq = jnp.concatenate([a_t[:, sl], r_t[:, sl]], axis=0)
        bk = jnp.concatenate([bd(b_t[:, sl]), bd(k_t[:, sl])], axis=0)
        sc = _mm(gq, bk, "nt", P_SCORE)
        l_ab = jnp.where(strict, sc[:C, :LANES], 0.0)
        l_ak = jnp.where(strict, sc[:C, LANES:], 0.0)
        m_rb = jnp.where(incl, sc[C:, :LANES], 0.0)
        m_rk = jnp.where(incl, sc[C:, LANES:], 0.0)
        vq = v[:, sl]
        bd_v = bd(vq)
        x = jnp.concatenate([a_t[:, sl], _mm(l_ak, bd_v, "nn", P_INTRA)], axis=1)
        pw = l_ab
        n_lvl = int(math.log2(C))
        for lvl in range(n_lvl):
            rhs = jnp.concatenate([bd(x[:, :LANES]), bd(x[:, LANES:])], axis=1)
            x = x + _mm(pw, rhs, "nn", P_TINV)
            if lvl + 1 < n_lvl:
                pw = _mm(pw, bd(pw), "nn", P_TINV)
        a_hat = x[:, :LANES]
        v_hat = x[:, LANES:]
        rhs2 = jnp.concatenate([
            jnp.concatenate([bd(a_hat), bd(v_hat)], axis=1),
            jnp.concatenate([zeros_sq, bd_v], axis=1)], axis=0)
        z = _mm(jnp.concatenate([m_rb, m_rk], axis=1), rhs2, "nn", P_INTRA)
        r_hat = r_t[:, sl] + z[:, :LANES]
        s_old = sbd_ref[q]
        t1 = _mm(jnp.concatenate([a_hat, r_hat], axis=0), s_old, "nt", P_STATE)
        u = t1[:C] + v_hat
        ys.append(t1[C:] + z[:, LANES:])
        uv = jnp.concatenate([u, vq], axis=0)
        bkh = jnp.concatenate([b_h[:, sl], k_h[:, sl]], axis=0)
        ds = _mm(uv.T, bkh, "nn", P_STATE)
        sbd_ref[q] = s_old * w_c[:, sl] + jnp.where(same_head, ds, 0.0)

    y = jnp.concatenate(ys, axis=1)
    mean = headsum(y) * (1.0 / RW_N)
    d = y - mean
    var = headsum(d * d) * (1.0 / RW_N)
    y = d * lax.rsqrt(var + GN_EPS) * lng_ref[...] + lnb_ref[...]
    y = y + headsum(r * k * rk_ref[...]) * v
    out_ref[...] = (y * g)[:t_in]

    @pl.when(c_idx == pl.num_programs(1) - 1)
    def _():
        sout_ref[0] = sbd_ref[...]


def _state_to_bd(s):
    b, h, n, _ = s.shape
    s = s.reshape(b, h // 2, 2, n, n)
    z = jnp.zeros_like(s[:, :, 0])
    top = jnp.concatenate([s[:, :, 0], z], axis=-1)
    bot = jnp.concatenate([z, s[:, :, 1]], axis=-1)
    return jnp.concatenate([top, bot], axis=-2)


def _state_from_bd(sbd):
    b, hp, n2, _ = sbd.shape
    n = n2 // 2
    return jnp.stack([sbd[:, :, :n, :n], sbd[:, :, n:, n:]], axis=2).reshape(b, hp * 2, n, n)


def _rwkv_mixer(rw, batch, seq, shift0, wkv0, mu, w0, w2, a0, a2, g2, k_k, k_a, r_k, lnx_g, lnx_b):
    n_shift = rw.shape[1]
    lora_w, lora_a = w2.shape[0], a2.shape[0]
    assert lora_w + lora_a == LANES and g2.shape[0] == LANES and n_shift == 3 * RW + 2 * LANES
    zero_init = shift0 is None
    t_in = min(CHUNK, seq)
    n_chunks = seq // t_in
    assert n_chunks * t_in == seq and (t_in == CHUNK or n_chunks == 1)
    n_pairs = RW // LANES
    w2a2 = jnp.zeros((LANES, 2 * RW), F32).at[:lora_w, :RW].set(w2).at[lora_w:, RW:].set(a2).astype(BF16)
    ltri = jnp.asarray(np.tril(np.ones((CHUNK, CHUNK), np.float32)), BF16)
    ones_bd = jnp.asarray(np.kron(np.eye(2, dtype=np.float32), np.ones((RW_N, RW_N), np.float32)), BF16)
    if zero_init:
        shift0 = jnp.zeros((1, 1, n_shift), F32)
        s0 = jnp.zeros((1, n_pairs, LANES, LANES), F32)
        bsel = lambda b, c: (0, 0, 0)
        ssel = lambda b, c: (0, 0, 0, 0)
    else:
        shift0 = shift0.reshape(batch, 1, n_shift)
        s0 = _state_to_bd(wkv0)
        bsel = lambda b, c: (b, 0, 0)
        ssel = lambda b, c: (b, 0, 0, 0)
    row = lambda x: x.reshape(1, -1).astype(F32)
    const = lambda shape: pl.BlockSpec(shape, lambda b, c: (0,) * len(shape))
    out, s_new = pl.pallas_call(
        functools.partial(_wkv_kernel, n_valid=t_in, zero_init=zero_init, lora_w=lora_w),
        grid=(batch, n_chunks),
        in_specs=[
            pl.BlockSpec((t_in, n_shift), lambda b, c: (b * n_chunks + c, 0)),
            pl.BlockSpec((1, 1, n_shift), bsel),
            pl.BlockSpec((1, n_pairs, LANES, LANES), ssel),
            const((1, n_shift)), const((LANES, 2 * RW)), const((1, RW)), const((1, RW)), const((LANES, RW)),
            const((1, RW)), const((1, RW)), const((1, RW)), const((1, RW)), const((1, RW)),
            const((CHUNK, CHUNK)), const((LANES, LANES)),
        ],
        out_specs=[
            pl.BlockSpec((t_in, RW), lambda b, c: (b * n_chunks + c, 0)),
            pl.BlockSpec((1, n_pairs, LANES, LANES), lambda b, c: (b, 0, 0, 0)),
        ],
        out_shape=[
            jax.ShapeDtypeStruct((batch * seq, RW), F32),
            jax.ShapeDtypeStruct((batch, n_pairs, LANES, LANES), F32),
        ],
        scratch_shapes=[pltpu.VMEM((1, n_shift), F32), pltpu.VMEM((n_pairs, LANES, LANES), F32)],
        compiler_params=_cparams(("arbitrary", "arbitrary")),
        name="rwkv7",
    )(rw, shift0, s0, row(mu), w2a2, row(w0), row(a0), g2.astype(BF16), row(k_k), row(k_a), row(r_k),
      row(lnx_g), row(lnx_b), ltri, ones_bd)
    return out, _state_from_bd(s_new)


def _mix_kernel(x_ref, att_ref, rwo_ref, gate_ref, wpa_ref, wpb_ref, wo_ref, o_ref):
    d = x_ref.shape[1]
    pa = jnp.dot(att_ref[...].astype(BF16), wpa_ref[...], preferred_element_type=F32)
    pb = jnp.dot(rwo_ref[...].astype(BF16), wpb_ref[...], preferred_element_type=F32)
    mix = gate_ref[:, :d] * pa + gate_ref[:, d:] * pb
    o_ref[...] = x_ref[...] + jnp.dot(mix.astype(BF16), wo_ref[...], preferred_element_type=F32)


def _mix(x2d, att, rwo, gates, w_pa, w_pb, w_o, tm):
    n, d = x2d.shape
    rows = lambda w: pl.BlockSpec((tm, w), lambda i: (i, 0))
    full = lambda a: _resident(a.shape)
    return pl.pallas_call(
        _mix_kernel,
        grid=(n // tm,),
        in_specs=[rows(d), rows(att.shape[1]), rows(rwo.shape[1]), rows(gates.shape[1]),
                  full(w_pa), full(w_pb), full(w_o)],
        out_specs=rows(d),
        out_shape=jax.ShapeDtypeStruct((n, d), F32),
        compiler_params=_cparams(("parallel",)),
        name="branch_mix",
    )(x2d, att, rwo, gates, w_pa, w_pb, w_o)


def _rms(x, g):
    ms = jnp.mean(x * x, axis=-1, keepdims=True)
    return x * lax.rsqrt(ms + NORM_EPS) * g


def _gelu_tanh(c):
    return c * (0.5 * (1.0 + jnp.tanh(math.sqrt(2.0 / math.pi) * (c + 0.044715 * (c * c * c)))))


def _conv_ffn_cols(h, x, wup_ref, cw_ref, cb_ref, wdown_ref, shifted, ug_sink):
    d_ff = cb_ref.shape[1]
    wc = d_ff // FFN_COL_CHUNKS
    acc = x
    for j in range(FFN_COL_CHUNKS):
        cs = slice(j * wc, (j + 1) * wc)
        ug_full = jnp.dot(h, wup_ref[:, j * wc:(j + 1) * wc], preferred_element_type=F32)
        uv_full = jnp.dot(h, wup_ref[:, d_ff + j * wc:d_ff + (j + 1) * wc], preferred_element_type=F32)
        ug, ug_m1, ug_m2, uv = shifted(ug_full, uv_full, cs)
        c = cb_ref[:, cs] + cw_ref[0:1, cs] * ug_m2 + cw_ref[1:2, cs] * ug_m1 + cw_ref[2:3, cs] * ug
        act = (_gelu_tanh(c) * uv).astype(BF16)
        acc = acc + jnp.dot(act, wdown_ref[j * wc:(j + 1) * wc, :], preferred_element_type=F32)
        ug_sink(ug, cs)
    return acc


def _ffn_prompt_kernel(x_ref, halo_ref, g2_ref, wup_ref, cw_ref, cb_ref, wdown_ref, gf_ref, y_ref, ug_ref):
    tm = x_ref.shape[0]
    x = x_ref[...]
    xe = jnp.concatenate([halo_ref[...], x], axis=0)
    h = _rms(xe, g2_ref[...]).astype(BF16)
    row = lax.broadcasted_iota(jnp.int32, (SUBLANES + tm, 1), 0)
    before_start = jnp.logical_and(pl.program_id(1) == 0, row < SUBLANES)

    def shifted(ug_e, uv_e, cs):
        ug_e = jnp.where(before_start, 0.0, ug_e)
        return (ug_e[SUBLANES:], pltpu.roll(ug_e, 1, 0)[SUBLANES:], pltpu.roll(ug_e, 2, 0)[SUBLANES:],
                uv_e[SUBLANES:])

    def ug_sink(ug, cs):
        ug_ref[:, cs] = ug[tm - SUBLANES:]

    x2 = _conv_ffn_cols(h, x, wup_ref, cw_ref, cb_ref, wdown_ref, shifted, ug_sink)
    y_ref[...] = _rms(x2, gf_ref[...])


def _ffn_sample_kernel(x_ref, e_ref, g2_ref, wup_ref, cw_ref, cb_ref, wdown_ref, gf_ref, y_ref, ug_ref, *, seq):
    rows = x_ref.shape[0]
    x = x_ref[...]
    h = _rms(x, g2_ref[...]).astype(BF16)
    t = lax.broadcasted_iota(jnp.int32, (rows, 1), 0) % seq

    def shifted(ug, uv, cs):
        e = e_ref[:, cs]
        ug_m1 = jnp.where(t == 0, pltpu.roll(e, rows - 1, 0), pltpu.roll(ug, 1, 0))
        ug_m2 = jnp.where(t < 2, e, pltpu.roll(ug, 2, 0))
        return ug, ug_m1, ug_m2, uv

    def ug_sink(ug, cs):
        ug_ref[:, cs] = ug

    x2 = _conv_ffn_cols(h, x, wup_ref, cw_ref, cb_ref, wdown_ref, shifted, ug_sink)
    y_ref[...] = _rms(x2, gf_ref[...])


def _ffn_prompt(x1, batch, seq, norm2_g, w_up, conv_w, conv_b, w_down, final_g, tm):
    n, d = x1.shape
    d_ff = conv_b.shape[0]
    nt = seq // tm
    full = lambda a: _resident(a.shape)
    g2, cb, gf = norm2_g.reshape(1, d), conv_b.reshape(1, d_ff), final_g.reshape(1, d)
    hb = tm // SUBLANES
    y, ug_last = pl.pallas_call(
        _ffn_prompt_kernel,
        grid=(batch, nt),
        in_specs=[
            pl.BlockSpec((tm, d), lambda b, i: (b * nt + i, 0)),
            pl.BlockSpec((SUBLANES, d), lambda b, i: (jnp.maximum((b * nt + i) * hb - 1, 0), 0)),
            full(g2), full(w_up), full(conv_w), full(cb), full(w_down), full(gf),
        ],
        out_specs=[
            pl.BlockSpec((tm, d), lambda b, i: (b * nt + i, 0)),
            pl.BlockSpec((SUBLANES, d_ff), lambda b, i: (b, 0)),
        ],
        out_shape=[
            jax.ShapeDtypeStruct((n, d), F32),
            jax.ShapeDtypeStruct((batch * SUBLANES, d_ff), F32),
        ],
        compiler_params=_cparams(("arbitrary", "arbitrary")),
        name="conv_ffn_prompt",
    )(x1, x1, g2, w_up, conv_w, cb, w_down, gf)
    conv_new = ug_last.reshape(batch, SUBLANES, d_ff)[:, SUBLANES - (CONV_W - 1):]
    return y, conv_new


def _ffn_sample(x1, batch, seq, conv0, norm2_g, w_up, conv_w, conv_b, w_down, final_g, bt):
    n, d = x1.shape
    d_ff = conv_b.shape[0]
    rows = bt * seq
    e = jnp.pad(conv0, ((0, 0), (0, seq - (CONV_W - 1)), (0, 0))).reshape(n, d_ff)
    full = lambda a: _resident(a.shape)
    g2, cb, gf = norm2_g.reshape(1, d), conv_b.reshape(1, d_ff), final_g.reshape(1, d)
    y, ug = pl.pallas_call(
        functools.partial(_ffn_sample_kernel, seq=seq),
        grid=(n // rows,),
        in_specs=[
            pl.BlockSpec((rows, d), lambda i: (i, 0)),
            pl.BlockSpec((rows, d_ff), lambda i: (i, 0)),
            full(g2), full(w_up), full(conv_w), full(cb), full(w_down), full(gf),
        ],
        out_specs=[
            pl.BlockSpec((rows, d), lambda i: (i, 0)),
            pl.BlockSpec((rows, d_ff), lambda i: (i, 0)),
        ],
        out_shape=[jax.ShapeDtypeStruct((n, d), F32), jax.ShapeDtypeStruct((n, d_ff), F32)],
        compiler_params=_cparams(("parallel",)),
        name="conv_ffn_sample",
    )(x1, e, g2, w_up, conv_w, cb, w_down, gf)
    conv_new = ug.reshape(batch, seq, d_ff)[:, seq - (CONV_W - 1):]
    return y, conv_new


def _row_tile(n, want):
    t = min(want, n)
    while n % t:
        t //= 2
    return t


def _layer(x, is_prompt, state, rel_bias, lw, final_g):
    batch, seq, d = x.shape
    n = batch * seq
    x2d = x.reshape(n, d)
    n_qkv = Q_W + 2 * KV_W
    n_rw = lw["mu_shift"].shape[0]
    tm = _row_tile(n, 512)
    qkv, rw, gates = _in_proj(x2d, lw["norm1_g"], lw["w_in"], n_qkv, n_rw, tm)
    kv = qkv.reshape(batch, seq, n_qkv)
    k_new = kv[:, :, Q_W:Q_W + KV_W]
    v_new = kv[:, :, Q_W + KV_W:]
    if is_prompt:
        att = _attention_prompt(qkv, batch, seq, rel_bias, lw["sinks"])
        wb = state["win_buf"]
        k_rows, v_rows = k_new[:, seq - wb:], v_new[:, seq - wb:]
        shift0 = wkv0 = None
    else:
        cache_k, cache_v = state["cache_k"], state["cache_v"]
        wb = cache_k.shape[1]
        att = _attention_sample(qkv, batch, seq, cache_k, cache_v, rel_bias, lw["sinks"], _row_tile(batch, 16))
        k_rows = jnp.concatenate([cache_k.reshape(batch, wb, KV_W), k_new], axis=1)[:, seq:]
        v_rows = jnp.concatenate([cache_v.reshape(batch, wb, KV_W), v_new], axis=1)[:, seq:]
        shift0, wkv0 = state["shift"], state["wkv"]
    k_rows = k_rows.reshape(batch, -1, N_KV, HEAD_DIM)
    v_rows = v_rows.reshape(batch, -1, N_KV, HEAD_DIM)
    rwo, wkv_new = _rwkv_mixer(rw, batch, seq, shift0, wkv0, lw["mu_shift"], lw["w0"], lw["w2"], lw["a0"],
                               lw["a2"], lw["g2"], lw["k_k"], lw["k_a"], lw["r_k"], lw["lnx_g"], lw["lnx_b"])
    shift_new = rw.reshape(batch, seq, n_rw)[:, seq - 1]
    x1 = _mix(x2d, att, rwo, gates, lw["w_pa"], lw["w_pb"], lw["w_o"], tm)
    if is_prompt:
        y, conv_new = _ffn_prompt(x1, batch, seq, lw["norm2_g"], lw["w_up"], lw["conv_w"], lw["conv_b"],
                                  lw["w_down"], final_g, _row_tile(seq, 512))
    else:
        y, conv_new = _ffn_sample(x1, batch, seq, state["conv"], lw["norm2_g"], lw["w_up"], lw["conv_w"],
                                  lw["conv_b"], lw["w_down"], final_g, _row_tile(batch, 64))
    return y.reshape(batch, seq, d), (k_rows, v_rows, shift_new, wkv_new, conv_new)


def kernel(x_prompt, x_sample, cache_win_k, cache_win_v, state_shift, state_wkv, state_conv, rel_bias, norm1_g,
           w_in, sinks, mu_shift, w0, w2, a0, a2, g2, k_k, k_a, r_k, lnx_g, lnx_b, w_pa, w_pb, w_o, norm2_g,
           w_up, conv_w, conv_b, w_down, final_g):
    depth = w_in.shape[0]
    assert depth == 1, "the final norm is fused into the layer's last kernel"
    l = 0
    lw = dict(norm1_g=norm1_g[l], w_in=w_in[l].astype(BF16), sinks=sinks[l], mu_shift=mu_shift[l], w0=w0[l],
              w2=w2[l], a0=a0[l], a2=a2[l], g2=g2[l], k_k=k_k[l], k_a=k_a[l], r_k=r_k[l].reshape(-1),
              lnx_g=lnx_g[l], lnx_b=lnx_b[l], w_pa=w_pa[l].astype(BF16), w_pb=w_pb[l].astype(BF16),
              w_o=w_o[l].astype(BF16), norm2_g=norm2_g[l], w_up=w_up[l].astype(BF16), conv_w=conv_w[l],
              conv_b=conv_b[l], w_down=w_down[l].astype(BF16))
    win_buf = cache_win_k.shape[2]
    y_p, st_p = _layer(x_prompt, True, dict(win_buf=win_buf), rel_bias, lw, final_g)
    y_s, st_s = _layer(x_sample, False,
                       dict(cache_k=cache_win_k[l], cache_v=cache_win_v[l], shift=state_shift[l],
                            wkv=state_wkv[l], conv=state_conv[l]), rel_bias, lw, final_g)
    stack = lambda t: t[None]
    return (y_p, y_s) + tuple(stack(t) for t in st_p) + tuple(stack(t) for t in st_s)
```

```python
import functools
import math

import numpy as np
import jax
import jax.numpy as jnp
from jax import lax
from jax.experimental import pallas as pl
from jax.experimental.pallas import tpu as pltpu

F32 = jnp.float32
BF16 = jnp.bfloat16

HEAD_DIM = 64
N_HEADS = 8
N_KV = 2
WINDOW = 128
N_BUCKETS = 32
MAX_EXACT = N_BUCKETS // 2
REL_MAX_DIST = 128
RW_N = 64
RW_HEADS = 8
RW = RW_HEADS * RW_N
NORM_EPS = 1e-6
GN_EPS = 64e-5
NEG = -1e30
CONV_W = 3

Q_W = N_HEADS * HEAD_DIM
KV_W = N_KV * HEAD_DIM
LANES = 128
SUBLANES = 8
CHUNK = 64
VMEM_LIMIT = 56 * 1024 * 1024
FFN_COL_CHUNKS = 2


def _resident(shape):
    return pl.BlockSpec(shape, lambda *_: (0,) * len(shape), pipeline_mode=pl.Buffered(1))


def _cparams(sem):
    return pltpu.CompilerParams(dimension_semantics=sem, vmem_limit_bytes=VMEM_LIMIT)


def _sigmoid(x):
    return 1.0 / (1.0 + jnp.exp(-x))


def _dg(a, b, kind):
    if kind == "nn":
        dn = (((1,), (0,)), ((), ()))
    else:
        dn = (((1,), (1,)), ((), ()))
    return lax.dot_general(a, b, dn, preferred_element_type=F32)


def _split(x):
    hi = x.astype(BF16)
    lo = (x - hi.astype(F32)).astype(BF16)
    return hi, lo


def _mm(a, b, kind="nn", passes=1):
    if passes == 1:
        return _dg(a.astype(BF16), b.astype(BF16), kind)
    ah, al = _split(a)
    bh, bl = _split(b)
    return _dg(ah, bh, kind) + (_dg(ah, bl, kind) + _dg(al, bh, kind))


def _mm_exact_lhs(a_bf16, b, n_terms):
    out = None
    rem = b
    for _ in range(n_terms):
        piece = rem.astype(BF16)
        term = _dg(a_bf16, piece, "nn")
        out = term if out is None else out + term
        rem = rem - piece.astype(F32)
    return out


def _mm_exact_rhs(a, b_bf16, n_terms):
    out = None
    rem = a
    for _ in range(n_terms):
        piece = rem.astype(BF16)
        term = _dg(piece, b_bf16, "nn")
        out = term if out is None else out + term
        rem = rem - piece.astype(F32)
    return out


def _inproj_kernel(x_ref, g_ref, w_ref, qkv_ref, rw_ref, gate_ref, *, n_qkv, n_rw):
    x = x_ref[...]
    ms = jnp.mean(x * x, axis=-1, keepdims=True)
    h = (x * lax.rsqrt(ms + NORM_EPS) * g_ref[...]).astype(BF16)
    qkv_ref[...] = jnp.dot(h, w_ref[:, :n_qkv], preferred_element_type=F32)
    rw_ref[...] = jnp.dot(h, w_ref[:, n_qkv:n_qkv + n_rw], preferred_element_type=F32)
    gate_ref[...] = _sigmoid(jnp.dot(h, w_ref[:, n_qkv + n_rw:], preferred_element_type=F32))


def _in_proj(x2d, g, w_bf16, n_qkv, n_rw, tm):
    n, d = x2d.shape
    n_gate = w_bf16.shape[1] - n_qkv - n_rw
    return pl.pallas_call(
        functools.partial(_inproj_kernel, n_qkv=n_qkv, n_rw=n_rw),
        grid=(n // tm,),
        in_specs=[
            pl.BlockSpec((tm, d), lambda i: (i, 0)),
            _resident((1, d)),
            _resident(w_bf16.shape),
        ],
        out_specs=[
            pl.BlockSpec((tm, n_qkv), lambda i: (i, 0)),
            pl.BlockSpec((tm, n_rw), lambda i: (i, 0)),
            pl.BlockSpec((tm, n_gate), lambda i: (i, 0)),
        ],
        out_shape=[
            jax.ShapeDtypeStruct((n, n_qkv), F32),
            jax.ShapeDtypeStruct((n, n_rw), F32),
            jax.ShapeDtypeStruct((n, n_gate), F32),
        ],
        compiler_params=_cparams(("parallel",)),
        name="in_proj",
    )(x2d, g.reshape(1, d), w_bf16)


def _t5_bucket_np(dist):
    n = np.maximum(dist, 0)
    nf = np.maximum(n, 1).astype(np.float32)
    large = MAX_EXACT + (np.log(nf / MAX_EXACT) / math.log(REL_MAX_DIST / MAX_EXACT)
                         * (N_BUCKETS - MAX_EXACT)).astype(np.int32)
    return np.where(n < MAX_EXACT, n, np.minimum(large, N_BUCKETS - 1)).astype(np.int32)


def _attn_kernel(q_ref, k1_ref, k2_ref, v1_ref, v2_ref, bucket_ref, relb_ref, sink_ref, o_ref, bias_ref,
                 *, nq, nk, first_block_axis):
    bt = q_ref.shape[0]
    first_step = pl.program_id(0) == 0
    if first_block_axis is not None:
        first_step = jnp.logical_and(first_step, pl.program_id(1) == 0)

    @pl.when(first_step)
    def _():
        bucket = bucket_ref[...]
        for n in range(N_HEADS):
            acc = jnp.full((nq, nk), NEG, F32)
            for b in range(N_BUCKETS):
                acc = jnp.where(bucket == b, relb_ref[b, n], acc)
            c, half = divmod(n, 2)
            bias_ref[c, :, half * nk:(half + 1) * nk] = acc

    def padded(a_ref, b_ref):
        parts = [a_ref[...], b_ref[...]]
        n_now = a_ref.shape[1] + b_ref.shape[1]
        if n_now < nk:
            parts.append(jnp.zeros((bt, nk - n_now, LANES), F32))
        return jnp.concatenate(parts, axis=1)

    kk = padded(k1_ref, k2_ref)
    vv = padded(v1_ref, v2_ref)
    kk_r = pltpu.roll(kk, HEAD_DIM, 2)
    vv_r = pltpu.roll(vv, HEAD_DIM, 2)
    lane = lax.broadcasted_iota(jnp.int32, (1, 1, LANES), 2)
    lo = lane < HEAD_DIM

    def halves(x, x_r, kvh):
        src_lo, src_hi = (x, x_r) if kvh == 0 else (x_r, x)
        even = jnp.where(lo, src_lo, 0.0).astype(BF16)
        odd = jnp.where(lo, 0.0, src_hi).astype(BF16)
        return jnp.concatenate([even, odd], axis=1)

    kcat = [halves(kk, kk_r, h) for h in range(N_KV)]
    vcat = [halves(vv, vv_r, h) for h in range(N_KV)]

    key_idx = lax.broadcasted_iota(jnp.int32, (1, 1, 2 * nk), 2)
    scale = HEAD_DIM ** -0.5
    for c in range(N_HEADS // 2):
        kvh = (2 * c) // (N_HEADS // N_KV)
        qc = q_ref[:, :, c * LANES:(c + 1) * LANES].astype(BF16)
        s = jnp.einsum("bqd,bkd->bqk", qc, kcat[kvh], preferred_element_type=F32) * scale
        bias = bias_ref[c][None]
        readable = bias > 0.5 * NEG
        if first_block_axis is not None:
            no_prev = pl.program_id(first_block_axis) == 0
            prev_key = (key_idx % nk) < (nk // 2)
            readable = jnp.logical_and(readable, jnp.logical_not(jnp.logical_and(no_prev, prev_key)))
        s = jnp.where(readable, s + bias, NEG)
        outs = []
        es = []
        invs = []
        for half in range(2):
            n = 2 * c + half
            sh = s[:, :, half * nk:(half + 1) * nk]
            sink = sink_ref[n]
            m = jnp.maximum(jnp.max(sh, axis=-1, keepdims=True), sink)
            e = jnp.exp(sh - m)
            denom = jnp.sum(e, axis=-1, keepdims=True) + jnp.exp(sink - m)
            es.append(e.astype(BF16))
            invs.append(1.0 / denom)
        e_cat = jnp.concatenate(es, axis=2)
        o = jnp.einsum("bqk,bkd->bqd", e_cat, vcat[kvh], preferred_element_type=F32)
        inv = jnp.where(lo, invs[0], invs[1])
        o_ref[:, :, c * LANES:(c + 1) * LANES] = o * inv


def _attention_prompt(qkv, batch, seq, rel_bias, sinks):
    nblk = seq // WINDOW
    q3 = qkv.reshape(batch * nblk, WINDOW, qkv.shape[1])
    kcol = Q_W // LANES
    vcol = (Q_W + KV_W) // LANES
    nk = 2 * WINDOW
    qi = np.arange(WINDOW)[:, None] + WINDOW
    kj = np.arange(nk)[None, :]
    dist = qi - kj
    bucket = np.where((dist >= 0) & (dist < WINDOW), _t5_bucket_np(dist), -1).astype(np.int32)

    def cur(b, i):
        return (b * nblk + i, 0, 0)

    def prev(b, i):
        return (b * nblk + jnp.maximum(i - 1, 0), 0, 0)

    def col(f, c):
        return lambda b, i: f(b, i)[:2] + (c,)

    out = pl.pallas_call(
        functools.partial(_attn_kernel, nq=WINDOW, nk=nk, first_block_axis=1),
        grid=(batch, nblk),
        in_specs=[
            pl.BlockSpec((1, WINDOW, Q_W), cur),
            pl.BlockSpec((1, WINDOW, LANES), col(prev, kcol)),
            pl.BlockSpec((1, WINDOW, LANES), col(cur, kcol)),
            pl.BlockSpec((1, WINDOW, LANES), col(prev, vcol)),
            pl.BlockSpec((1, WINDOW, LANES), col(cur, vcol)),
            pl.BlockSpec(bucket.shape, lambda b, i: (0, 0)),
            pl.BlockSpec(memory_space=pltpu.SMEM),
            pl.BlockSpec(memory_space=pltpu.SMEM),
        ],
        out_specs=pl.BlockSpec((1, WINDOW, Q_W), cur),
        out_shape=jax.ShapeDtypeStruct((batch * nblk, WINDOW, Q_W), F32),
        scratch_shapes=[pltpu.VMEM((N_HEADS // 2, WINDOW, 2 * nk), F32)],
        compiler_params=_cparams(("arbitrary", "arbitrary")),
        name="attn_prompt",
    )(q3, q3, q3, q3, q3, jnp.asarray(bucket), rel_bias, sinks)
    return out.reshape(batch * seq, Q_W)


def _attention_sample(qkv, batch, seq, cache_k, cache_v, rel_bias, sinks, bt):
    wb = cache_k.shape[1]
    nk = 2 * WINDOW
    q3 = qkv.reshape(batch, seq, qkv.shape[1])
    ck = cache_k.reshape(batch, wb, KV_W)
    cv = cache_v.reshape(batch, wb, KV_W)
    kcol = Q_W // LANES
    vcol = (Q_W + KV_W) // LANES
    tq = np.arange(seq)[:, None]
    j = np.arange(nk)[None, :]
    dist = np.where(j < wb, tq + wb - j, tq - (j - wb))
    ok = (dist >= 0) & (dist < WINDOW) & (j < wb + seq)
    bucket = np.where(ok, _t5_bucket_np(dist), -1).astype(np.int32)
    out = pl.pallas_call(
        functools.partial(_attn_kernel, nq=seq, nk=nk, first_block_axis=None),
        grid=(batch // bt,),
        in_specs=[
            pl.BlockSpec((bt, seq, Q_W), lambda b: (b, 0, 0)),
            pl.BlockSpec((bt, wb, LANES), lambda b: (b, 0, 0)),
            pl.BlockSpec((bt, seq, LANES), lambda b: (b, 0, kcol)),
            pl.BlockSpec((bt, wb, LANES), lambda b: (b, 0, 0)),
            pl.BlockSpec((bt, seq, LANES), lambda b: (b, 0, vcol)),
            pl.BlockSpec(bucket.shape, lambda b: (0, 0)),
            pl.BlockSpec(memory_space=pltpu.SMEM),
            pl.BlockSpec(memory_space=pltpu.SMEM),
        ],
        out_specs=pl.BlockSpec((bt, seq, Q_W), lambda b: (b, 0, 0)),
        out_shape=jax.ShapeDtypeStruct((batch, seq, Q_W), F32),
        scratch_shapes=[pltpu.VMEM((N_HEADS // 2, seq, 2 * nk), F32)],
        compiler_params=_cparams(("arbitrary",)),
        name="attn_sample",
    )(q3, ck, q3, cv, q3, jnp.asarray(bucket), rel_bias, sinks)
    return out.reshape(batch * seq, Q_W)


P_SCORE = 1
P_TINV = 1
P_INTRA = 1
P_STATE = 1


def _wkv_kernel(p_ref, shift0_ref, s0_ref, mu_ref, w2a2_ref, w0_ref, a0_ref, g2_ref, kk_ref, ka_ref, rk_ref,
                lng_ref, lnb_ref, ltri_ref, ones_ref, out_ref, sout_ref, last_ref, sbd_ref,
                *, n_valid, zero_init, lora_w):
    C = CHUNK
    t_in = p_ref.shape[0]
    n_shift = p_ref.shape[1]
    n_pairs = RW // LANES
    c_idx = pl.program_id(1)

    @pl.when(c_idx == 0)
    def _():
        if zero_init:
            last_ref[...] = jnp.zeros_like(last_ref)
            sbd_ref[...] = jnp.zeros_like(sbd_ref)
        else:
            last_ref[...] = shift0_ref[0]
            sbd_ref[...] = s0_ref[0]

    p = p_ref[...]
    if t_in < C:
        p = jnp.concatenate([p, jnp.zeros((C - t_in, n_shift), F32)], axis=0)
    row = lax.broadcasted_iota(jnp.int32, (C, 1), 0)
    prev = jnp.where(row == 0, last_ref[...], pltpu.roll(p, 1, 0))
    last_ref[...] = p_ref[t_in - 1:t_in, :]
    xs = p + mu_ref[...] * (prev - p)

    r = xs[:, 0:RW]
    k = xs[:, RW:2 * RW]
    v = xs[:, 2 * RW:3 * RW]
    lwla = xs[:, 3 * RW:3 * RW + LANES]
    lg = xs[:, 3 * RW + LANES:3 * RW + 2 * LANES]
    lane = lax.broadcasted_iota(jnp.int32, (1, LANES), 1)
    lo = lane < RW_N
    lwla = jnp.where(lane < lora_w, jnp.tanh(lwla), lwla)
    wa = jnp.dot(lwla.astype(BF16), w2a2_ref[...], preferred_element_type=F32)
    logw = -math.exp(-0.5) * _sigmoid(w0_ref[...] + wa[:, :RW])
    a_sig = _sigmoid(a0_ref[...] + wa[:, RW:])
    g = jnp.dot(_sigmoid(lg).astype(BF16), g2_ref[...], preferred_element_type=F32)

    ones_bd = ones_ref[...]

    def headsum(x):
        stacked = jnp.concatenate([x[:, q * LANES:(q + 1) * LANES] for q in range(n_pairs)], axis=0)
        s = _mm_exact_rhs(stacked, ones_bd, 2)
        return jnp.concatenate([s[q * C:(q + 1) * C] for q in range(n_pairs)], axis=1)

    kk = k * kk_ref[...]
    kk = kk * (1.0 / jnp.maximum(jnp.sqrt(headsum(kk * kk)), 1e-12))
    k = k * (1.0 + (a_sig - 1.0) * ka_ref[...])
    a = -kk
    b = kk * a_sig
    if n_valid < C:
        ok = row < n_valid
        r, k, v, a, b = (jnp.where(ok, t, 0.0) for t in (r, k, v, a, b))
        logw = jnp.where(ok, logw, 0.0)

    cw = _mm_exact_lhs(ltri_ref[...], logw, 3)
    cw_last = cw[C - 1:C, :]
    w_incl = jnp.exp(cw)
    w_inv = jnp.exp(-cw)
    w_tail = jnp.exp(cw_last - cw)
    a_t = a * jnp.exp(cw - logw)
    r_t = r * w_incl
    b_t = b * w_inv
    k_t = k * w_inv
    b_h = b * w_tail
    k_h = k * w_tail
    w_c = jnp.exp(cw_last)

    def bd(y):
        return jnp.concatenate([jnp.where(lo, y, 0.0), jnp.where(lo, 0.0, y)], axis=0)

    s_idx = lane % C
    strict = s_idx < row
    incl = s_idx <= row
    row2 = lax.broadcasted_iota(jnp.int32, (2 * C, 1), 0)
    same_head = (row2 < RW_N) == lo
    zeros_sq = jnp.zeros((2 * C, LANES), F32)

    pairs = range(n_pairs)
    sls = [slice(q * LANES, (q + 1) * LANES) for q in pairs]
    sc = [_mm(jnp.concatenate([a_t[:, sl], r_t[:, sl]], axis=0),
              jnp.concatenate([bd(b_t[:, sl]), bd(k_t[:, sl])], axis=0),
              "nt", P_SCORE) for sl in sls]
    pw = [jnp.where(strict, s[:C, :LANES], 0.0) for s in sc]
    l_ak = [jnp.where(strict, s[:C, LANES:], 0.0) for s in sc]
    m_rbk = [jnp.concatenate([jnp.where(incl, s[C:, :LANES], 0.0), jnp.where(incl, s[C:, LANES:], 0.0)], axis=1)
             for s in sc]
    bd_v = [bd(v[:, sl]) for sl in sls]
    x = [jnp.concatenate([a_t[:, sl], _mm(l, bv, "nn", P_INTRA)], axis=1)
         for sl, l, bv in zip(sls, l_ak, bd_v)]
    n_lvl = int(math.log2(C))
    for lvl in range(n_lvl):
        last = lvl + 1 == n_lvl
        for q in pairs:
            parts = [bd(x[q][:, :LANES]), bd(x[q][:, LANES:])]
            if not last:
                parts.append(bd(pw[q]))
            upd = _mm(pw[q], jnp.concatenate(parts, axis=1), "nn", P_TINV)
            x[q] = x[q] + upd[:, :2 * LANES]
            if not last:
                pw[q] = upd[:, 2 * LANES:]
    a_hat = [xq[:, :LANES] for xq in x]
    v_hat = [xq[:, LANES:] for xq in x]
    z = [_mm(m_rbk[q],
             jnp.concatenate([jnp.concatenate([bd(a_hat[q]), bd(v_hat[q])], axis=1),
                              jnp.concatenate([zeros_sq, bd_v[q]], axis=1)], axis=0),
             "nn", P_INTRA) for q in pairs]
    s_old = [sbd_ref[q] for q in pairs]
    t1 = [_mm(jnp.concatenate([a_hat[q], r_t[:, sls[q]] + z[q][:, :LANES]], axis=0), s_old[q], "nt", P_STATE)
          for q in pairs]
    ys = [t1[q][C:] + z[q][:, LANES:] for q in pairs]
    for q in pairs:
        uv = jnp.concatenate([t1[q][:C] + v_hat[q], v[:, sls[q]]], axis=0)
        bkh = jnp.concatenate([b_h[:, sls[q]], k_h[:, sls[q]]], axis=0)
        ds = _mm(uv.T, bkh, "nn", P_STATE)
        sbd_ref[q] = s_old[q] * w_c[:, sls[q]] + jnp.where(same_head, ds, 0.0)

    y = jnp.concatenate(ys, axis=1)
    mean = headsum(y) * (1.0 / RW_N)
    d = y - mean
    var = headsum(d * d) * (1.0 / RW_N)
    y = d * lax.rsqrt(var + GN_EPS) * lng_ref[...] + lnb_ref[...]
    y = y + headsum(r * k * rk_ref[...]) * v
    out_ref[...] = (y * g)[:t_in]

    @pl.when(c_idx == pl.num_programs(1) - 1)
    def _():
        sout_ref[0] = sbd_ref[...]


def _state_to_bd(s):
    b, h, n, _ = s.shape
    s = s.reshape(b, h // 2, 2, n, n)
    z = jnp.zeros_like(s[:, :, 0])
    top = jnp.concatenate([s[:, :, 0], z], axis=-1)
    bot = jnp.concatenate([z, s[:, :, 1]], axis=-1)
    return jnp.concatenate([top, bot], axis=-2)


def _state_from_bd(sbd):
    b, hp, n2, _ = sbd.shape
    n = n2 // 2
    return jnp.stack([sbd[:, :, :n, :n], sbd[:, :, n:, n:]], axis=2).reshape(b, hp * 2, n, n)


def _rwkv_mixer(rw, batch, seq, shift0, wkv0, mu, w0, w2, a0, a2, g2, k_k, k_a, r_k, lnx_g, lnx_b):
    n_shift = rw.shape[1]
    lora_w, lora_a = w2.shape[0], a2.shape[0]
    assert lora_w + lora_a == LANES and g2.shape[0] == LANES and n_shift == 3 * RW + 2 * LANES
    zero_init = shift0 is None
    t_in = min(CHUNK, seq)
    n_chunks = seq // t_in
    assert n_chunks * t_in == seq and (t_in == CHUNK or n_chunks == 1)
    n_pairs = RW // LANES
    w2a2 = jnp.zeros((LANES, 2 * RW), F32).at[:lora_w, :RW].set(w2).at[lora_w:, RW:].set(a2).astype(BF16)
    ltri = jnp.asarray(np.tril(np.ones((CHUNK, CHUNK), np.float32)), BF16)
    ones_bd = jnp.asarray(np.kron(np.eye(2, dtype=np.float32), np.ones((RW_N, RW_N), np.float32)), BF16)
    if zero_init:
        shift0 = jnp.zeros((1, 1, n_shift), F32)
        s0 = jnp.zeros((1, n_pairs, LANES, LANES), F32)
        bsel = lambda b, c: (0, 0, 0)
        ssel = lambda b, c: (0, 0, 0, 0)
    else:
        shift0 = shift0.reshape(batch, 1, n_shift)
        s0 = _state_to_bd(wkv0)
        bsel = lambda b, c: (b, 0, 0)
        ssel = lambda b, c: (b, 0, 0, 0)
    row = lambda x: x.reshape(1, -1).astype(F32)
    const = lambda shape: pl.BlockSpec(shape, lambda b, c: (0,) * len(shape))
    out, s_new = pl.pallas_call(
        functools.partial(_wkv_kernel, n_valid=t_in, zero_init=zero_init, lora_w=lora_w),
        grid=(batch, n_chunks),
        in_specs=[
            pl.BlockSpec((t_in, n_shift), lambda b, c: (b * n_chunks + c, 0)),
            pl.BlockSpec((1, 1, n_shift), bsel),
            pl.BlockSpec((1, n_pairs, LANES, LANES), ssel),
            const((1, n_shift)), const((LANES, 2 * RW)), const((1, RW)), const((1, RW)), const((LANES, RW)),
            const((1, RW)), const((1, RW)), const((1, RW)), const((1, RW)), const((1, RW)),
            const((CHUNK, CHUNK)), const((LANES, LANES)),
        ],
        out_specs=[
            pl.BlockSpec((t_in, RW), lambda b, c: (b * n_chunks + c, 0)),
            pl.BlockSpec((1, n_pairs, LANES, LANES), lambda b, c: (b, 0, 0, 0)),
        ],
        out_shape=[
            jax.ShapeDtypeStruct((batch * seq, RW), F32),
            jax.ShapeDtypeStruct((batch, n_pairs, LANES, LANES), F32),
        ],
        scratch_shapes=[pltpu.VMEM((1, n_shift), F32), pltpu.VMEM((n_pairs, LANES, LANES), F32)],
        compiler_params=_cparams(("arbitrary", "arbitrary")),
        name="rwkv7",
    )(rw, shift0, s0, row(mu), w2a2, row(w0), row(a0), g2.astype(BF16), row(k_k), row(k_a), row(r_k),
      row(lnx_g), row(lnx_b), ltri, ones_bd)
    return out, _state_from_bd(s_new)


def _mix_kernel(x_ref, att_ref, rwo_ref, gate_ref, wpa_ref, wpb_ref, wo_ref, o_ref):
    d = x_ref.shape[1]
    pa = jnp.dot(att_ref[...].astype(BF16), wpa_ref[...], preferred_element_type=F32)
    pb = jnp.dot(rwo_ref[...].astype(BF16), wpb_ref[...], preferred_element_type=F32)
    mix = gate_ref[:, :d] * pa + gate_ref[:, d:] * pb
    o_ref[...] = x_ref[...] + jnp.dot(mix.astype(BF16), wo_ref[...], preferred_element_type=F32)


def _mix(x2d, att, rwo, gates, w_pa, w_pb, w_o, tm):
    n, d = x2d.shape
    rows = lambda w: pl.BlockSpec((tm, w), lambda i: (i, 0))
    full = lambda a: _resident(a.shape)
    return pl.pallas_call(
        _mix_kernel,
        grid=(n // tm,),
        in_specs=[rows(d), rows(att.shape[1]), rows(rwo.shape[1]), rows(gates.shape[1]),
                  full(w_pa), full(w_pb), full(w_o)],
        out_specs=rows(d),
        out_shape=jax.ShapeDtypeStruct((n, d), F32),
        compiler_params=_cparams(("parallel",)),
        name="branch_mix",
    )(x2d, att, rwo, gates, w_pa, w_pb, w_o)


def _rms(x, g):
    ms = jnp.mean(x * x, axis=-1, keepdims=True)
    return x * lax.rsqrt(ms + NORM_EPS) * g


def _gelu_tanh(c):
    return c * (0.5 * (1.0 + jnp.tanh(math.sqrt(2.0 / math.pi) * (c + 0.044715 * (c * c * c)))))


def _conv_ffn_cols(h, x, wup_ref, cw_ref, cb_ref, wdown_ref, shifted, ug_sink):
    d_ff = cb_ref.shape[1]
    wc = d_ff // FFN_COL_CHUNKS
    acc = x
    for j in range(FFN_COL_CHUNKS):
        cs = slice(j * wc, (j + 1) * wc)
        ug_full = jnp.dot(h, wup_ref[:, j * wc:(j + 1) * wc], preferred_element_type=F32)
        uv_full = jnp.dot(h, wup_ref[:, d_ff + j * wc:d_ff + (j + 1) * wc], preferred_element_type=F32)
        ug, ug_m1, ug_m2, uv = shifted(ug_full, uv_full, cs)
        c = cb_ref[:, cs] + cw_ref[0:1, cs] * ug_m2 + cw_ref[1:2, cs] * ug_m1 + cw_ref[2:3, cs] * ug
        act = (_gelu_tanh(c) * uv).astype(BF16)
        acc = acc + jnp.dot(act, wdown_ref[j * wc:(j + 1) * wc, :], preferred_element_type=F32)
        ug_sink(ug, cs)
    return acc


def _ffn_prompt_kernel(x_ref, halo_ref, g2_ref, wup_ref, cw_ref, cb_ref, wdown_ref, gf_ref, y_ref, ug_ref):
    tm = x_ref.shape[0]
    x = x_ref[...]
    xe = jnp.concatenate([halo_ref[...], x], axis=0)
    h = _rms(xe, g2_ref[...]).astype(BF16)
    row = lax.broadcasted_iota(jnp.int32, (SUBLANES + tm, 1), 0)
    before_start = jnp.logical_and(pl.program_id(1) == 0, row < SUBLANES)

    def shifted(ug_e, uv_e, cs):
        ug_e = jnp.where(before_start, 0.0, ug_e)
        return (ug_e[SUBLANES:], pltpu.roll(ug_e, 1, 0)[SUBLANES:], pltpu.roll(ug_e, 2, 0)[SUBLANES:],
                uv_e[SUBLANES:])

    def ug_sink(ug, cs):
        ug_ref[:, cs] = ug[tm - SUBLANES:]

    x2 = _conv_ffn_cols(h, x, wup_ref, cw_ref, cb_ref, wdown_ref, shifted, ug_sink)
    y_ref[...] = _rms(x2, gf_ref[...])


def _ffn_sample_kernel(x_ref, e_ref, g2_ref, wup_ref, cw_ref, cb_ref, wdown_ref, gf_ref, y_ref, ug_ref, *, seq):
    rows = x_ref.shape[0]
    x = x_ref[...]
    h = _rms(x, g2_ref[...]).astype(BF16)
    t = lax.broadcasted_iota(jnp.int32, (rows, 1), 0) % seq

    def shifted(ug, uv, cs):
        e = e_ref[:, cs]
        ug_m1 = jnp.where(t == 0, pltpu.roll(e, rows - 1, 0), pltpu.roll(ug, 1, 0))
        ug_m2 = jnp.where(t < 2, e, pltpu.roll(ug, 2, 0))
        return ug, ug_m1, ug_m2, uv

    def ug_sink(ug, cs):
        ug_ref[:, cs] = ug

    x2 = _conv_ffn_cols(h, x, wup_ref, cw_ref, cb_ref, wdown_ref, shifted, ug_sink)
    y_ref[...] = _rms(x2, gf_ref[...])


def _ffn_prompt(x1, batch, seq, norm2_g, w_up, conv_w, conv_b, w_down, final_g, tm):
    n, d = x1.shape
    d_ff = conv_b.shape[0]
    nt = seq // tm
    full = lambda a: _resident(a.shape)
    g2, cb, gf = norm2_g.reshape(1, d), conv_b.reshape(1, d_ff), final_g.reshape(1, d)
    hb = tm // SUBLANES
    y, ug_last = pl.pallas_call(
        _ffn_prompt_kernel,
        grid=(batch, nt),
        in_specs=[
            pl.BlockSpec((tm, d), lambda b, i: (b * nt + i, 0)),
            pl.BlockSpec((SUBLANES, d), lambda b, i: (jnp.maximum((b * nt + i) * hb - 1, 0), 0)),
            full(g2), full(w_up), full(conv_w), full(cb), full(w_down), full(gf),
        ],
        out_specs=[
            pl.BlockSpec((tm, d), lambda b, i: (b * nt + i, 0)),
            pl.BlockSpec((SUBLANES, d_ff), lambda b, i: (b, 0)),
        ],
        out_shape=[
            jax.ShapeDtypeStruct((n, d), F32),
            jax.ShapeDtypeStruct((batch * SUBLANES, d_ff), F32),
        ],
        compiler_params=_cparams(("arbitrary", "arbitrary")),
        name="conv_ffn_prompt",
    )(x1, x1, g2, w_up, conv_w, cb, w_down, gf)
    conv_new = ug_last.reshape(batch, SUBLANES, d_ff)[:, SUBLANES - (CONV_W - 1):]
    return y, conv_new


def _ffn_sample(x1, batch, seq, conv0, norm2_g, w_up, conv_w, conv_b, w_down, final_g, bt):
    n, d = x1.shape
    d_ff = conv_b.shape[0]
    rows = bt * seq
    e = jnp.pad(conv0, ((0, 0), (0, seq - (CONV_W - 1)), (0, 0))).reshape(n, d_ff)
    full = lambda a: _resident(a.shape)
    g2, cb, gf = norm2_g.reshape(1, d), conv_b.reshape(1, d_ff), final_g.reshape(1, d)
    y, ug = pl.pallas_call(
        functools.partial(_ffn_sample_kernel, seq=seq),
        grid=(n // rows,),
        in_specs=[
            pl.BlockSpec((rows, d), lambda i: (i, 0)),
            pl.BlockSpec((rows, d_ff), lambda i: (i, 0)),
            full(g2), full(w_up), full(conv_w), full(cb), full(w_down), full(gf),
        ],
        out_specs=[
            pl.BlockSpec((rows, d), lambda i: (i, 0)),
            pl.BlockSpec((rows, d_ff), lambda i: (i, 0)),
        ],
        out_shape=[jax.ShapeDtypeStruct((n, d), F32), jax.ShapeDtypeStruct((n, d_ff), F32)],
        compiler_params=_cparams(("parallel",)),
        name="conv_ffn_sample",
    )(x1, e, g2, w_up, conv_w, cb, w_down, gf)
    conv_new = ug.reshape(batch, seq, d_ff)[:, seq - (CONV_W - 1):]
    return y, conv_new


def _row_tile(n, want):
    t = min(want, n)
    while n % t:
        t //= 2
    return t


def _layer(x, is_prompt, state, rel_bias, lw, final_g):
    batch, seq, d = x.shape
    n = batch * seq
    x2d = x.reshape(n, d)
    n_qkv = Q_W + 2 * KV_W
    n_rw = lw["mu_shift"].shape[0]
    tm = _row_tile(n, 512)
    qkv, rw, gates = _in_proj(x2d, lw["norm1_g"], lw["w_in"], n_qkv, n_rw, tm)
    kv = qkv.reshape(batch, seq, n_qkv)
    k_new = kv[:, :, Q_W:Q_W + KV_W]
    v_new = kv[:, :, Q_W + KV_W:]
    if is_prompt:
        att = _attention_prompt(qkv, batch, seq, rel_bias, lw["sinks"])
        wb = state["win_buf"]
        k_rows, v_rows = k_new[:, seq - wb:], v_new[:, seq - wb:]
        shift0 = wkv0 = None
    else:
        cache_k, cache_v = state["cache_k"], state["cache_v"]
        wb = cache_k.shape[1]
        att = _attention_sample(qkv, batch, seq, cache_k, cache_v, rel_bias, lw["sinks"], _row_tile(batch, 16))
        k_rows = jnp.concatenate([cache_k.reshape(batch, wb, KV_W), k_new], axis=1)[:, seq:]
        v_rows = jnp.concatenate([cache_v.reshape(batch, wb, KV_W), v_new], axis=1)[:, seq:]
        shift0, wkv0 = state["shift"], state["wkv"]
    k_rows = k_rows.reshape(batch, -1, N_KV, HEAD_DIM)
    v_rows = v_rows.reshape(batch, -1, N_KV, HEAD_DIM)
    rwo, wkv_new = _rwkv_mixer(rw, batch, seq, shift0, wkv0, lw["mu_shift"], lw["w0"], lw["w2"], lw["a0"],
                               lw["a2"], lw["g2"], lw["k_k"], lw["k_a"], lw["r_k"], lw["lnx_g"], lw["lnx_b"])
    shift_new = rw.reshape(batch, seq, n_rw)[:, seq - 1]
    x1 = _mix(x2d, att, rwo, gates, lw["w_pa"], lw["w_pb"], lw["w_o"], tm)
    if is_prompt:
        y, conv_new = _ffn_prompt(x1, batch, seq, lw["norm2_g"], lw["w_up"], lw["conv_w"], lw["conv_b"],
                                  lw["w_down"], final_g, _row_tile(seq, 512))
    else:
        y, conv_new = _ffn_sample(x1, batch, seq, state["conv"], lw["norm2_g"], lw["w_up"], lw["conv_w"],
                                  lw["conv_b"], lw["w_down"], final_g, _row_tile(batch, 64))
    return y.reshape(batch, seq, d), (k_rows, v_rows, shift_new, wkv_new, conv_new)


def kernel(x_prompt, x_sample, cache_win_k, cache_win_v, state_shift, state_wkv, state_conv, rel_bias, norm1_g,
           w_in, sinks, mu_shift, w0, w2, a0, a2, g2, k_k, k_a, r_k, lnx_g, lnx_b, w_pa, w_pb, w_o, norm2_g,
           w_up, conv_w, conv_b, w_down, final_g):
    depth = w_in.shape[0]
    assert depth == 1, "the final norm is fused into the layer's last kernel"
    l = 0
    lw = dict(norm1_g=norm1_g[l], w_in=w_in[l].astype(BF16), sinks=sinks[l], mu_shift=mu_shift[l], w0=w0[l],
              w2=w2[l], a0=a0[l], a2=a2[l], g2=g2[l], k_k=k_k[l], k_a=k_a[l], r_k=r_k[l].reshape(-1),
              lnx_g=lnx_g[l], lnx_b=lnx_b[l], w_pa=w_pa[l].astype(BF16), w_pb=w_pb[l].astype(BF16),
              w_o=w_o[l].astype(BF16), norm2_g=norm2_g[l], w_up=w_up[l].astype(BF16), conv_w=conv_w[l],
              conv_b=conv_b[l], w_down=w_down[l].astype(BF16))
    win_buf = cache_win_k.shape[2]
    y_p, st_p = _layer(x_prompt, True, dict(win_buf=win_buf), rel_bias, lw, final_g)
    y_s, st_s = _layer(x_sample, False,
                       dict(cache_k=cache_win_k[l], cache_v=cache_win_v[l], shift=state_shift[l],
                            wkv=state_wkv[l], conv=state_conv[l]), rel_bias, lw, final_g)
    stack = lambda t: t[None]
    return (y_p, y_s) + tuple(stack(t) for t in st_p) + tuple(stack(t) for t in st_s)
```

```python
import functools
import math

import numpy as np
import jax
import jax.numpy as jnp
from jax import lax
from jax.experimental import pallas as pl
from jax.experimental.pallas import tpu as pltpu

F32 = jnp.float32
BF16 = jnp.bfloat16

HEAD_DIM = 64
N_HEADS = 8
N_KV = 2
WINDOW = 128
N_BUCKETS = 32
MAX_EXACT = N_BUCKETS // 2
REL_MAX_DIST = 128
RW_N = 64
RW_HEADS = 8
RW = RW_HEADS * RW_N
NORM_EPS = 1e-6
GN_EPS = 64e-5
NEG = -1e30
CONV_W = 3

Q_W = N_HEADS * HEAD_DIM
KV_W = N_KV * HEAD_DIM
LANES = 128
SUBLANES = 8
CHUNK = 64
VMEM_LIMIT = 56 * 1024 * 1024
FFN_COL_CHUNKS = 2
ATTN_BLOCKS = 2
WKV_UNITS = 2


def _resident(shape):
    return pl.BlockSpec(shape, lambda *_: (0,) * len(shape), pipeline_mode=pl.Buffered(1))


def _cparams(sem):
    return pltpu.CompilerParams(dimension_semantics=sem, vmem_limit_bytes=VMEM_LIMIT)


def _sigmoid(x):
    return 1.0 / (1.0 + jnp.exp(-x))


def _dg(a, b, kind):
    if kind == "nn":
        dn = (((1,), (0,)), ((), ()))
    else:
        dn = (((1,), (1,)), ((), ()))
    return lax.dot_general(a, b, dn, preferred_element_type=F32)


def _split(x):
    hi = x.astype(BF16)
    lo = (x - hi.astype(F32)).astype(BF16)
    return hi, lo


def _mm(a, b, kind="nn", passes=1):
    if passes == 1:
        return _dg(a.astype(BF16), b.astype(BF16), kind)
    ah, al = _split(a)
    bh, bl = _split(b)
    return _dg(ah, bh, kind) + (_dg(ah, bl, kind) + _dg(al, bh, kind))


def _mm_exact_lhs(a_bf16, b, n_terms):
    out = None
    rem = b
    for _ in range(n_terms):
        piece = rem.astype(BF16)
        term = _dg(a_bf16, piece, "nn")
        out = term if out is None else out + term
        rem = rem - piece.astype(F32)
    return out


def _mm_exact_rhs(a, b_bf16, n_terms):
    out = None
    rem = a
    for _ in range(n_terms):
        piece = rem.astype(BF16)
        term = _dg(piece, b_bf16, "nn")
        out = term if out is None else out + term
        rem = rem - piece.astype(F32)
    return out


def _inproj_kernel(x_ref, g_ref, w_ref, qkv_ref, rw_ref, gate_ref, *, n_qkv, n_rw):
    x = x_ref[...]
    ms = jnp.mean(x * x, axis=-1, keepdims=True)
    h = (x * lax.rsqrt(ms + NORM_EPS) * g_ref[...]).astype(BF16)
    qkv_ref[...] = jnp.dot(h, w_ref[:, :n_qkv], preferred_element_type=F32)
    rw_ref[...] = jnp.dot(h, w_ref[:, n_qkv:n_qkv + n_rw], preferred_element_type=F32)
    gate = _sigmoid(jnp.dot(h, w_ref[:, n_qkv + n_rw:], preferred_element_type=F32))
    gate_ref[...] = gate.astype(gate_ref.dtype)


def _in_proj(x2d, g, w_bf16, n_qkv, n_rw, tm):
    n, d = x2d.shape
    n_gate = w_bf16.shape[1] - n_qkv - n_rw
    return pl.pallas_call(
        functools.partial(_inproj_kernel, n_qkv=n_qkv, n_rw=n_rw),
        grid=(n // tm,),
        in_specs=[
            pl.BlockSpec((tm, d), lambda i: (i, 0)),
            _resident((1, d)),
            _resident(w_bf16.shape),
        ],
        out_specs=[
            pl.BlockSpec((tm, n_qkv), lambda i: (i, 0)),
            pl.BlockSpec((tm, n_rw), lambda i: (i, 0)),
            pl.BlockSpec((tm, n_gate), lambda i: (i, 0)),
        ],
        out_shape=[
            jax.ShapeDtypeStruct((n, n_qkv), F32),
            jax.ShapeDtypeStruct((n, n_rw), F32),
            jax.ShapeDtypeStruct((n, n_gate), BF16),
        ],
        compiler_params=_cparams(("parallel",)),
        name="in_proj",
    )(x2d, g.reshape(1, d), w_bf16)


def _t5_bucket_np(dist):
    n = np.maximum(dist, 0)
    nf = np.maximum(n, 1).astype(np.float32)
    large = MAX_EXACT + (np.log(nf / MAX_EXACT) / math.log(REL_MAX_DIST / MAX_EXACT)
                         * (N_BUCKETS - MAX_EXACT)).astype(np.int32)
    return np.where(n < MAX_EXACT, n, np.minimum(large, N_BUCKETS - 1)).astype(np.int32)


def _attn_kernel(q_ref, k1_ref, k2_ref, v1_ref, v2_ref, bucket_ref, relb_ref, sink_ref, o_ref, bias_ref,
                 *, nq, nk, n_blocks, first_block_axis):
    bt = q_ref.shape[0]
    first_step = pl.program_id(0) == 0
    if first_block_axis is not None:
        first_step = jnp.logical_and(first_step, pl.program_id(1) == 0)

    @pl.when(first_step)
    def _():
        bucket = bucket_ref[...]
        for n in range(N_HEADS):
            acc = jnp.full((nq, nk), NEG, F32)
            for b in range(N_BUCKETS):
                acc = jnp.where(bucket == b, relb_ref[b, n], acc)
            c, half = divmod(n, 2)
            bias_ref[c, :, half * nk:(half + 1) * nk] = acc

    n_keys = nk + (n_blocks - 1) * nq

    def padded(a_ref, b_ref):
        parts = [a_ref[...], b_ref[...]]
        n_now = a_ref.shape[1] + b_ref.shape[1]
        if n_now < n_keys:
            parts.append(jnp.zeros((bt, n_keys - n_now, LANES), F32))
        return jnp.concatenate(parts, axis=1)

    kk = padded(k1_ref, k2_ref)
    vv = padded(v1_ref, v2_ref)
    kk_r = pltpu.roll(kk, HEAD_DIM, 2)
    vv_r = pltpu.roll(vv, HEAD_DIM, 2)
    lane = lax.broadcasted_iota(jnp.int32, (1, 1, LANES), 2)
    lo = lane < HEAD_DIM

    def halves(x, x_r, kvh):
        src_lo, src_hi = (x, x_r) if kvh == 0 else (x_r, x)
        even = jnp.where(lo, src_lo, 0.0).astype(BF16)
        odd = jnp.where(lo, 0.0, src_hi).astype(BF16)
        return even, odd

    k_eo = [halves(kk, kk_r, h) for h in range(N_KV)]
    v_eo = [halves(vv, vv_r, h) for h in range(N_KV)]

    def window(eo, j):
        return jnp.concatenate([eo[0][:, j * nq:j * nq + nk], eo[1][:, j * nq:j * nq + nk]], axis=1)

    n_cols = N_HEADS // 2
    units = [(j, c) for j in range(n_blocks) for c in range(n_cols)]
    kvh_of = lambda c: (2 * c) // (N_HEADS // N_KV)
    k_win = {(j, h): window(k_eo[h], j) for j in range(n_blocks) for h in range(N_KV)}
    v_win = {(j, h): window(v_eo[h], j) for j in range(n_blocks) for h in range(N_KV)}

    key_idx = lax.broadcasted_iota(jnp.int32, (1, 1, 2 * nk), 2)
    scale = HEAD_DIM ** -0.5
    s_all = {}
    for j, c in units:
        qc = q_ref[:, j * nq:(j + 1) * nq, c * LANES:(c + 1) * LANES].astype(BF16)
        s_all[j, c] = jnp.einsum("bqd,bkd->bqk", qc, k_win[j, kvh_of(c)], preferred_element_type=F32)
    e_all, inv_all = {}, {}
    for j, c in units:
        bias = bias_ref[c][None]
        readable = bias > 0.5 * NEG
        if first_block_axis is not None and j == 0:
            no_prev = pl.program_id(first_block_axis) == 0
            prev_key = (key_idx % nk) < (nk // 2)
            readable = jnp.logical_and(readable, jnp.logical_not(jnp.logical_and(no_prev, prev_key)))
        s = jnp.where(readable, s_all[j, c] * scale + bias, NEG)
        es, invs = [], []
        for half in range(2):
            sh = s[:, :, half * nk:(half + 1) * nk]
            sink = sink_ref[2 * c + half]
            m = jnp.maximum(jnp.max(sh, axis=-1, keepdims=True), sink)
            e = jnp.exp(sh - m)
            invs.append(1.0 / (jnp.sum(e, axis=-1, keepdims=True) + jnp.exp(sink - m)))
            es.append(e.astype(BF16))
        e_all[j, c] = jnp.concatenate(es, axis=2)
        inv_all[j, c] = jnp.where(lo, invs[0], invs[1])
    for j, c in units:
        o = jnp.einsum("bqk,bkd->bqd", e_all[j, c], v_win[j, kvh_of(c)], preferred_element_type=F32)
        o_ref[:, j * nq:(j + 1) * nq, c * LANES:(c + 1) * LANES] = (o * inv_all[j, c]).astype(o_ref.dtype)


def _attention_prompt(qkv, batch, seq, rel_bias, sinks):
    nblk = seq // WINDOW
    nb = ATTN_BLOCKS if nblk % ATTN_BLOCKS == 0 else 1
    nsteps = nblk // nb
    width = qkv.shape[1]
    q_blk = qkv.reshape(batch * nblk, WINDOW, width)
    q_step = qkv.reshape(batch * nsteps, nb * WINDOW, width)
    kcol = Q_W // LANES
    vcol = (Q_W + KV_W) // LANES
    nk = 2 * WINDOW
    qi = np.arange(WINDOW)[:, None] + WINDOW
    kj = np.arange(nk)[None, :]
    dist = qi - kj
    bucket = np.where((dist >= 0) & (dist < WINDOW), _t5_bucket_np(dist), -1).astype(np.int32)
    cur = lambda c: (lambda b, i: (b * nsteps + i, 0, c))
    prev = lambda c: (lambda b, i: (b * nblk + jnp.maximum(i * nb - 1, 0), 0, c))
    out = pl.pallas_call(
        functools.partial(_attn_kernel, nq=WINDOW, nk=nk, n_blocks=nb, first_block_axis=1),
        grid=(batch, nsteps),
        in_specs=[
            pl.BlockSpec((1, nb * WINDOW, Q_W), cur(0)),
            pl.BlockSpec((1, WINDOW, LANES), prev(kcol)),
            pl.BlockSpec((1, nb * WINDOW, LANES), cur(kcol)),
            pl.BlockSpec((1, WINDOW, LANES), prev(vcol)),
            pl.BlockSpec((1, nb * WINDOW, LANES), cur(vcol)),
            _resident(bucket.shape),
            pl.BlockSpec(memory_space=pltpu.SMEM),
            pl.BlockSpec(memory_space=pltpu.SMEM),
        ],
        out_specs=pl.BlockSpec((1, nb * WINDOW, Q_W), cur(0)),
        out_shape=jax.ShapeDtypeStruct((batch * nsteps, nb * WINDOW, Q_W), BF16),
        scratch_shapes=[pltpu.VMEM((N_HEADS // 2, WINDOW, 2 * nk), F32)],
        compiler_params=_cparams(("arbitrary", "arbitrary")),
        name="attn_prompt",
    )(q_step, q_blk, q_step, q_blk, q_step, jnp.asarray(bucket), rel_bias, sinks)
    return out.reshape(batch * seq, Q_W)


def _attention_sample(qkv, batch, seq, cache_k, cache_v, rel_bias, sinks, bt):
    wb = cache_k.shape[1]
    nk = 2 * WINDOW
    q3 = qkv.reshape(batch, seq, qkv.shape[1])
    ck = cache_k.reshape(batch, wb, KV_W)
    cv = cache_v.reshape(batch, wb, KV_W)
    kcol = Q_W // LANES
    vcol = (Q_W + KV_W) // LANES
    tq = np.arange(seq)[:, None]
    j = np.arange(nk)[None, :]
    dist = np.where(j < wb, tq + wb - j, tq - (j - wb))
    ok = (dist >= 0) & (dist < WINDOW) & (j < wb + seq)
    bucket = np.where(ok, _t5_bucket_np(dist), -1).astype(np.int32)
    out = pl.pallas_call(
        functools.partial(_attn_kernel, nq=seq, nk=nk, n_blocks=1, first_block_axis=None),
        grid=(batch // bt,),
        in_specs=[
            pl.BlockSpec((bt, seq, Q_W), lambda b: (b, 0, 0)),
            pl.BlockSpec((bt, wb, LANES), lambda b: (b, 0, 0)),
            pl.BlockSpec((bt, seq, LANES), lambda b: (b, 0, kcol)),
            pl.BlockSpec((bt, wb, LANES), lambda b: (b, 0, 0)),
            pl.BlockSpec((bt, seq, LANES), lambda b: (b, 0, vcol)),
            _resident(bucket.shape),
            pl.BlockSpec(memory_space=pltpu.SMEM),
            pl.BlockSpec(memory_space=pltpu.SMEM),
        ],
        out_specs=pl.BlockSpec((bt, seq, Q_W), lambda b: (b, 0, 0)),
        out_shape=jax.ShapeDtypeStruct((batch, seq, Q_W), BF16),
        scratch_shapes=[pltpu.VMEM((N_HEADS // 2, seq, 2 * nk), F32)],
        compiler_params=_cparams(("arbitrary",)),
        name="attn_sample",
    )(q3, ck, q3, cv, q3, jnp.asarray(bucket), rel_bias, sinks)
    return out.reshape(batch * seq, Q_W)


P_SCORE = 1
P_TINV = 1
P_INTRA = 1
P_STATE = 1


def _wkv_kernel(*refs, n_units, n_seg, has_state, lora_w):
    C = CHUNK
    seg_len = C // n_seg
    n_pairs = RW // LANES
    it = iter(refs)
    p_ref = next(it)
    p0_ref, s0_ref = (next(it), next(it)) if has_state else (None, None)
    (mu_ref, w2a2_ref, w0_ref, a0_ref, g2_ref, kk_ref, ka_ref, rk_ref, lng_ref, lnb_ref, ltri_ref) = (
        next(it) for _ in range(11))
    elast_ref = next(it) if n_seg > 1 else None
    ones_ref, out_ref, sout_ref = next(it), next(it), next(it)
    last_ref, sbd_ref = (None, None) if has_state else (next(it), next(it))

    if not has_state:
        @pl.when(pl.program_id(1) == 0)
        def _():
            last_ref[...] = jnp.zeros_like(last_ref)
            sbd_ref[...] = jnp.zeros_like(sbd_ref)

    row = lax.broadcasted_iota(jnp.int32, (C, 1), 0)
    mu = mu_ref[...]
    xs_parts = []
    for u in range(n_units):
        p = p_ref[u, 0]
        rolled = pltpu.roll(p, 1, 0)
        if has_state:
            prev = jnp.where(row % seg_len == 0, p0_ref[u, 0], rolled)
        else:
            prev = jnp.where(row == 0, last_ref[u], rolled)
            last_ref[u] = p_ref[u, 0, C - 1:C, :]
        xs_parts.append(p + mu * (prev - p))
    xs = jnp.concatenate(xs_parts, axis=0)
    rows_of = lambda u: slice(u * C, (u + 1) * C)

    r = xs[:, 0:RW]
    k = xs[:, RW:2 * RW]
    v = xs[:, 2 * RW:3 * RW]
    lwla = xs[:, 3 * RW:3 * RW + LANES]
    lg = xs[:, 3 * RW + LANES:3 * RW + 2 * LANES]
    lane = lax.broadcasted_iota(jnp.int32, (1, LANES), 1)
    lo = lane < RW_N
    lwla = jnp.where(lane < lora_w, jnp.tanh(lwla), lwla)
    wa = jnp.dot(lwla.astype(BF16), w2a2_ref[...], preferred_element_type=F32)
    logw = -math.exp(-0.5) * _sigmoid(w0_ref[...] + wa[:, :RW])
    a_sig = _sigmoid(a0_ref[...] + wa[:, RW:])
    g = jnp.dot(_sigmoid(lg).astype(BF16), g2_ref[...], preferred_element_type=F32)

    ones_bd = ones_ref[...]
    n_rows = n_units * C

    def headsum(x):
        stacked = jnp.concatenate([x[:, q * LANES:(q + 1) * LANES] for q in range(n_pairs)], axis=0)
        s = _mm_exact_rhs(stacked, ones_bd, 2)
        return jnp.concatenate([s[q * n_rows:(q + 1) * n_rows] for q in range(n_pairs)], axis=1)

    kk = k * kk_ref[...]
    kk = kk * (1.0 / jnp.maximum(jnp.sqrt(headsum(kk * kk)), 1e-12))
    k = k * (1.0 + (a_sig - 1.0) * ka_ref[...])
    a = -kk
    b = kk * a_sig

    cws, cwl = [], []
    for u in range(n_units):
        cw_u = _mm_exact_lhs(ltri_ref[...], logw[rows_of(u)], 3)
        cws.append(cw_u)
        if n_seg == 1:
            cwl.append(jnp.broadcast_to(cw_u[C - 1:C, :], (C, RW)))
        else:
            cwl.append(_mm_exact_lhs(elast_ref[...], cw_u, 3))
    cw = jnp.concatenate(cws, axis=0)
    cw_last = jnp.concatenate(cwl, axis=0)
    w_inv = jnp.exp(-cw)
    w_tail = jnp.exp(cw_last - cw)
    a_t = a * jnp.exp(cw - logw)
    r_t = r * jnp.exp(cw)
    b_t = b * w_inv
    k_t = k * w_inv
    b_h = b * w_tail
    k_h = k * w_tail
    w_c = jnp.exp(cw_last)

    def bd(y):
        return jnp.concatenate([jnp.where(lo, y, 0.0), jnp.where(lo, 0.0, y)], axis=0)

    s_idx = lane % C
    strict = s_idx < row
    incl = s_idx <= row
    if n_seg > 1:
        same_seg = (s_idx // seg_len) == (row // seg_len)
        strict = jnp.logical_and(strict, same_seg)
        incl = jnp.logical_and(incl, same_seg)
    row2 = lax.broadcasted_iota(jnp.int32, (2 * C, 1), 0)
    same_head = (row2 < RW_N) == lo
    zeros_sq = jnp.zeros((2 * C, LANES), F32)

    units = [(u, q) for u in range(n_units) for q in range(n_pairs)]
    part = lambda x, u, q: x[u * C:(u + 1) * C, q * LANES:(q + 1) * LANES]
    sc = {uq: _mm(jnp.concatenate([part(a_t, *uq), part(r_t, *uq)], axis=0),
                  jnp.concatenate([bd(part(b_t, *uq)), bd(part(k_t, *uq))], axis=0),
                  "nt", P_SCORE) for uq in units}
    pw = {uq: jnp.where(strict, sc[uq][:C, :LANES], 0.0) for uq in units}
    m_rbk = {uq: jnp.concatenate([jnp.where(incl, sc[uq][C:, :LANES], 0.0),
                                  jnp.where(incl, sc[uq][C:, LANES:], 0.0)], axis=1) for uq in units}
    bd_v = {uq: bd(part(v, *uq)) for uq in units}
    x = {uq: jnp.concatenate([part(a_t, *uq),
                              _mm(jnp.where(strict, sc[uq][:C, LANES:], 0.0), bd_v[uq], "nn", P_INTRA)], axis=1)
         for uq in units}
    n_lvl = int(math.log2(seg_len))
    for lvl in range(n_lvl):
        last = lvl + 1 == n_lvl
        for uq in units:
            parts = [bd(x[uq][:, :LANES]), bd(x[uq][:, LANES:])]
            if not last:
                parts.append(bd(pw[uq]))
            upd = _mm(pw[uq], jnp.concatenate(parts, axis=1), "nn", P_TINV)
            x[uq] = x[uq] + upd[:, :2 * LANES]
            if not last:
                pw[uq] = upd[:, 2 * LANES:]
    a_hat = {uq: x[uq][:, :LANES] for uq in units}
    v_hat = {uq: x[uq][:, LANES:] for uq in units}
    z = {uq: _mm(m_rbk[uq],
                 jnp.concatenate([jnp.concatenate([bd(a_hat[uq]), bd(v_hat[uq])], axis=1),
                                  jnp.concatenate([zeros_sq, bd_v[uq]], axis=1)], axis=0),
                 "nn", P_INTRA) for uq in units}
    r_hat = {uq: part(r_t, *uq) + z[uq][:, :LANES] for uq in units}
    ys = {}
    if n_seg == 1:
        s_old = {uq: sbd_ref[uq[0], uq[1]] for uq in units}
        t1 = {uq: _mm(jnp.concatenate([a_hat[uq], r_hat[uq]], axis=0), s_old[uq], "nt", P_STATE)
              for uq in units}
        for uq in units:
            ys[uq] = t1[uq][C:] + z[uq][:, LANES:]
            uv = jnp.concatenate([t1[uq][:C] + v_hat[uq], part(v, *uq)], axis=0)
            bkh = jnp.concatenate([part(b_h, *uq), part(k_h, *uq)], axis=0)
            ds = _mm(uv.T, bkh, "nn", P_STATE)
            sbd_ref[uq[0], uq[1]] = s_old[uq] * part(w_c, *uq)[0:1] + jnp.where(same_head, ds, 0.0)
    else:
        row_seg = (row2 % C) // seg_len
        for uq in units:
            u, q = uq
            u_parts, y_parts, s_olds = [], [], []
            for sg in range(n_seg):
                rs = slice(sg * seg_len, (sg + 1) * seg_len)
                s_sg = s0_ref[u * n_seg + sg, q]
                t1 = _mm(jnp.concatenate([a_hat[uq][rs], r_hat[uq][rs]], axis=0), s_sg, "nt", P_STATE)
                u_parts.append(t1[:seg_len] + v_hat[uq][rs])
                y_parts.append(t1[seg_len:] + z[uq][rs, LANES:])
                s_olds.append(s_sg)
            ys[uq] = jnp.concatenate(y_parts, axis=0)
            uv_t = jnp.concatenate(u_parts + [part(v, *uq)], axis=0).T
            bkh = jnp.concatenate([part(b_h, *uq), part(k_h, *uq)], axis=0)
            w_cq = part(w_c, *uq)
            for sg in range(n_seg):
                ds = _mm(uv_t, jnp.where(row_seg == sg, bkh, 0.0), "nn", P_STATE)
                sout_ref[u * n_seg + sg, q] = (s_olds[sg] * w_cq[sg * seg_len:sg * seg_len + 1]
                                               + jnp.where(same_head, ds, 0.0))

    y = jnp.concatenate([jnp.concatenate([ys[u, q] for q in range(n_pairs)], axis=1)
                         for u in range(n_units)], axis=0)
    mean = headsum(y) * (1.0 / RW_N)
    d = y - mean
    var = headsum(d * d) * (1.0 / RW_N)
    y = d * lax.rsqrt(var + GN_EPS) * lng_ref[...] + lnb_ref[...]
    y = y + headsum(r * k * rk_ref[...]) * v
    y = (y * g).astype(out_ref.dtype)
    for u in range(n_units):
        out_ref[u, 0] = y[rows_of(u)]

    if not has_state:
        @pl.when(pl.program_id(1) == pl.num_programs(1) - 1)
        def _():
            sout_ref[...] = sbd_ref[...]


def _state_to_bd(s):
    b, h, n, _ = s.shape
    s = s.reshape(b, h // 2, 2, n, n)
    z = jnp.zeros_like(s[:, :, 0])
    top = jnp.concatenate([s[:, :, 0], z], axis=-1)
    bot = jnp.concatenate([z, s[:, :, 1]], axis=-1)
    return jnp.concatenate([top, bot], axis=-2)


def _state_from_bd(sbd):
    b, hp, n2, _ = sbd.shape
    n = n2 // 2
    return jnp.stack([sbd[:, :, :n, :n], sbd[:, :, n:, n:]], axis=2).reshape(b, hp * 2, n, n)


def _rwkv_mixer(rw, batch, seq, shift0, wkv0, mu, w0, w2, a0, a2, g2, k_k, k_a, r_k, lnx_g, lnx_b):
    n_shift = rw.shape[1]
    lora_w, lora_a = w2.shape[0], a2.shape[0]
    assert lora_w + lora_a == LANES and g2.shape[0] == LANES and n_shift == 3 * RW + 2 * LANES
    has_state = shift0 is not None
    C = CHUNK
    n_pairs = RW // LANES
    if has_state:
        assert C % seq == 0 and batch % (C // seq) == 0
        n_seg, n_chunks, n_groups = C // seq, 1, batch * seq // C
    else:
        assert seq % C == 0
        n_seg, n_chunks, n_groups = 1, seq // C, batch
    nu = WKV_UNITS if n_groups % WKV_UNITS == 0 else 1
    seg_len = C // n_seg
    w2a2 = jnp.zeros((LANES, 2 * RW), F32).at[:lora_w, :RW].set(w2).at[lora_w:, RW:].set(a2).astype(BF16)
    t = np.arange(C)
    same_seg = (t[:, None] // seg_len) == (t[None, :] // seg_len)
    ltri = jnp.asarray(((t[:, None] >= t[None, :]) & same_seg).astype(np.float32), BF16)
    elast = jnp.asarray((t[None, :] == (t[:, None] // seg_len) * seg_len + seg_len - 1).astype(np.float32), BF16)
    ones_bd = jnp.asarray(np.kron(np.eye(2, dtype=np.float32), np.ones((RW_N, RW_N), np.float32)), BF16)
    row = lambda x: x.reshape(1, -1).astype(F32)
    rw4 = rw.reshape(n_groups, n_chunks, C, n_shift)
    blk = lambda w: pl.BlockSpec((nu, 1, C, w), lambda i, c: (i, c, 0, 0))
    st_blk = pl.BlockSpec((nu * n_seg, n_pairs, LANES, LANES), lambda i, c: (i, 0, 0, 0))
    args, specs = [rw4], [blk(n_shift)]
    if has_state:
        p0 = jnp.pad(shift0[:, None, :], ((0, 0), (0, seq - 1), (0, 0))).reshape(n_groups, 1, C, n_shift)
        args += [p0, _state_to_bd(wkv0)]
        specs += [blk(n_shift), st_blk]
    consts = [row(mu), w2a2, row(w0), row(a0), g2.astype(BF16), row(k_k), row(k_a), row(r_k), row(lnx_g),
              row(lnx_b), ltri] + ([elast] if n_seg > 1 else []) + [ones_bd]
    args += consts
    specs += [_resident(c.shape) for c in consts]
    scratch = [] if has_state else [pltpu.VMEM((nu, 1, n_shift), F32),
                                    pltpu.VMEM((nu, n_pairs, LANES, LANES), F32)]
    out, s_new = pl.pallas_call(
        functools.partial(_wkv_kernel, n_units=nu, n_seg=n_seg, has_state=has_state, lora_w=lora_w),
        grid=(n_groups // nu, n_chunks),
        in_specs=specs,
        out_specs=[blk(RW), st_blk],
        out_shape=[
            jax.ShapeDtypeStruct((n_groups, n_chunks, C, RW), BF16),
            jax.ShapeDtypeStruct((batch, n_pairs, LANES, LANES), F32),
        ],
        scratch_shapes=scratch,
        compiler_params=_cparams(("arbitrary", "arbitrary")),
        name="rwkv7",
    )(*args)
    return out.reshape(batch * seq, RW), _state_from_bd(s_new)


def _mix_kernel(x_ref, att_ref, rwo_ref, gate_ref, wpa_ref, wpb_ref, wo_ref, o_ref):
    d = x_ref.shape[1]
    pa = jnp.dot(att_ref[...], wpa_ref[...], preferred_element_type=F32)
    pb = jnp.dot(rwo_ref[...], wpb_ref[...], preferred_element_type=F32)
    mix = gate_ref[:, :d] * pa + gate_ref[:, d:] * pb
    o_ref[...] = x_ref[...] + jnp.dot(mix.astype(BF16), wo_ref[...], preferred_element_type=F32)


def _mix(x2d, att, rwo, gates, w_pa, w_pb, w_o, tm):
    n, d = x2d.shape
    rows = lambda w: pl.BlockSpec((tm, w), lambda i: (i, 0))
    full = lambda a: _resident(a.shape)
    return pl.pallas_call(
        _mix_kernel,
        grid=(n // tm,),
        in_specs=[rows(d), rows(att.shape[1]), rows(rwo.shape[1]), rows(gates.shape[1]),
                  full(w_pa), full(w_pb), full(w_o)],
        out_specs=rows(d),
        out_shape=jax.ShapeDtypeStruct((n, d), F32),
        compiler_params=_cparams(("parallel",)),
        name="branch_mix",
    )(x2d, att, rwo, gates, w_pa, w_pb, w_o)


def _rms(x, g):
    ms = jnp.mean(x * x, axis=-1, keepdims=True)
    return x * lax.rsqrt(ms + NORM_EPS) * g


def _gelu_tanh(c):
    return c * (0.5 * (1.0 + jnp.tanh(math.sqrt(2.0 / math.pi) * (c + 0.044715 * (c * c * c)))))


def _conv_ffn_cols(h, x, wup_ref, cw_ref, cb_ref, wdown_ref, shifted, ug_sink):
    d_ff = cb_ref.shape[1]
    wc = d_ff // FFN_COL_CHUNKS
    acc = x
    for j in range(FFN_COL_CHUNKS):
        cs = slice(j * wc, (j + 1) * wc)
        ug_full = jnp.dot(h, wup_ref[:, j * wc:(j + 1) * wc], preferred_element_type=F32)
        uv_full = jnp.dot(h, wup_ref[:, d_ff + j * wc:d_ff + (j + 1) * wc], preferred_element_type=F32)
        ug, ug_m1, ug_m2, uv = shifted(ug_full, uv_full, cs)
        c = cb_ref[:, cs] + cw_ref[0:1, cs] * ug_m2 + cw_ref[1:2, cs] * ug_m1 + cw_ref[2:3, cs] * ug
        act = (_gelu_tanh(c) * uv).astype(BF16)
        acc = acc + jnp.dot(act, wdown_ref[j * wc:(j + 1) * wc, :], preferred_element_type=F32)
        ug_sink(ug, cs)
    return acc


def _ffn_prompt_kernel(x_ref, halo_ref, g2_ref, wup_ref, cw_ref, cb_ref, wdown_ref, gf_ref, y_ref, ug_ref):
    tm = x_ref.shape[0]
    x = x_ref[...]
    xe = jnp.concatenate([halo_ref[...], x], axis=0)
    h = _rms(xe, g2_ref[...]).astype(BF16)
    row = lax.broadcasted_iota(jnp.int32, (SUBLANES + tm, 1), 0)
    before_start = jnp.logical_and(pl.program_id(1) == 0, row < SUBLANES)

    def shifted(ug_e, uv_e, cs):
        ug_e = jnp.where(before_start, 0.0, ug_e)
        return (ug_e[SUBLANES:], pltpu.roll(ug_e, 1, 0)[SUBLANES:], pltpu.roll(ug_e, 2, 0)[SUBLANES:],
                uv_e[SUBLANES:])

    def ug_sink(ug, cs):
        ug_ref[:, cs] = ug[tm - SUBLANES:]

    x2 = _conv_ffn_cols(h, x, wup_ref, cw_ref, cb_ref, wdown_ref, shifted, ug_sink)
    y_ref[...] = _rms(x2, gf_ref[...])


def _ffn_sample_kernel(x_ref, e_ref, g2_ref, wup_ref, cw_ref, cb_ref, wdown_ref, gf_ref, y_ref, ug_ref, *, seq):
    rows = x_ref.shape[0]
    x = x_ref[...]
    h = _rms(x, g2_ref[...]).astype(BF16)
    t = lax.broadcasted_iota(jnp.int32, (rows, 1), 0) % seq

    def shifted(ug, uv, cs):
        e = e_ref[:, cs]
        ug_m1 = jnp.where(t == 0, pltpu.roll(e, rows - 1, 0), pltpu.roll(ug, 1, 0))
        ug_m2 = jnp.where(t < 2, e, pltpu.roll(ug, 2, 0))
        return ug, ug_m1, ug_m2, uv

    def ug_sink(ug, cs):
        ug_ref[:, cs] = ug

    x2 = _conv_ffn_cols(h, x, wup_ref, cw_ref, cb_ref, wdown_ref, shifted, ug_sink)
    y_ref[...] = _rms(x2, gf_ref[...])


def _ffn_prompt(x1, batch, seq, norm2_g, w_up, conv_w, conv_b, w_down, final_g, tm):
    n, d = x1.shape
    d_ff = conv_b.shape[0]
    nt = seq // tm
    full = lambda a: _resident(a.shape)
    g2, cb, gf = norm2_g.reshape(1, d), conv_b.reshape(1, d_ff), final_g.reshape(1, d)
    hb = tm // SUBLANES
    y, ug_last = pl.pallas_call(
        _ffn_prompt_kernel,
        grid=(batch, nt),
        in_specs=[
            pl.BlockSpec((tm, d), lambda b, i: (b * nt + i, 0)),
            pl.BlockSpec((SUBLANES, d), lambda b, i: (jnp.maximum((b * nt + i) * hb - 1, 0), 0)),
            full(g2), full(w_up), full(conv_w), full(cb), full(w_down), full(gf),
        ],
        out_specs=[
            pl.BlockSpec((tm, d), lambda b, i: (b * nt + i, 0)),
            pl.BlockSpec((SUBLANES, d_ff), lambda b, i: (b, 0)),
        ],
        out_shape=[
            jax.ShapeDtypeStruct((n, d), F32),
            jax.ShapeDtypeStruct((batch * SUBLANES, d_ff), F32),
        ],
        compiler_params=_cparams(("arbitrary", "arbitrary")),
        name="conv_ffn_prompt",
    )(x1, x1, g2, w_up, conv_w, cb, w_down, gf)
    conv_new = ug_last.reshape(batch, SUBLANES, d_ff)[:, SUBLANES - (CONV_W - 1):]
    return y, conv_new


def _ffn_sample(x1, batch, seq, conv0, norm2_g, w_up, conv_w, conv_b, w_down, final_g, bt):
    n, d = x1.shape
    d_ff = conv_b.shape[0]
    rows = bt * seq
    e = jnp.pad(conv0, ((0, 0), (0, seq - (CONV_W - 1)), (0, 0))).reshape(n, d_ff)
    full = lambda a: _resident(a.shape)
    g2, cb, gf = norm2_g.reshape(1, d), conv_b.reshape(1, d_ff), final_g.reshape(1, d)
    y, ug = pl.pallas_call(
        functools.partial(_ffn_sample_kernel, seq=seq),
        grid=(n // rows,),
        in_specs=[
            pl.BlockSpec((rows, d), lambda i: (i, 0)),
            pl.BlockSpec((rows, d_ff), lambda i: (i, 0)),
            full(g2), full(w_up), full(conv_w), full(cb), full(w_down), full(gf),
        ],
        out_specs=[
            pl.BlockSpec((rows, d), lambda i: (i, 0)),
            pl.BlockSpec((rows, d_ff), lambda i: (i, 0)),
        ],
        out_shape=[jax.ShapeDtypeStruct((n, d), F32), jax.ShapeDtypeStruct((n, d_ff), F32)],
        compiler_params=_cparams(("parallel",)),
        name="conv_ffn_sample",
    )(x1, e, g2, w_up, conv_w, cb, w_down, gf)
    conv_new = ug.reshape(batch, seq, d_ff)[:, seq - (CONV_W - 1):]
    return y, conv_new


def _row_tile(n, want):
    t = min(want, n)
    while n % t:
        t //= 2
    return t


def _layer(x, is_prompt, state, rel_bias, lw, final_g):
    batch, seq, d = x.shape
    n = batch * seq
    x2d = x.reshape(n, d)
    n_qkv = Q_W + 2 * KV_W
    n_rw = lw["mu_shift"].shape[0]
    tm = _row_tile(n, 512)
    qkv, rw, gates = _in_proj(x2d, lw["norm1_g"], lw["w_in"], n_qkv, n_rw, tm)
    kv = qkv.reshape(batch, seq, n_qkv)
    k_new = kv[:, :, Q_W:Q_W + KV_W]
    v_new = kv[:, :, Q_W + KV_W:]
    if is_prompt:
        att = _attention_prompt(qkv, batch, seq, rel_bias, lw["sinks"])
        wb = state["win_buf"]
        k_rows, v_rows = k_new[:, seq - wb:], v_new[:, seq - wb:]
        shift0 = wkv0 = None
    else:
        cache_k, cache_v = state["cache_k"], state["cache_v"]
        wb = cache_k.shape[1]
        att = _attention_sample(qkv, batch, seq, cache_k, cache_v, rel_bias, lw["sinks"], _row_tile(batch, 16))
        k_rows = jnp.concatenate([cache_k.reshape(batch, wb, KV_W), k_new], axis=1)[:, seq:]
        v_rows = jnp.concatenate([cache_v.reshape(batch, wb, KV_W), v_new], axis=1)[:, seq:]
        shift0, wkv0 = state["shift"], state["wkv"]
    k_rows = k_rows.reshape(batch, -1, N_KV, HEAD_DIM)
    v_rows = v_rows.reshape(batch, -1, N_KV, HEAD_DIM)
    rwo, wkv_new = _rwkv_mixer(rw, batch, seq, shift0, wkv0, lw["mu_shift"], lw["w0"], lw["w2"], lw["a0"],
                               lw["a2"], lw["g2"], lw["k_k"], lw["k_a"], lw["r_k"], lw["lnx_g"], lw["lnx_b"])
    shift_new = rw.reshape(batch, seq, n_rw)[:, seq - 1]
    x1 = _mix(x2d, att, rwo, gates, lw["w_pa"], lw["w_pb"], lw["w_o"], tm)
    if is_prompt:
        y, conv_new = _ffn_prompt(x1, batch, seq, lw["norm2_g"], lw["w_up"], lw["conv_w"], lw["conv_b"],
                                  lw["w_down"], final_g, _row_tile(seq, 512))
    else:
        y, conv_new = _ffn_sample(x1, batch, seq, state["conv"], lw["norm2_g"], lw["w_up"], lw["conv_w"],
                                  lw["conv_b"], lw["w_down"], final_g, _row_tile(batch, 64))
    return y.reshape(batch, seq, d), (k_rows, v_rows, shift_new, wkv_new, conv_new)


def kernel(x_prompt, x_sample, cache_win_k, cache_win_v, state_shift, state_wkv, state_conv, rel_bias, norm1_g,
           w_in, sinks, mu_shift, w0, w2, a0, a2, g2, k_k, k_a, r_k, lnx_g, lnx_b, w_pa, w_pb, w_o, norm2_g,
           w_up, conv_w, conv_b, w_down, final_g):
    depth = w_in.shape[0]
    assert depth == 1, "the final norm is fused into the layer's last kernel"
    l = 0
    lw = dict(norm1_g=norm1_g[l], w_in=w_in[l].astype(BF16), sinks=sinks[l], mu_shift=mu_shift[l], w0=w0[l],
              w2=w2[l], a0=a0[l], a2=a2[l], g2=g2[l], k_k=k_k[l], k_a=k_a[l], r_k=r_k[l].reshape(-1),
              lnx_g=lnx_g[l], lnx_b=lnx_b[l], w_pa=w_pa[l].astype(BF16), w_pb=w_pb[l].astype(BF16),
              w_o=w_o[l].astype(BF16), norm2_g=norm2_g[l], w_up=w_up[l].astype(BF16), conv_w=conv_w[l],
              conv_b=conv_b[l], w_down=w_down[l].astype(BF16))
    win_buf = cache_win_k.shape[2]
    y_p, st_p = _layer(x_prompt, True, dict(win_buf=win_buf), rel_bias, lw, final_g)
    y_s, st_s = _layer(x_sample, False,
                       dict(cache_k=cache_win_k[l], cache_v=cache_win_v[l], shift=state_shift[l],
                            wkv=state_wkv[l], conv=state_conv[l]), rel_bias, lw, final_g)
    stack = lambda t: t[None]
    return (y_p, y_s) + tuple(stack(t) for t in st_p) + tuple(stack(t) for t in st_s)
```

```python
import functools
import math

import numpy as np
import jax
import jax.numpy as jnp
from jax import lax
from jax.experimental import pallas as pl
from jax.experimental.pallas import tpu as pltpu

F32 = jnp.float32
BF16 = jnp.bfloat16

HEAD_DIM = 64
N_HEADS = 8
N_KV = 2
WINDOW = 128
N_BUCKETS = 32
MAX_EXACT = N_BUCKETS // 2
REL_MAX_DIST = 128
RW_N = 64
RW_HEADS = 8
RW = RW_HEADS * RW_N
NORM_EPS = 1e-6
GN_EPS = 64e-5
NEG = -1e30
CONV_W = 3

Q_W = N_HEADS * HEAD_DIM
KV_W = N_KV * HEAD_DIM
LANES = 128
SUBLANES = 8
MXU_WIDTH = 256
CHUNK = 64
VMEM_LIMIT = 56 * 1024 * 1024
FFN_COL_CHUNKS = 2
ATTN_BLOCKS = 2
WKV_UNITS = 4


def _resident(shape):
    return pl.BlockSpec(shape, lambda *_: (0,) * len(shape), pipeline_mode=pl.Buffered(1))


def _cparams(sem):
    return pltpu.CompilerParams(dimension_semantics=sem, vmem_limit_bytes=VMEM_LIMIT)


def _sigmoid(x):
    return 1.0 / (1.0 + jnp.exp(-x))


def _dg(a, b, kind):
    if kind == "nn":
        dn = (((1,), (0,)), ((), ()))
    else:
        dn = (((1,), (1,)), ((), ()))
    return lax.dot_general(a, b, dn, preferred_element_type=F32)


def _split(x):
    hi = x.astype(BF16)
    lo = (x - hi.astype(F32)).astype(BF16)
    return hi, lo


def _mm(a, b, kind="nn", passes=1):
    if passes == 1:
        return _dg(a.astype(BF16), b.astype(BF16), kind)
    ah, al = _split(a)
    bh, bl = _split(b)
    return _dg(ah, bh, kind) + (_dg(ah, bl, kind) + _dg(al, bh, kind))


def _mm_exact_lhs(a_bf16, b, n_terms):
    out = None
    rem = b
    for _ in range(n_terms):
        piece = rem.astype(BF16)
        term = _dg(a_bf16, piece, "nn")
        out = term if out is None else out + term
        rem = rem - piece.astype(F32)
    return out


def _mm_exact_rhs(a, b_bf16, n_terms):
    out = None
    rem = a
    for _ in range(n_terms):
        piece = rem.astype(BF16)
        term = _dg(piece, b_bf16, "nn")
        out = term if out is None else out + term
        rem = rem - piece.astype(F32)
    return out


def _inproj_kernel(x_ref, g_ref, w_ref, qkv_ref, rw_ref, gate_ref, *, n_qkv, n_rw):
    x = x_ref[...]
    ms = jnp.mean(x * x, axis=-1, keepdims=True)
    h = (x * lax.rsqrt(ms + NORM_EPS) * g_ref[...]).astype(BF16)
    qkv_ref[...] = jnp.dot(h, w_ref[:, :n_qkv], preferred_element_type=F32)
    rw_ref[...] = jnp.dot(h, w_ref[:, n_qkv:n_qkv + n_rw], preferred_element_type=F32)
    gate = _sigmoid(jnp.dot(h, w_ref[:, n_qkv + n_rw:], preferred_element_type=F32))
    gate_ref[...] = gate.astype(gate_ref.dtype)


def _in_proj(x2d, g, w_bf16, n_qkv, n_rw, tm):
    n, d = x2d.shape
    n_gate = w_bf16.shape[1] - n_qkv - n_rw
    return pl.pallas_call(
        functools.partial(_inproj_kernel, n_qkv=n_qkv, n_rw=n_rw),
        grid=(n // tm,),
        in_specs=[
            pl.BlockSpec((tm, d), lambda i: (i, 0)),
            _resident((1, d)),
            _resident(w_bf16.shape),
        ],
        out_specs=[
            pl.BlockSpec((tm, n_qkv), lambda i: (i, 0)),
            pl.BlockSpec((tm, n_rw), lambda i: (i, 0)),
            pl.BlockSpec((tm, n_gate), lambda i: (i, 0)),
        ],
        out_shape=[
            jax.ShapeDtypeStruct((n, n_qkv), F32),
            jax.ShapeDtypeStruct((n, n_rw), F32),
            jax.ShapeDtypeStruct((n, n_gate), BF16),
        ],
        compiler_params=_cparams(("parallel",)),
        name="in_proj",
    )(x2d, g.reshape(1, d), w_bf16)


def _t5_bucket_np(dist):
    n = np.maximum(dist, 0)
    nf = np.maximum(n, 1).astype(np.float32)
    large = MAX_EXACT + (np.log(nf / MAX_EXACT) / math.log(REL_MAX_DIST / MAX_EXACT)
                         * (N_BUCKETS - MAX_EXACT)).astype(np.int32)
    return np.where(n < MAX_EXACT, n, np.minimum(large, N_BUCKETS - 1)).astype(np.int32)


def _attn_kernel(q_ref, k1_ref, k2_ref, v1_ref, v2_ref, bucket_ref, relb_ref, sink_ref, o_ref, *rest,
                 nq, nk, n_blocks, first_block_axis, emit_window):
    bt = q_ref.shape[0]
    bias_ref = rest[-1]
    if emit_window:
        n_old, n_new = k1_ref.shape[1], k2_ref.shape[1]
        for w_ref, old_ref, new_ref in ((rest[0], k1_ref, k2_ref), (rest[1], v1_ref, v2_ref)):
            w_ref[:, :n_old - n_new] = old_ref[:, n_new:]
            w_ref[:, n_old - n_new:] = new_ref[...]
    first_step = pl.program_id(0) == 0
    if first_block_axis is not None:
        first_step = jnp.logical_and(first_step, pl.program_id(1) == 0)

    @pl.when(first_step)
    def _():
        bucket = bucket_ref[...]
        for n in range(N_HEADS):
            acc = jnp.full((nq, nk), NEG, F32)
            for b in range(N_BUCKETS):
                acc = jnp.where(bucket == b, relb_ref[b, n], acc)
            c, half = divmod(n, 2)
            bias_ref[c, :, half * nk:(half + 1) * nk] = acc

    n_keys = nk + (n_blocks - 1) * nq

    def padded(a_ref, b_ref):
        parts = [a_ref[...], b_ref[...]]
        n_now = a_ref.shape[1] + b_ref.shape[1]
        if n_now < n_keys:
            parts.append(jnp.zeros((bt, n_keys - n_now, LANES), F32))
        return jnp.concatenate(parts, axis=1)

    kk = padded(k1_ref, k2_ref)
    vv = padded(v1_ref, v2_ref)
    kk_r = pltpu.roll(kk, HEAD_DIM, 2)
    vv_r = pltpu.roll(vv, HEAD_DIM, 2)
    lane = lax.broadcasted_iota(jnp.int32, (1, 1, LANES), 2)
    lo = lane < HEAD_DIM

    def halves(x, x_r, kvh):
        src_lo, src_hi = (x, x_r) if kvh == 0 else (x_r, x)
        even = jnp.where(lo, src_lo, 0.0).astype(BF16)
        odd = jnp.where(lo, 0.0, src_hi).astype(BF16)
        return even, odd

    k_eo = [halves(kk, kk_r, h) for h in range(N_KV)]
    v_eo = [halves(vv, vv_r, h) for h in range(N_KV)]

    def window(eo, j):
        return jnp.concatenate([eo[0][:, j * nq:j * nq + nk], eo[1][:, j * nq:j * nq + nk]], axis=1)

    n_cols = N_HEADS // 2
    units = [(j, c) for j in range(n_blocks) for c in range(n_cols)]
    kvh_of = lambda c: (2 * c) // (N_HEADS // N_KV)
    k_win = {(j, h): window(k_eo[h], j) for j in range(n_blocks) for h in range(N_KV)}
    v_win = {(j, h): window(v_eo[h], j) for j in range(n_blocks) for h in range(N_KV)}

    key_idx = lax.broadcasted_iota(jnp.int32, (1, 1, 2 * nk), 2)
    scale = HEAD_DIM ** -0.5
    s_all = {}
    for j, c in units:
        qc = q_ref[:, j * nq:(j + 1) * nq, c * LANES:(c + 1) * LANES].astype(BF16)
        s_all[j, c] = jnp.einsum("bqd,bkd->bqk", qc, k_win[j, kvh_of(c)], preferred_element_type=F32)
    e_all, inv_all = {}, {}
    for j, c in units:
        bias = bias_ref[c][None]
        readable = bias > 0.5 * NEG
        if first_block_axis is not None and j == 0:
            no_prev = pl.program_id(first_block_axis) == 0
            prev_key = (key_idx % nk) < (nk // 2)
            readable = jnp.logical_and(readable, jnp.logical_not(jnp.logical_and(no_prev, prev_key)))
        s = jnp.where(readable, s_all[j, c] * scale + bias, NEG)
        es, invs = [], []
        for half in range(2):
            sh = s[:, :, half * nk:(half + 1) * nk]
            sink = sink_ref[2 * c + half]
            m = jnp.maximum(jnp.max(sh, axis=-1, keepdims=True), sink)
            e = jnp.exp(sh - m)
            invs.append(1.0 / (jnp.sum(e, axis=-1, keepdims=True) + jnp.exp(sink - m)))
            es.append(e.astype(BF16))
        e_all[j, c] = jnp.concatenate(es, axis=2)
        inv_all[j, c] = jnp.where(lo, invs[0], invs[1])
    for j, c in units:
        o = jnp.einsum("bqk,bkd->bqd", e_all[j, c], v_win[j, kvh_of(c)], preferred_element_type=F32)
        o_ref[:, j * nq:(j + 1) * nq, c * LANES:(c + 1) * LANES] = (o * inv_all[j, c]).astype(o_ref.dtype)


def _attention_prompt(qkv, batch, seq, rel_bias, sinks):
    nblk = seq // WINDOW
    nb = ATTN_BLOCKS if nblk % ATTN_BLOCKS == 0 else 1
    nsteps = nblk // nb
    width = qkv.shape[1]
    q_blk = qkv.reshape(batch * nblk, WINDOW, width)
    q_step = qkv.reshape(batch * nsteps, nb * WINDOW, width)
    kcol = Q_W // LANES
    vcol = (Q_W + KV_W) // LANES
    nk = 2 * WINDOW
    qi = np.arange(WINDOW)[:, None] + WINDOW
    kj = np.arange(nk)[None, :]
    dist = qi - kj
    bucket = np.where((dist >= 0) & (dist < WINDOW), _t5_bucket_np(dist), -1).astype(np.int32)
    cur = lambda c: (lambda b, i: (b * nsteps + i, 0, c))
    prev = lambda c: (lambda b, i: (b * nblk + jnp.maximum(i * nb - 1, 0), 0, c))
    out = pl.pallas_call(
        functools.partial(_attn_kernel, nq=WINDOW, nk=nk, n_blocks=nb, first_block_axis=1, emit_window=False),
        grid=(batch, nsteps),
        in_specs=[
            pl.BlockSpec((1, nb * WINDOW, Q_W), cur(0)),
            pl.BlockSpec((1, WINDOW, LANES), prev(kcol)),
            pl.BlockSpec((1, nb * WINDOW, LANES), cur(kcol)),
            pl.BlockSpec((1, WINDOW, LANES), prev(vcol)),
            pl.BlockSpec((1, nb * WINDOW, LANES), cur(vcol)),
            _resident(bucket.shape),
            pl.BlockSpec(memory_space=pltpu.SMEM),
            pl.BlockSpec(memory_space=pltpu.SMEM),
        ],
        out_specs=pl.BlockSpec((1, nb * WINDOW, Q_W), cur(0)),
        out_shape=jax.ShapeDtypeStruct((batch * nsteps, nb * WINDOW, Q_W), BF16),
        scratch_shapes=[pltpu.VMEM((N_HEADS // 2, WINDOW, 2 * nk), F32)],
        compiler_params=_cparams(("arbitrary", "arbitrary")),
        name="attn_prompt",
    )(q_step, q_blk, q_step, q_blk, q_step, jnp.asarray(bucket), rel_bias, sinks)
    return out.reshape(batch * seq, Q_W)


def _attention_sample(qkv, batch, seq, cache_k, cache_v, rel_bias, sinks, bt):
    wb = cache_k.shape[1]
    nk = 2 * WINDOW
    q3 = qkv.reshape(batch, seq, qkv.shape[1])
    ck = cache_k.reshape(batch, wb, KV_W)
    cv = cache_v.reshape(batch, wb, KV_W)
    kcol = Q_W // LANES
    vcol = (Q_W + KV_W) // LANES
    tq = np.arange(seq)[:, None]
    j = np.arange(nk)[None, :]
    dist = np.where(j < wb, tq + wb - j, tq - (j - wb))
    ok = (dist >= 0) & (dist < WINDOW) & (j < wb + seq)
    bucket = np.where(ok, _t5_bucket_np(dist), -1).astype(np.int32)
    win_spec = pl.BlockSpec((bt, wb, LANES), lambda b: (b, 0, 0))
    win_shape = jax.ShapeDtypeStruct((batch, wb, KV_W), F32)
    out, win_k, win_v = pl.pallas_call(
        functools.partial(_attn_kernel, nq=seq, nk=nk, n_blocks=1, first_block_axis=None, emit_window=True),
        grid=(batch // bt,),
        in_specs=[
            pl.BlockSpec((bt, seq, Q_W), lambda b: (b, 0, 0)),
            pl.BlockSpec((bt, wb, LANES), lambda b: (b, 0, 0)),
            pl.BlockSpec((bt, seq, LANES), lambda b: (b, 0, kcol)),
            pl.BlockSpec((bt, wb, LANES), lambda b: (b, 0, 0)),
            pl.BlockSpec((bt, seq, LANES), lambda b: (b, 0, vcol)),
            _resident(bucket.shape),
            pl.BlockSpec(memory_space=pltpu.SMEM),
            pl.BlockSpec(memory_space=pltpu.SMEM),
        ],
        out_specs=[pl.BlockSpec((bt, seq, Q_W), lambda b: (b, 0, 0)), win_spec, win_spec],
        out_shape=[jax.ShapeDtypeStruct((batch, seq, Q_W), BF16), win_shape, win_shape],
        scratch_shapes=[pltpu.VMEM((N_HEADS // 2, seq, 2 * nk), F32)],
        compiler_params=_cparams(("arbitrary",)),
        name="attn_sample",
    )(q3, ck, q3, cv, q3, jnp.asarray(bucket), rel_bias, sinks)
    return out.reshape(batch * seq, Q_W), win_k, win_v


P_SCORE = 1
P_TINV = 1
P_INTRA = 1
P_STATE = 1


def _wkv_kernel(*refs, n_units, n_seg, has_state, lora_w):
    C = CHUNK
    seg_len = C // n_seg
    n_pairs = RW // LANES
    it = iter(refs)
    p_ref = next(it)
    p0_ref, s0_ref = (next(it), next(it)) if has_state else (None, None)
    (mu_ref, w2a2_ref, w0_ref, a0_ref, g2_ref, kk_ref, ka_ref, rk_ref, lng_ref, lnb_ref, ltri_ref) = (
        next(it) for _ in range(11))
    elast_ref = next(it) if n_seg > 1 else None
    ones_ref, out_ref, sout_ref = next(it), next(it), next(it)
    last_ref, sbd_ref = (None, None) if has_state else (next(it), next(it))

    if not has_state:
        @pl.when(pl.program_id(1) == 0)
        def _():
            last_ref[...] = jnp.zeros_like(last_ref)
            sbd_ref[...] = jnp.zeros_like(sbd_ref)

    row = lax.broadcasted_iota(jnp.int32, (C, 1), 0)
    mu = mu_ref[...]
    xs_parts = []
    for u in range(n_units):
        p = p_ref[u, 0]
        rolled = pltpu.roll(p, 1, 0)
        if has_state:
            prev = jnp.where(row % seg_len == 0, p0_ref[u, 0], rolled)
        else:
            prev = jnp.where(row == 0, last_ref[u], rolled)
            last_ref[u] = p_ref[u, 0, C - 1:C, :]
        xs_parts.append(p + mu * (prev - p))
    xs = jnp.concatenate(xs_parts, axis=0)
    rows_of = lambda u: slice(u * C, (u + 1) * C)

    r = xs[:, 0:RW]
    k = xs[:, RW:2 * RW]
    v = xs[:, 2 * RW:3 * RW]
    lwla = xs[:, 3 * RW:3 * RW + LANES]
    lg = xs[:, 3 * RW + LANES:3 * RW + 2 * LANES]
    lane = lax.broadcasted_iota(jnp.int32, (1, LANES), 1)
    lo = lane < RW_N
    lwla = jnp.where(lane < lora_w, jnp.tanh(lwla), lwla)
    wa = jnp.dot(lwla.astype(BF16), w2a2_ref[...], preferred_element_type=F32)
    logw = -math.exp(-0.5) * _sigmoid(w0_ref[...] + wa[:, :RW])
    a_sig = _sigmoid(a0_ref[...] + wa[:, RW:])
    g = jnp.dot(_sigmoid(lg).astype(BF16), g2_ref[...], preferred_element_type=F32)

    ones_bd = ones_ref[...]
    n_rows = n_units * C

    def headsum(x):
        stacked = jnp.concatenate([x[:, q * LANES:(q + 1) * LANES] for q in range(n_pairs)], axis=0)
        s = _mm_exact_rhs(stacked, ones_bd, 1)
        return jnp.concatenate([s[q * n_rows:(q + 1) * n_rows] for q in range(n_pairs)], axis=1)

    kk = k * kk_ref[...]
    kk = kk * (1.0 / jnp.maximum(jnp.sqrt(headsum(kk * kk)), 1e-12))
    k = k * (1.0 + (a_sig - 1.0) * ka_ref[...])
    a = -kk
    b = kk * a_sig

    cws, cwl = [], []
    for u in range(n_units):
        cw_u = _mm_exact_lhs(ltri_ref[...], logw[rows_of(u)], 3)
        cws.append(cw_u)
        if n_seg == 1:
            cwl.append(jnp.broadcast_to(cw_u[C - 1:C, :], (C, RW)))
        else:
            cwl.append(_mm_exact_lhs(elast_ref[...], cw_u, 3))
    cw = jnp.concatenate(cws, axis=0)
    cw_last = jnp.concatenate(cwl, axis=0)
    w_inv = jnp.exp(-cw)
    w_tail = jnp.exp(cw_last - cw)
    a_t = a * jnp.exp(cw - logw)
    r_t = r * jnp.exp(cw)
    b_t = b * w_inv
    k_t = k * w_inv
    b_h = b * w_tail
    k_h = k * w_tail
    w_c = jnp.exp(cw_last)

    def bd(y):
        return jnp.concatenate([jnp.where(lo, y, 0.0), jnp.where(lo, 0.0, y)], axis=0)

    zeros_head = jnp.zeros((RW_N, RW_N), F32)

    def pack_pair(s_even, s_odd):
        return jnp.concatenate([jnp.concatenate([s_even, zeros_head], axis=1),
                                jnp.concatenate([zeros_head, s_odd], axis=1)], axis=0)

    def store_pair(ref, i, q, s_pair):
        ref[i, 2 * q] = s_pair[:RW_N, :RW_N]
        ref[i, 2 * q + 1] = s_pair[RW_N:, RW_N:]

    s_idx = lane % C
    strict = s_idx < row
    incl = s_idx <= row
    if n_seg > 1:
        same_seg = (s_idx // seg_len) == (row // seg_len)
        strict = jnp.logical_and(strict, same_seg)
        incl = jnp.logical_and(incl, same_seg)
    row2 = lax.broadcasted_iota(jnp.int32, (2 * C, 1), 0)
    same_head = (row2 < RW_N) == lo
    zeros_sq = jnp.zeros((2 * C, LANES), F32)

    units = [(u, q) for u in range(n_units) for q in range(n_pairs)]
    part = lambda x, u, q: x[u * C:(u + 1) * C, q * LANES:(q + 1) * LANES]
    sc = {uq: _mm(jnp.concatenate([part(a_t, *uq), part(r_t, *uq)], axis=0),
                  jnp.concatenate([bd(part(b_t, *uq)), bd(part(k_t, *uq))], axis=0),
                  "nt", P_SCORE) for uq in units}
    pw = {uq: jnp.where(strict, sc[uq][:C, :LANES], 0.0) for uq in units}
    m_rbk = {uq: jnp.concatenate([jnp.where(incl, sc[uq][C:, :LANES], 0.0),
                                  jnp.where(incl, sc[uq][C:, LANES:], 0.0)], axis=1) for uq in units}
    bd_v = {uq: bd(part(v, *uq)) for uq in units}
    lak_v = {uq: _mm(jnp.where(strict, sc[uq][:C, LANES:], 0.0), bd_v[uq], "nn", P_INTRA) for uq in units}
    tinv = {uq: pw[uq] + jnp.where(s_idx == row, 1.0, 0.0) for uq in units}
    n_lvl = int(math.log2(seg_len))
    for lvl in range(1, n_lvl):
        last = lvl + 1 == n_lvl
        for uq in units:
            if lvl == 1:
                pw[uq] = _mm(pw[uq], bd(pw[uq]), "nn", P_TINV)
            rhs = bd(tinv[uq]) if last else jnp.concatenate([bd(tinv[uq]), bd(pw[uq])], axis=1)
            upd = _mm(pw[uq], rhs, "nn", P_TINV)
            tinv[uq] = tinv[uq] + upd[:, :LANES]
            if not last:
                pw[uq] = upd[:, LANES:]
    x = {uq: _mm(tinv[uq], jnp.concatenate([bd(part(a_t, *uq)), bd(lak_v[uq])], axis=1), "nn", P_TINV)
         for uq in units}
    a_hat = {uq: x[uq][:, :LANES] for uq in units}
    v_hat = {uq: x[uq][:, LANES:] for uq in units}
    z = {uq: _mm(m_rbk[uq],
                 jnp.concatenate([jnp.concatenate([bd(a_hat[uq]), bd(v_hat[uq])], axis=1),
                                  jnp.concatenate([zeros_sq, bd_v[uq]], axis=1)], axis=0),
                 "nn", P_INTRA) for uq in units}
    r_hat = {uq: part(r_t, *uq) + z[uq][:, :LANES] for uq in units}
    ys = {}
    if n_seg == 1:
        s_old = {uq: sbd_ref[uq[0], uq[1]] for uq in units}
        t1 = {uq: _mm(jnp.concatenate([a_hat[uq], r_hat[uq]], axis=0), s_old[uq], "nt", P_STATE)
              for uq in units}
        for uq in units:
            ys[uq] = t1[uq][C:] + z[uq][:, LANES:]
            uv = jnp.concatenate([t1[uq][:C] + v_hat[uq], part(v, *uq)], axis=0)
            bkh = jnp.concatenate([part(b_h, *uq), part(k_h, *uq)], axis=0)
            ds = _mm(uv.T, bkh, "nn", P_STATE)
            sbd_ref[uq[0], uq[1]] = s_old[uq] * part(w_c, *uq)[0:1] + jnp.where(same_head, ds, 0.0)
    else:
        row_seg = (row2 % C) // seg_len
        for uq in units:
            u, q = uq
            u_parts, y_parts, s_olds = [], [], []
            for sg in range(n_seg):
                rs = slice(sg * seg_len, (sg + 1) * seg_len)
                s_sg = pack_pair(s0_ref[u * n_seg + sg, 2 * q], s0_ref[u * n_seg + sg, 2 * q + 1])
                t1 = _mm(jnp.concatenate([a_hat[uq][rs], r_hat[uq][rs]], axis=0), s_sg, "nt", P_STATE)
                u_parts.append(t1[:seg_len] + v_hat[uq][rs])
                y_parts.append(t1[seg_len:] + z[uq][rs, LANES:])
                s_olds.append(s_sg)
            ys[uq] = jnp.concatenate(y_parts, axis=0)
            uv_t = jnp.concatenate(u_parts + [part(v, *uq)], axis=0).T
            bkh = jnp.concatenate([part(b_h, *uq), part(k_h, *uq)], axis=0)
            w_cq = part(w_c, *uq)
            for sg in range(n_seg):
                ds = _mm(uv_t, jnp.where(row_seg == sg, bkh, 0.0), "nn", P_STATE)
                store_pair(sout_ref, u * n_seg + sg, q,
                           s_olds[sg] * w_cq[sg * seg_len:sg * seg_len + 1] + jnp.where(same_head, ds, 0.0))

    y = jnp.concatenate([jnp.concatenate([ys[u, q] for q in range(n_pairs)], axis=1)
                         for u in range(n_units)], axis=0)
    mean = headsum(y) * (1.0 / RW_N)
    d = y - mean
    var = headsum(d * d) * (1.0 / RW_N)
    y = d * lax.rsqrt(var + GN_EPS) * lng_ref[...] + lnb_ref[...]
    y = y + headsum(r * k * rk_ref[...]) * v
    y = (y * g).astype(out_ref.dtype)
    for u in range(n_units):
        out_ref[u, 0] = y[rows_of(u)]

    if not has_state:
        @pl.when(pl.program_id(1) == pl.num_programs(1) - 1)
        def _():
            for u, q in units:
                store_pair(sout_ref, u, q, sbd_ref[u, q])


def _rwkv_mixer(rw, batch, seq, shift0, wkv0, mu, w0, w2, a0, a2, g2, k_k, k_a, r_k, lnx_g, lnx_b):
    n_shift = rw.shape[1]
    lora_w, lora_a = w2.shape[0], a2.shape[0]
    assert lora_w + lora_a == LANES and g2.shape[0] == LANES and n_shift == 3 * RW + 2 * LANES
    has_state = shift0 is not None
    C = CHUNK
    n_pairs = RW // LANES
    if has_state:
        assert C % seq == 0 and batch % (C // seq) == 0
        n_seg, n_chunks, n_groups = C // seq, 1, batch * seq // C
    else:
        assert seq % C == 0
        n_seg, n_chunks, n_groups = 1, seq // C, batch
    nu = WKV_UNITS if n_groups % WKV_UNITS == 0 else 1
    seg_len = C // n_seg
    w2a2 = jnp.zeros((LANES, 2 * RW), F32).at[:lora_w, :RW].set(w2).at[lora_w:, RW:].set(a2).astype(BF16)
    t = np.arange(C)
    same_seg = (t[:, None] // seg_len) == (t[None, :] // seg_len)
    ltri = jnp.asarray(((t[:, None] >= t[None, :]) & same_seg).astype(np.float32), BF16)
    elast = jnp.asarray((t[None, :] == (t[:, None] // seg_len) * seg_len + seg_len - 1).astype(np.float32), BF16)
    ones_bd = jnp.asarray(np.kron(np.eye(2, dtype=np.float32), np.ones((RW_N, RW_N), np.float32)), BF16)
    row = lambda x: x.reshape(1, -1).astype(F32)
    rw4 = rw.reshape(n_groups, n_chunks, C, n_shift)
    blk = lambda w: pl.BlockSpec((nu, 1, C, w), lambda i, c: (i, c, 0, 0))
    st_blk = pl.BlockSpec((nu * n_seg, RW_HEADS, RW_N, RW_N), lambda i, c: (i, 0, 0, 0))
    args, specs = [rw4], [blk(n_shift)]
    if has_state:
        p0 = jnp.pad(shift0[:, None, :], ((0, 0), (0, seq - 1), (0, 0))).reshape(n_groups, 1, C, n_shift)
        args += [p0, wkv0]
        specs += [blk(n_shift), st_blk]
    consts = [row(mu), w2a2, row(w0), row(a0), g2.astype(BF16), row(k_k), row(k_a), row(r_k), row(lnx_g),
              row(lnx_b), ltri] + ([elast] if n_seg > 1 else []) + [ones_bd]
    args += consts
    specs += [_resident(c.shape) for c in consts]
    scratch = [] if has_state else [pltpu.VMEM((nu, 1, n_shift), F32),
                                    pltpu.VMEM((nu, n_pairs, LANES, LANES), F32)]
    out, s_new = pl.pallas_call(
        functools.partial(_wkv_kernel, n_units=nu, n_seg=n_seg, has_state=has_state, lora_w=lora_w),
        grid=(n_groups // nu, n_chunks),
        in_specs=specs,
        out_specs=[blk(RW), st_blk],
        out_shape=[
            jax.ShapeDtypeStruct((n_groups, n_chunks, C, RW), BF16),
            jax.ShapeDtypeStruct((batch, RW_HEADS, RW_N, RW_N), F32),
        ],
        scratch_shapes=scratch,
        compiler_params=_cparams(("arbitrary", "arbitrary")),
        name="rwkv7",
    )(*args)
    return out.reshape(batch * seq, RW), s_new


def _mix_kernel(x_ref, att_ref, rwo_ref, gate_ref, wpa_ref, wpb_ref, wo_ref, o_ref):
    d = x_ref.shape[1]
    pa = jnp.dot(att_ref[...], wpa_ref[...], preferred_element_type=F32)
    pb = jnp.dot(rwo_ref[...], wpb_ref[...], preferred_element_type=F32)
    mix = gate_ref[:, :d] * pa + gate_ref[:, d:] * pb
    o_ref[...] = x_ref[...] + jnp.dot(mix.astype(BF16), wo_ref[...], preferred_element_type=F32)


def _mix(x2d, att, rwo, gates, w_pa, w_pb, w_o, tm):
    n, d = x2d.shape
    rows = lambda w: pl.BlockSpec((tm, w), lambda i: (i, 0))
    full = lambda a: _resident(a.shape)
    return pl.pallas_call(
        _mix_kernel,
        grid=(n // tm,),
        in_specs=[rows(d), rows(att.shape[1]), rows(rwo.shape[1]), rows(gates.shape[1]),
                  full(w_pa), full(w_pb), full(w_o)],
        out_specs=rows(d),
        out_shape=jax.ShapeDtypeStruct((n, d), F32),
        compiler_params=_cparams(("parallel",)),
        name="branch_mix",
    )(x2d, att, rwo, gates, w_pa, w_pb, w_o)


def _rms(x, g):
    ms = jnp.mean(x * x, axis=-1, keepdims=True)
    return x * lax.rsqrt(ms + NORM_EPS) * g


def _gelu_tanh(c):
    return c * (0.5 * (1.0 + jnp.tanh(math.sqrt(2.0 / math.pi) * (c + 0.044715 * (c * c * c)))))


def _conv_ffn_cols(h, x, wup_ref, cw_ref, cb_ref, wdown_ref, shifted, ug_sink):
    d_ff = cb_ref.shape[1]
    tiles = -(-d_ff // MXU_WIDTH)
    edges = [min(d_ff, MXU_WIDTH * ((tiles * j + FFN_COL_CHUNKS - 1) // FFN_COL_CHUNKS))
             for j in range(FFN_COL_CHUNKS + 1)]
    acc = x
    for lo_c, hi_c in zip(edges[:-1], edges[1:]):
        cs = slice(lo_c, hi_c)
        ug_full = jnp.dot(h, wup_ref[:, lo_c:hi_c], preferred_element_type=F32)
        uv_full = jnp.dot(h, wup_ref[:, d_ff + lo_c:d_ff + hi_c], preferred_element_type=F32)
        ug, ug_m1, ug_m2, uv = shifted(ug_full, uv_full, cs)
        c = cb_ref[:, cs] + cw_ref[0:1, cs] * ug_m2 + cw_ref[1:2, cs] * ug_m1 + cw_ref[2:3, cs] * ug
        act = (_gelu_tanh(c) * uv).astype(BF16)
        acc = acc + jnp.dot(act, wdown_ref[lo_c:hi_c, :], preferred_element_type=F32)
        ug_sink(ug, cs)
    return acc


def _ffn_prompt_kernel(x_ref, halo_ref, g2_ref, wup_ref, cw_ref, cb_ref, wdown_ref, gf_ref, y_ref, ug_ref):
    tm = x_ref.shape[0]
    x = x_ref[...]
    xe = jnp.concatenate([halo_ref[...], x], axis=0)
    h = _rms(xe, g2_ref[...]).astype(BF16)
    row = lax.broadcasted_iota(jnp.int32, (SUBLANES + tm, 1), 0)
    before_start = jnp.logical_and(pl.program_id(1) == 0, row < SUBLANES)

    def shifted(ug_e, uv_e, cs):
        ug_e = jnp.where(before_start, 0.0, ug_e)
        return (ug_e[SUBLANES:], pltpu.roll(ug_e, 1, 0)[SUBLANES:], pltpu.roll(ug_e, 2, 0)[SUBLANES:],
                uv_e[SUBLANES:])

    def ug_sink(ug, cs):
        ug_ref[:, cs] = ug[tm - SUBLANES:]

    x2 = _conv_ffn_cols(h, x, wup_ref, cw_ref, cb_ref, wdown_ref, shifted, ug_sink)
    y_ref[...] = _rms(x2, gf_ref[...])


def _ffn_sample_kernel(x_ref, e_ref, g2_ref, wup_ref, cw_ref, cb_ref, wdown_ref, gf_ref, y_ref, ug_ref, *, seq):
    rows = x_ref.shape[0]
    x = x_ref[...]
    h = _rms(x, g2_ref[...]).astype(BF16)
    t = lax.broadcasted_iota(jnp.int32, (rows, 1), 0) % seq

    def shifted(ug, uv, cs):
        e = e_ref[:, cs]
        ug_m1 = jnp.where(t == 0, pltpu.roll(e, rows - 1, 0), pltpu.roll(ug, 1, 0))
        ug_m2 = jnp.where(t < 2, e, pltpu.roll(ug, 2, 0))
        return ug, ug_m1, ug_m2, uv

    def ug_sink(ug, cs):
        ug_ref[:, cs] = ug

    x2 = _conv_ffn_cols(h, x, wup_ref, cw_ref, cb_ref, wdown_ref, shifted, ug_sink)
    y_ref[...] = _rms(x2, gf_ref[...])


def _ffn_prompt(x1, batch, seq, norm2_g, w_up, conv_w, conv_b, w_down, final_g, tm):
    n, d = x1.shape
    d_ff = conv_b.shape[0]
    nt = seq // tm
    full = lambda a: _resident(a.shape)
    g2, cb, gf = norm2_g.reshape(1, d), conv_b.reshape(1, d_ff), final_g.reshape(1, d)
    hb = tm // SUBLANES
    y, ug_last = pl.pallas_call(
        _ffn_prompt_kernel,
        grid=(batch, nt),
        in_specs=[
            pl.BlockSpec((tm, d), lambda b, i: (b * nt + i, 0)),
            pl.BlockSpec((SUBLANES, d), lambda b, i: (jnp.maximum((b * nt + i) * hb - 1, 0), 0)),
            full(g2), full(w_up), full(conv_w), full(cb), full(w_down), full(gf),
        ],
        out_specs=[
            pl.BlockSpec((tm, d), lambda b, i: (b * nt + i, 0)),
            pl.BlockSpec((SUBLANES, d_ff), lambda b, i: (b, 0)),
        ],
        out_shape=[
            jax.ShapeDtypeStruct((n, d), F32),
            jax.ShapeDtypeStruct((batch * SUBLANES, d_ff), F32),
        ],
        compiler_params=_cparams(("arbitrary", "arbitrary")),
        name="conv_ffn_prompt",
    )(x1, x1, g2, w_up, conv_w, cb, w_down, gf)
    conv_new = ug_last.reshape(batch, SUBLANES, d_ff)[:, SUBLANES - (CONV_W - 1):]
    return y, conv_new


def _ffn_sample(x1, batch, seq, conv0, norm2_g, w_up, conv_w, conv_b, w_down, final_g, bt):
    n, d = x1.shape
    d_ff = conv_b.shape[0]
    rows = bt * seq
    e = jnp.pad(conv0, ((0, 0), (0, seq - (CONV_W - 1)), (0, 0))).reshape(n, d_ff)
    full = lambda a: _resident(a.shape)
    g2, cb, gf = norm2_g.reshape(1, d), conv_b.reshape(1, d_ff), final_g.reshape(1, d)
    y, ug = pl.pallas_call(
        functools.partial(_ffn_sample_kernel, seq=seq),
        grid=(n // rows,),
        in_specs=[
            pl.BlockSpec((rows, d), lambda i: (i, 0)),
            pl.BlockSpec((rows, d_ff), lambda i: (i, 0)),
            full(g2), full(w_up), full(conv_w), full(cb), full(w_down), full(gf),
        ],
        out_specs=[
            pl.BlockSpec((rows, d), lambda i: (i, 0)),
            pl.BlockSpec((rows, d_ff), lambda i: (i, 0)),
        ],
        out_shape=[jax.ShapeDtypeStruct((n, d), F32), jax.ShapeDtypeStruct((n, d_ff), F32)],
        compiler_params=_cparams(("parallel",)),
        name="conv_ffn_sample",
    )(x1, e, g2, w_up, conv_w, cb, w_down, gf)
    conv_new = ug.reshape(batch, seq, d_ff)[:, seq - (CONV_W - 1):]
    return y, conv_new


def _row_tile(n, want):
    t = min(want, n)
    while n % t:
        t //= 2
    return t


def _layer(x, is_prompt, state, rel_bias, lw, final_g):
    batch, seq, d = x.shape
    n = batch * seq
    x2d = x.reshape(n, d)
    n_qkv = Q_W + 2 * KV_W
    n_rw = lw["mu_shift"].shape[0]
    tm = _row_tile(n, 512)
    qkv, rw, gates = _in_proj(x2d, lw["norm1_g"], lw["w_in"], n_qkv, n_rw, tm)
    kv = qkv.reshape(batch, seq, n_qkv)
    k_new = kv[:, :, Q_W:Q_W + KV_W]
    v_new = kv[:, :, Q_W + KV_W:]
    if is_prompt:
        att = _attention_prompt(qkv, batch, seq, rel_bias, lw["sinks"])
        wb = state["win_buf"]
        k_rows, v_rows = k_new[:, seq - wb:], v_new[:, seq - wb:]
        shift0 = wkv0 = None
    else:
        cache_k, cache_v = state["cache_k"], state["cache_v"]
        wb = cache_k.shape[1]
        assert seq <= wb
        att, k_rows, v_rows = _attention_sample(qkv, batch, seq, cache_k, cache_v, rel_bias, lw["sinks"],
                                                _row_tile(batch, 16))
        shift0, wkv0 = state["shift"], state["wkv"]
    k_rows = k_rows.reshape(batch, -1, N_KV, HEAD_DIM)
    v_rows = v_rows.reshape(batch, -1, N_KV, HEAD_DIM)
    rwo, wkv_new = _rwkv_mixer(rw, batch, seq, shift0, wkv0, lw["mu_shift"], lw["w0"], lw["w2"], lw["a0"],
                               lw["a2"], lw["g2"], lw["k_k"], lw["k_a"], lw["r_k"], lw["lnx_g"], lw["lnx_b"])
    shift_new = rw.reshape(batch, seq, n_rw)[:, seq - 1]
    x1 = _mix(x2d, att, rwo, gates, lw["w_pa"], lw["w_pb"], lw["w_o"], tm)
    if is_prompt:
        y, conv_new = _ffn_prompt(x1, batch, seq, lw["norm2_g"], lw["w_up"], lw["conv_w"], lw["conv_b"],
                                  lw["w_down"], final_g, _row_tile(seq, 512))
    else:
        y, conv_new = _ffn_sample(x1, batch, seq, state["conv"], lw["norm2_g"], lw["w_up"], lw["conv_w"],
                                  lw["conv_b"], lw["w_down"], final_g, _row_tile(batch, 64))
    return y.reshape(batch, seq, d), (k_rows, v_rows, shift_new, wkv_new, conv_new)


def kernel(x_prompt, x_sample, cache_win_k, cache_win_v, state_shift, state_wkv, state_conv, rel_bias, norm1_g,
           w_in, sinks, mu_shift, w0, w2, a0, a2, g2, k_k, k_a, r_k, lnx_g, lnx_b, w_pa, w_pb, w_o, norm2_g,
           w_up, conv_w, conv_b, w_down, final_g):
    depth = w_in.shape[0]
    assert depth == 1, "the final norm is fused into the layer's last kernel"
    l = 0
    lw = dict(norm1_g=norm1_g[l], w_in=w_in[l].astype(BF16), sinks=sinks[l], mu_shift=mu_shift[l], w0=w0[l],
              w2=w2[l], a0=a0[l], a2=a2[l], g2=g2[l], k_k=k_k[l], k_a=k_a[l], r_k=r_k[l].reshape(-1),
              lnx_g=lnx_g[l], lnx_b=lnx_b[l], w_pa=w_pa[l].astype(BF16), w_pb=w_pb[l].astype(BF16),
              w_o=w_o[l].astype(BF16), norm2_g=norm2_g[l], w_up=w_up[l].astype(BF16), conv_w=conv_w[l],
              conv_b=conv_b[l], w_down=w_down[l].astype(BF16))
    win_buf = cache_win_k.shape[2]
    y_p, st_p = _layer(x_prompt, True, dict(win_buf=win_buf), rel_bias, lw, final_g)
    y_s, st_s = _layer(x_sample, False,
                       dict(cache_k=cache_win_k[l], cache_v=cache_win_v[l], shift=state_shift[l],
                            wkv=state_wkv[l], conv=state_conv[l]), rel_bias, lw, final_g)
    stack = lambda t: t[None]
    return (y_p, y_s) + tuple(stack(t) for t in st_p) + tuple(stack(t) for t in st_s)
```

```python
import functools
import math

import numpy as np
import jax
import jax.numpy as jnp
from jax import lax
from jax.experimental import pallas as pl
from jax.experimental.pallas import tpu as pltpu

F32 = jnp.float32
BF16 = jnp.bfloat16

HEAD_DIM = 64
N_HEADS = 8
N_KV = 2
WINDOW = 128
N_BUCKETS = 32
MAX_EXACT = N_BUCKETS // 2
REL_MAX_DIST = 128
RW_N = 64
RW_HEADS = 8
RW = RW_HEADS * RW_N
NORM_EPS = 1e-6
GN_EPS = 64e-5
NEG = -1e30
CONV_W = 3

Q_W = N_HEADS * HEAD_DIM
KV_W = N_KV * HEAD_DIM
LANES = 128
SUBLANES = 8
MXU_WIDTH = 256
CHUNK = 64
VMEM_LIMIT = 56 * 1024 * 1024
FFN_COL_CHUNKS = 3
FFN_ROWS = 512
FFN_ROWS_WITH_STATE = 256
ATTN_BLOCKS = 2
WKV_UNITS = 4


def _resident(shape):
    return pl.BlockSpec(shape, lambda *_: (0,) * len(shape), pipeline_mode=pl.Buffered(1))


def _cparams(sem):
    return pltpu.CompilerParams(dimension_semantics=sem, vmem_limit_bytes=VMEM_LIMIT)


def _sigmoid(x):
    return 1.0 / (1.0 + jnp.exp(-x))


def _dg(a, b, kind):
    if kind == "nn":
        dn = (((1,), (0,)), ((), ()))
    else:
        dn = (((1,), (1,)), ((), ()))
    return lax.dot_general(a, b, dn, preferred_element_type=F32)


def _split(x):
    hi = x.astype(BF16)
    lo = (x - hi.astype(F32)).astype(BF16)
    return hi, lo


def _mm(a, b, kind="nn", passes=1):
    if passes == 1:
        return _dg(a.astype(BF16), b.astype(BF16), kind)
    ah, al = _split(a)
    bh, bl = _split(b)
    return _dg(ah, bh, kind) + (_dg(ah, bl, kind) + _dg(al, bh, kind))


def _mm_exact_lhs(a_bf16, b, n_terms):
    out = None
    rem = b
    for _ in range(n_terms):
        piece = rem.astype(BF16)
        term = _dg(a_bf16, piece, "nn")
        out = term if out is None else out + term
        rem = rem - piece.astype(F32)
    return out


def _mm_exact_rhs(a, b_bf16, n_terms):
    out = None
    rem = a
    for _ in range(n_terms):
        piece = rem.astype(BF16)
        term = _dg(piece, b_bf16, "nn")
        out = term if out is None else out + term
        rem = rem - piece.astype(F32)
    return out


def _inproj_kernel(x_ref, g_ref, w_ref, qkv_ref, rw_ref, gate_ref, *, n_qkv, n_rw):
    x = x_ref[...]
    ms = jnp.mean(x * x, axis=-1, keepdims=True)
    h = (x * lax.rsqrt(ms + NORM_EPS) * g_ref[...]).astype(BF16)
    qkv_ref[...] = jnp.dot(h, w_ref[:, :n_qkv], preferred_element_type=F32)
    rw_ref[...] = jnp.dot(h, w_ref[:, n_qkv:n_qkv + n_rw], preferred_element_type=F32)
    gate = _sigmoid(jnp.dot(h, w_ref[:, n_qkv + n_rw:], preferred_element_type=F32))
    gate_ref[...] = gate.astype(gate_ref.dtype)


def _in_proj(x2d, g, w_bf16, n_qkv, n_rw, tm):
    n, d = x2d.shape
    n_gate = w_bf16.shape[1] - n_qkv - n_rw
    return pl.pallas_call(
        functools.partial(_inproj_kernel, n_qkv=n_qkv, n_rw=n_rw),
        grid=(n // tm,),
        in_specs=[
            pl.BlockSpec((tm, d), lambda i: (i, 0)),
            _resident((1, d)),
            _resident(w_bf16.shape),
        ],
        out_specs=[
            pl.BlockSpec((tm, n_qkv), lambda i: (i, 0)),
            pl.BlockSpec((tm, n_rw), lambda i: (i, 0)),
            pl.BlockSpec((tm, n_gate), lambda i: (i, 0)),
        ],
        out_shape=[
            jax.ShapeDtypeStruct((n, n_qkv), F32),
            jax.ShapeDtypeStruct((n, n_rw), F32),
            jax.ShapeDtypeStruct((n, n_gate), BF16),
        ],
        compiler_params=_cparams(("parallel",)),
        name="in_proj",
    )(x2d, g.reshape(1, d), w_bf16)


def _t5_bucket_np(dist):
    n = np.maximum(dist, 0)
    nf = np.maximum(n, 1).astype(np.float32)
    large = MAX_EXACT + (np.log(nf / MAX_EXACT) / math.log(REL_MAX_DIST / MAX_EXACT)
                         * (N_BUCKETS - MAX_EXACT)).astype(np.int32)
    return np.where(n < MAX_EXACT, n, np.minimum(large, N_BUCKETS - 1)).astype(np.int32)


def _attn_kernel(q_ref, k1_ref, k2_ref, v1_ref, v2_ref, bucket_ref, relb_ref, sink_ref, o_ref, *rest,
                 nq, nk, n_blocks, first_block_axis, emit_window):
    bt = q_ref.shape[0]
    bias_ref = rest[-1]
    if emit_window:
        n_old, n_new = k1_ref.shape[1], k2_ref.shape[1]
        for w_ref, old_ref, new_ref in ((rest[0], k1_ref, k2_ref), (rest[1], v1_ref, v2_ref)):
            w_ref[:, :n_old - n_new] = old_ref[:, n_new:]
            w_ref[:, n_old - n_new:] = new_ref[...]
    first_step = pl.program_id(0) == 0
    if first_block_axis is not None:
        first_step = jnp.logical_and(first_step, pl.program_id(1) == 0)

    @pl.when(first_step)
    def _():
        bucket = bucket_ref[...]
        prev_key = lax.broadcasted_iota(jnp.int32, (1, nk), 1) < (nk // 2)
        for n in range(N_HEADS):
            acc = jnp.full((nq, nk), NEG, F32)
            for b in range(N_BUCKETS):
                acc = jnp.where(bucket == b, relb_ref[b, n], acc)
            c, half = divmod(n, 2)
            bias_ref[0, c, :, half * nk:(half + 1) * nk] = acc
            if first_block_axis is not None:
                bias_ref[1, c, :, half * nk:(half + 1) * nk] = jnp.where(prev_key, NEG, acc)

    n_keys = nk + (n_blocks - 1) * nq

    def padded(a_ref, b_ref):
        parts = [a_ref[...], b_ref[...]]
        n_now = a_ref.shape[1] + b_ref.shape[1]
        if n_now < n_keys:
            parts.append(jnp.zeros((bt, n_keys - n_now, LANES), F32))
        return jnp.concatenate(parts, axis=1)

    kk = padded(k1_ref, k2_ref)
    vv = padded(v1_ref, v2_ref)
    kk_r = pltpu.roll(kk, HEAD_DIM, 2)
    vv_r = pltpu.roll(vv, HEAD_DIM, 2)
    lane = lax.broadcasted_iota(jnp.int32, (1, 1, LANES), 2)
    lo = lane < HEAD_DIM

    def halves(x, x_r, kvh):
        src_lo, src_hi = (x, x_r) if kvh == 0 else (x_r, x)
        even = jnp.where(lo, src_lo, 0.0).astype(BF16)
        odd = jnp.where(lo, 0.0, src_hi).astype(BF16)
        return even, odd

    k_eo = [halves(kk, kk_r, h) for h in range(N_KV)]
    v_eo = [halves(vv, vv_r, h) for h in range(N_KV)]

    def window(eo, j):
        return jnp.concatenate([eo[0][:, j * nq:j * nq + nk], eo[1][:, j * nq:j * nq + nk]], axis=1)

    n_cols = N_HEADS // 2
    units = [(j, c) for j in range(n_blocks) for c in range(n_cols)]
    kvh_of = lambda c: (2 * c) // (N_HEADS // N_KV)
    k_win = {(j, h): window(k_eo[h], j) for j in range(n_blocks) for h in range(N_KV)}
    v_win = {(j, h): window(v_eo[h], j) for j in range(n_blocks) for h in range(N_KV)}

    scale = HEAD_DIM ** -0.5
    first_table = 0
    if first_block_axis is not None:
        first_table = jnp.where(pl.program_id(first_block_axis) == 0, 1, 0)
    s_all = {}
    for j, c in units:
        qc = (q_ref[:, j * nq:(j + 1) * nq, c * LANES:(c + 1) * LANES] * scale).astype(BF16)
        s_all[j, c] = jnp.einsum("bqd,bkd->bqk", qc, k_win[j, kvh_of(c)], preferred_element_type=F32)
    e_all, inv_all = {}, {}
    for j, c in units:
        s = s_all[j, c] + bias_ref[first_table if j == 0 else 0, c][None]
        es, invs = [], []
        for half in range(2):
            sh = s[:, :, half * nk:(half + 1) * nk]
            sink = sink_ref[2 * c + half]
            m = jnp.maximum(jnp.max(sh, axis=-1, keepdims=True), sink)
            e = jnp.exp(sh - m)
            invs.append(1.0 / (jnp.sum(e, axis=-1, keepdims=True) + jnp.exp(sink - m)))
            es.append(e.astype(BF16))
        e_all[j, c] = jnp.concatenate(es, axis=2)
        inv_all[j, c] = jnp.where(lo, invs[0], invs[1])
    for j, c in units:
        o = jnp.einsum("bqk,bkd->bqd", e_all[j, c], v_win[j, kvh_of(c)], preferred_element_type=F32)
        o_ref[:, j * nq:(j + 1) * nq, c * LANES:(c + 1) * LANES] = (o * inv_all[j, c]).astype(o_ref.dtype)


def _attention_prompt(qkv, batch, seq, rel_bias, sinks):
    nblk = seq // WINDOW
    nb = ATTN_BLOCKS if nblk % ATTN_BLOCKS == 0 else 1
    nsteps = nblk // nb
    width = qkv.shape[1]
    q_blk = qkv.reshape(batch * nblk, WINDOW, width)
    q_step = qkv.reshape(batch * nsteps, nb * WINDOW, width)
    kcol = Q_W // LANES
    vcol = (Q_W + KV_W) // LANES
    nk = 2 * WINDOW
    qi = np.arange(WINDOW)[:, None] + WINDOW
    kj = np.arange(nk)[None, :]
    dist = qi - kj
    bucket = np.where((dist >= 0) & (dist < WINDOW), _t5_bucket_np(dist), -1).astype(np.int32)
    cur = lambda c: (lambda b, i: (b * nsteps + i, 0, c))
    prev = lambda c: (lambda b, i: (b * nblk + jnp.maximum(i * nb - 1, 0), 0, c))
    out = pl.pallas_call(
        functools.partial(_attn_kernel, nq=WINDOW, nk=nk, n_blocks=nb, first_block_axis=1, emit_window=False),
        grid=(batch, nsteps),
        in_specs=[
            pl.BlockSpec((1, nb * WINDOW, Q_W), cur(0)),
            pl.BlockSpec((1, WINDOW, LANES), prev(kcol)),
            pl.BlockSpec((1, nb * WINDOW, LANES), cur(kcol)),
            pl.BlockSpec((1, WINDOW, LANES), prev(vcol)),
            pl.BlockSpec((1, nb * WINDOW, LANES), cur(vcol)),
            _resident(bucket.shape),
            pl.BlockSpec(memory_space=pltpu.SMEM),
            pl.BlockSpec(memory_space=pltpu.SMEM),
        ],
        out_specs=pl.BlockSpec((1, nb * WINDOW, Q_W), cur(0)),
        out_shape=jax.ShapeDtypeStruct((batch * nsteps, nb * WINDOW, Q_W), BF16),
        scratch_shapes=[pltpu.VMEM((2, N_HEADS // 2, WINDOW, 2 * nk), F32)],
        compiler_params=_cparams(("arbitrary", "arbitrary")),
        name="attn_prompt",
    )(q_step, q_blk, q_step, q_blk, q_step, jnp.asarray(bucket), rel_bias, sinks)
    return out.reshape(batch * seq, Q_W)


def _attention_sample(qkv, batch, seq, cache_k, cache_v, rel_bias, sinks, bt):
    wb = cache_k.shape[1]
    nk = 2 * WINDOW
    q3 = qkv.reshape(batch, seq, qkv.shape[1])
    ck = cache_k.reshape(batch, wb, KV_W)
    cv = cache_v.reshape(batch, wb, KV_W)
    kcol = Q_W // LANES
    vcol = (Q_W + KV_W) // LANES
    tq = np.arange(seq)[:, None]
    j = np.arange(nk)[None, :]
    dist = np.where(j < wb, tq + wb - j, tq - (j - wb))
    ok = (dist >= 0) & (dist < WINDOW) & (j < wb + seq)
    bucket = np.where(ok, _t5_bucket_np(dist), -1).astype(np.int32)
    win_spec = pl.BlockSpec((bt, wb, LANES), lambda b: (b, 0, 0))
    win_shape = jax.ShapeDtypeStruct((batch, wb, KV_W), F32)
    out, win_k, win_v = pl.pallas_call(
        functools.partial(_attn_kernel, nq=seq, nk=nk, n_blocks=1, first_block_axis=None, emit_window=True),
        grid=(batch // bt,),
        in_specs=[
            pl.BlockSpec((bt, seq, Q_W), lambda b: (b, 0, 0)),
            pl.BlockSpec((bt, wb, LANES), lambda b: (b, 0, 0)),
            pl.BlockSpec((bt, seq, LANES), lambda b: (b, 0, kcol)),
            pl.BlockSpec((bt, wb, LANES), lambda b: (b, 0, 0)),
            pl.BlockSpec((bt, seq, LANES), lambda b: (b, 0, vcol)),
            _resident(bucket.shape),
            pl.BlockSpec(memory_space=pltpu.SMEM),
            pl.BlockSpec(memory_space=pltpu.SMEM),
        ],
        out_specs=[pl.BlockSpec((bt, seq, Q_W), lambda b: (b, 0, 0)), win_spec, win_spec],
        out_shape=[jax.ShapeDtypeStruct((batch, seq, Q_W), BF16), win_shape, win_shape],
        scratch_shapes=[pltpu.VMEM((1, N_HEADS // 2, seq, 2 * nk), F32)],
        compiler_params=_cparams(("arbitrary",)),
        name="attn_sample",
    )(q3, ck, q3, cv, q3, jnp.asarray(bucket), rel_bias, sinks)
    return out.reshape(batch * seq, Q_W), win_k, win_v


P_SCORE = 1
P_TINV = 1
P_INTRA = 1
P_STATE = 1


def _wkv_kernel(*refs, n_units, n_seg, has_state, lora_w):
    C = CHUNK
    seg_len = C // n_seg
    n_pairs = RW // LANES
    it = iter(refs)
    p_ref = next(it)
    p0_ref, s0_ref = (next(it), next(it)) if has_state else (None, None)
    (mu_ref, w2a2_ref, w0_ref, a0_ref, g2_ref, kk_ref, ka_ref, rk_ref, lng_ref, lnb_ref, ltri_ref) = (
        next(it) for _ in range(11))
    elast_ref = next(it) if n_seg > 1 else None
    ones_ref, out_ref, sout_ref = next(it), next(it), next(it)
    last_ref, sbd_ref = (None, None) if has_state else (next(it), next(it))

    if not has_state:
        @pl.when(pl.program_id(1) == 0)
        def _():
            last_ref[...] = jnp.zeros_like(last_ref)
            sbd_ref[...] = jnp.zeros_like(sbd_ref)

    row = lax.broadcasted_iota(jnp.int32, (C, 1), 0)
    rows_of = lambda u: slice(u * C, (u + 1) * C)
    lane = lax.broadcasted_iota(jnp.int32, (1, LANES), 1)
    lo = lane < RW_N
    ones_bd = ones_ref[...]

    def headsum(x):
        n_rows = x.shape[0]
        stacked = jnp.concatenate([x[:, q * LANES:(q + 1) * LANES] for q in range(n_pairs)], axis=0)
        s = _mm_exact_rhs(stacked, ones_bd, 1)
        return jnp.concatenate([s[q * n_rows:(q + 1) * n_rows] for q in range(n_pairs)], axis=1)

    def prepare():
        xs_parts = []
        for u in range(n_units):
            p = p_ref[u, 0]
            rolled = pltpu.roll(p, 1, 0)
            if has_state:
                prev = jnp.where(row % seg_len == 0, p0_ref[u, 0], rolled)
            else:
                prev = jnp.where(row == 0, last_ref[u], rolled)
                last_ref[u] = p_ref[u, 0, C - 1:C, :]
            xs_parts.append(p + mu_ref[...] * (prev - p))
        xs = jnp.concatenate(xs_parts, axis=0)
        r = xs[:, 0:RW]
        k = xs[:, RW:2 * RW]
        v = xs[:, 2 * RW:3 * RW]
        lwla = xs[:, 3 * RW:3 * RW + LANES]
        lg = xs[:, 3 * RW + LANES:3 * RW + 2 * LANES]
        lwla = jnp.where(lane < lora_w, jnp.tanh(lwla), lwla)
        wa = jnp.dot(lwla.astype(BF16), w2a2_ref[...], preferred_element_type=F32)
        logw = -math.exp(-0.5) * _sigmoid(w0_ref[...] + wa[:, :RW])
        a_sig = _sigmoid(a0_ref[...] + wa[:, RW:])
        g = jnp.dot(_sigmoid(lg).astype(BF16), g2_ref[...], preferred_element_type=F32)
        kk = k * kk_ref[...]
        kk = kk * (1.0 / jnp.maximum(jnp.sqrt(headsum(kk * kk)), 1e-12))
        k = k * (1.0 + (a_sig - 1.0) * ka_ref[...])
        a = -kk
        b = kk * a_sig
        bonus = headsum(r * k * rk_ref[...]) * v
        cws, cwl = [], []
        for u in range(n_units):
            cw_u = _mm_exact_lhs(ltri_ref[...], logw[rows_of(u)], 3)
            cws.append(cw_u)
            if n_seg == 1:
                cwl.append(jnp.broadcast_to(cw_u[C - 1:C, :], (C, RW)))
            else:
                cwl.append(_mm_exact_lhs(elast_ref[...], cw_u, 3))
        cw = jnp.concatenate(cws, axis=0)
        cw_last = jnp.concatenate(cwl, axis=0)
        w_inv = jnp.exp(-cw)
        w_tail = jnp.exp(cw_last - cw)
        return (a * jnp.exp(cw - logw), r * jnp.exp(cw), b * w_inv, k * w_inv, b * w_tail, k * w_tail, v,
                jnp.exp(cw_last),
                bonus, g)

    a_t, r_t, b_t, k_t, b_h, k_h, v, w_c, bonus, g = prepare()

    def bd(y):
        return jnp.concatenate([jnp.where(lo, y, 0.0), jnp.where(lo, 0.0, y)], axis=0)

    zeros_head = jnp.zeros((RW_N, RW_N), F32)

    def pack_pair(s_even, s_odd):
        return jnp.concatenate([jnp.concatenate([s_even, zeros_head], axis=1),
                                jnp.concatenate([zeros_head, s_odd], axis=1)], axis=0)

    def store_pair(ref, i, q, s_pair):
        ref[i, 2 * q] = s_pair[:RW_N, :RW_N]
        ref[i, 2 * q + 1] = s_pair[RW_N:, RW_N:]

    s_idx = lane % C
    strict = s_idx < row
    incl = s_idx <= row
    if n_seg > 1:
        same_seg = (s_idx // seg_len) == (row // seg_len)
        strict = jnp.logical_and(strict, same_seg)
        incl = jnp.logical_and(incl, same_seg)
    row2 = lax.broadcasted_iota(jnp.int32, (2 * C, 1), 0)
    same_head = (row2 < RW_N) == lo

    units = [(u, q) for u in range(n_units) for q in range(n_pairs)]
    part = lambda x, u, q: x[u * C:(u + 1) * C, q * LANES:(q + 1) * LANES]
    sc = {uq: _mm(jnp.concatenate([part(a_t, *uq), part(r_t, *uq)], axis=0),
                  jnp.concatenate([bd(part(b_t, *uq)), bd(part(k_t, *uq))], axis=0),
                  "nt", P_SCORE) for uq in units}
    pw = {uq: jnp.where(strict, sc[uq][:C, :LANES], 0.0) for uq in units}
    m_rb = {uq: jnp.where(incl, sc[uq][C:, :LANES], 0.0) for uq in units}
    lm_v = {uq: _mm(jnp.concatenate([jnp.where(strict, sc[uq][:C, LANES:], 0.0),
                                     jnp.where(incl, sc[uq][C:, LANES:], 0.0)], axis=0),
                    bd(part(v, *uq)), "nn", P_INTRA) for uq in units}
    lak_v = {uq: lm_v[uq][:C] for uq in units}
    tinv = {uq: pw[uq] + jnp.where(s_idx == row, 1.0, 0.0) for uq in units}
    n_lvl = int(math.log2(seg_len))
    for lvl in range(1, n_lvl):
        last = lvl + 1 == n_lvl
        for uq in units:
            if lvl == 1:
                pw[uq] = _mm(pw[uq], bd(pw[uq]), "nn", P_TINV)
            rhs = bd(tinv[uq]) if last else jnp.concatenate([bd(tinv[uq]), bd(pw[uq])], axis=1)
            upd = _mm(pw[uq], rhs, "nn", P_TINV)
            tinv[uq] = tinv[uq] + upd[:, :LANES]
            if not last:
                pw[uq] = upd[:, LANES:]
    x = {uq: _mm(tinv[uq], jnp.concatenate([bd(part(a_t, *uq)), bd(lak_v[uq])], axis=1), "nn", P_TINV)
         for uq in units}
    a_hat = {uq: x[uq][:, :LANES] for uq in units}
    v_hat = {uq: x[uq][:, LANES:] for uq in units}
    z = {uq: _mm(m_rb[uq], jnp.concatenate([bd(a_hat[uq]), bd(v_hat[uq])], axis=1), "nn", P_INTRA)
         for uq in units}
    r_hat = {uq: part(r_t, *uq) + z[uq][:, :LANES] for uq in units}
    y_intra = {uq: z[uq][:, LANES:] + lm_v[uq][C:] for uq in units}
    ys = {}
    if n_seg == 1:
        s_old = {uq: sbd_ref[uq[0], uq[1]] for uq in units}
        t1 = {uq: _mm(jnp.concatenate([a_hat[uq], r_hat[uq]], axis=0), s_old[uq], "nt", P_STATE)
              for uq in units}
        for uq in units:
            ys[uq] = t1[uq][C:] + y_intra[uq]
            uv = jnp.concatenate([t1[uq][:C] + v_hat[uq], part(v, *uq)], axis=0)
            bkh = jnp.concatenate([part(b_h, *uq), part(k_h, *uq)], axis=0)
            ds = _mm(uv.T, bkh, "nn", P_STATE)
            sbd_ref[uq[0], uq[1]] = s_old[uq] * part(w_c, *uq)[0:1] + jnp.where(same_head, ds, 0.0)
    else:
        row_seg = (row2 % C) // seg_len
        for uq in units:
            u, q = uq
            u_parts, y_parts, s_olds = [], [], []
            for sg in range(n_seg):
                rs = slice(sg * seg_len, (sg + 1) * seg_len)
                s_sg = pack_pair(s0_ref[u * n_seg + sg, 2 * q], s0_ref[u * n_seg + sg, 2 * q + 1])
                t1 = _mm(jnp.concatenate([a_hat[uq][rs], r_hat[uq][rs]], axis=0), s_sg, "nt", P_STATE)
                u_parts.append(t1[:seg_len] + v_hat[uq][rs])
                y_parts.append(t1[seg_len:] + y_intra[uq][rs])
                s_olds.append(s_sg)
            ys[uq] = jnp.concatenate(y_parts, axis=0)
            uv_t = jnp.concatenate(u_parts + [part(v, *uq)], axis=0).T
            bkh = jnp.concatenate([part(b_h, *uq), part(k_h, *uq)], axis=0)
            w_cq = part(w_c, *uq)
            for sg in range(n_seg):
                ds = _mm(uv_t, jnp.where(row_seg == sg, bkh, 0.0), "nn", P_STATE)
                store_pair(sout_ref, u * n_seg + sg, q,
                           s_olds[sg] * w_cq[sg * seg_len:sg * seg_len + 1] + jnp.where(same_head, ds, 0.0))

    y = jnp.concatenate([jnp.concatenate([ys[u, q] for q in range(n_pairs)], axis=1)
                         for u in range(n_units)], axis=0)
    mean = headsum(y) * (1.0 / RW_N)
    d = y - mean
    var = headsum(d * d) * (1.0 / RW_N)
    y = d * lax.rsqrt(var + GN_EPS) * lng_ref[...] + lnb_ref[...]
    y = ((y + bonus) * g).astype(out_ref.dtype)
    for u in range(n_units):
        out_ref[u, 0] = y[rows_of(u)]

    if not has_state:
        @pl.when(pl.program_id(1) == pl.num_programs(1) - 1)
        def _():
            for u, q in units:
                store_pair(sout_ref, u, q, sbd_ref[u, q])


def _rwkv_mixer(rw, batch, seq, shift0, wkv0, mu, w0, w2, a0, a2, g2, k_k, k_a, r_k, lnx_g, lnx_b):
    n_shift = rw.shape[1]
    lora_w, lora_a = w2.shape[0], a2.shape[0]
    assert lora_w + lora_a == LANES and g2.shape[0] == LANES and n_shift == 3 * RW + 2 * LANES
    has_state = shift0 is not None
    C = CHUNK
    n_pairs = RW // LANES
    if has_state:
        assert C % seq == 0 and batch % (C // seq) == 0
        n_seg, n_chunks, n_groups = C // seq, 1, batch * seq // C
    else:
        assert seq % C == 0
        n_seg, n_chunks, n_groups = 1, seq // C, batch
    nu = WKV_UNITS if n_groups % WKV_UNITS == 0 else 1
    seg_len = C // n_seg
    w2a2 = jnp.zeros((LANES, 2 * RW), F32).at[:lora_w, :RW].set(w2).at[lora_w:, RW:].set(a2).astype(BF16)
    t = np.arange(C)
    same_seg = (t[:, None] // seg_len) == (t[None, :] // seg_len)
    ltri = jnp.asarray(((t[:, None] >= t[None, :]) & same_seg).astype(np.float32), BF16)
    elast = jnp.asarray((t[None, :] == (t[:, None] // seg_len) * seg_len + seg_len - 1).astype(np.float32), BF16)
    ones_bd = jnp.asarray(np.kron(np.eye(2, dtype=np.float32), np.ones((RW_N, RW_N), np.float32)), BF16)
    row = lambda x: x.reshape(1, -1).astype(F32)
    rw4 = rw.reshape(n_groups, n_chunks, C, n_shift)
    blk = lambda w: pl.BlockSpec((nu, 1, C, w), lambda i, c: (i, c, 0, 0))
    st_blk = pl.BlockSpec((nu * n_seg, RW_HEADS, RW_N, RW_N), lambda i, c: (i, 0, 0, 0))
    args, specs = [rw4], [blk(n_shift)]
    if has_state:
        p0 = jnp.pad(shift0[:, None, :], ((0, 0), (0, seq - 1), (0, 0))).reshape(n_groups, 1, C, n_shift)
        args += [p0, wkv0]
        specs += [blk(n_shift), st_blk]
    consts = [row(mu), w2a2, row(w0), row(a0), g2.astype(BF16), row(k_k), row(k_a), row(r_k), row(lnx_g),
              row(lnx_b), ltri] + ([elast] if n_seg > 1 else []) + [ones_bd]
    args += consts
    specs += [_resident(c.shape) for c in consts]
    scratch = [] if has_state else [pltpu.VMEM((nu, 1, n_shift), F32),
                                    pltpu.VMEM((nu, n_pairs, LANES, LANES), F32)]
    out, s_new = pl.pallas_call(
        functools.partial(_wkv_kernel, n_units=nu, n_seg=n_seg, has_state=has_state, lora_w=lora_w),
        grid=(n_groups // nu, n_chunks),
        in_specs=specs,
        out_specs=[blk(RW), st_blk],
        out_shape=[
            jax.ShapeDtypeStruct((n_groups, n_chunks, C, RW), BF16),
            jax.ShapeDtypeStruct((batch, RW_HEADS, RW_N, RW_N), F32),
        ],
        scratch_shapes=scratch,
        compiler_params=_cparams(("arbitrary", "arbitrary")),
        name="rwkv7",
    )(*args)
    return out.reshape(batch * seq, RW), s_new


def _branch_mix(x_ref, att_ref, rwo_ref, gate_ref, wpa_ref, wpb_ref, wo_ref):
    d = x_ref.shape[1]
    pa = jnp.dot(att_ref[...], wpa_ref[...], preferred_element_type=F32)
    pb = jnp.dot(rwo_ref[...], wpb_ref[...], preferred_element_type=F32)
    mix = gate_ref[:, :d] * pa + gate_ref[:, d:] * pb
    return x_ref[...] + jnp.dot(mix.astype(BF16), wo_ref[...], preferred_element_type=F32)


def _rms(x, g):
    ms = jnp.mean(x * x, axis=-1, keepdims=True)
    return x * lax.rsqrt(ms + NORM_EPS) * g


def _gelu_tanh(c):
    return c * (0.5 * (1.0 + jnp.tanh(math.sqrt(2.0 / math.pi) * (c + 0.044715 * (c * c * c)))))


def _conv_ffn_cols(h, x, wup_ref, cw_ref, cb_ref, wdown_ref, shifted, ug_sink):
    d_ff = cb_ref.shape[1]
    tiles = -(-d_ff // MXU_WIDTH)
    edges = [min(d_ff, MXU_WIDTH * ((tiles * j + FFN_COL_CHUNKS - 1) // FFN_COL_CHUNKS))
             for j in range(FFN_COL_CHUNKS + 1)]
    acc = x
    for lo_c, hi_c in zip(edges[:-1], edges[1:]):
        cs = slice(lo_c, hi_c)
        ug_full = jnp.dot(h, wup_ref[:, lo_c:hi_c], preferred_element_type=F32)
        uv_full = jnp.dot(h, wup_ref[:, d_ff + lo_c:d_ff + hi_c], preferred_element_type=F32)
        ug, ug_m1, ug_m2, uv = shifted(ug_full, uv_full, cs)
        c = cb_ref[:, cs] + cw_ref[0:1, cs] * ug_m2 + cw_ref[1:2, cs] * ug_m1 + cw_ref[2:3, cs] * ug
        act = (_gelu_tanh(c) * uv).astype(BF16)
        acc = acc + jnp.dot(act, wdown_ref[lo_c:hi_c, :], preferred_element_type=F32)
        ug_sink(ug, cs)
    return acc


def _ffn_prompt_kernel(x_ref, att_ref, rwo_ref, gate_ref, wpa_ref, wpb_ref, wo_ref, g2_ref, wup_ref, cw_ref,
                       cb_ref, wdown_ref, gf_ref, y_ref, ug_ref, carry_ref):
    tm = x_ref.shape[0]
    x = _branch_mix(x_ref, att_ref, rwo_ref, gate_ref, wpa_ref, wpb_ref, wo_ref)
    h = _rms(x, g2_ref[...]).astype(BF16)
    seq_start = pl.program_id(1) == 0

    def shifted(ug, uv, cs):
        before = jnp.where(seq_start, 0.0, carry_ref[:, cs])
        carry_ref[:, cs] = ug[tm - SUBLANES:]
        ug_e = jnp.concatenate([before, ug], axis=0)
        return ug, pltpu.roll(ug_e, 1, 0)[SUBLANES:], pltpu.roll(ug_e, 2, 0)[SUBLANES:], uv

    def ug_sink(ug, cs):
        ug_ref[:, cs] = ug[tm - SUBLANES:]

    x2 = _conv_ffn_cols(h, x, wup_ref, cw_ref, cb_ref, wdown_ref, shifted, ug_sink)
    y_ref[...] = _rms(x2, gf_ref[...])


def _ffn_sample_kernel(x_ref, att_ref, rwo_ref, gate_ref, e_ref, wpa_ref, wpb_ref, wo_ref, g2_ref, wup_ref,
                       cw_ref, cb_ref, wdown_ref, gf_ref, y_ref, ug_ref, *, seq):
    rows = x_ref.shape[0]
    x = _branch_mix(x_ref, att_ref, rwo_ref, gate_ref, wpa_ref, wpb_ref, wo_ref)
    h = _rms(x, g2_ref[...]).astype(BF16)
    t = lax.broadcasted_iota(jnp.int32, (rows, 1), 0) % seq

    def shifted(ug, uv, cs):
        e = e_ref[:, cs]
        ug_m1 = jnp.where(t == 0, pltpu.roll(e, rows - 1, 0), pltpu.roll(ug, 1, 0))
        ug_m2 = jnp.where(t < 2, e, pltpu.roll(ug, 2, 0))
        return ug, ug_m1, ug_m2, uv

    def ug_sink(ug, cs):
        ug_ref[:, cs] = ug

    x2 = _conv_ffn_cols(h, x, wup_ref, cw_ref, cb_ref, wdown_ref, shifted, ug_sink)
    y_ref[...] = _rms(x2, gf_ref[...])


def _ffn_prompt(x2d, att, rwo, gates, batch, seq, lw, final_g, tm):
    n, d = x2d.shape
    d_ff = lw["conv_b"].shape[0]
    nt = seq // tm
    rows = lambda a: pl.BlockSpec((tm, a.shape[1]), lambda b, i: (b * nt + i, 0))
    full = lambda a: _resident(a.shape)
    consts = [lw["w_pa"], lw["w_pb"], lw["w_o"], lw["norm2_g"].reshape(1, d), lw["w_up"], lw["conv_w"],
              lw["conv_b"].reshape(1, d_ff), lw["w_down"], final_g.reshape(1, d)]
    acts = [x2d, att, rwo, gates]
    y, ug_last = pl.pallas_call(
        _ffn_prompt_kernel,
        grid=(batch, nt),
        in_specs=[rows(a) for a in acts] + [full(c) for c in consts],
        out_specs=[
            pl.BlockSpec((tm, d), lambda b, i: (b * nt + i, 0)),
            pl.BlockSpec((SUBLANES, d_ff), lambda b, i: (b, 0)),
        ],
        out_shape=[
            jax.ShapeDtypeStruct((n, d), F32),
            jax.ShapeDtypeStruct((batch * SUBLANES, d_ff), F32),
        ],
        scratch_shapes=[pltpu.VMEM((SUBLANES, d_ff), F32)],
        compiler_params=_cparams(("arbitrary", "arbitrary")),
        name="mix_conv_ffn_prompt",
    )(*acts, *consts)
    conv_new = ug_last.reshape(batch, SUBLANES, d_ff)[:, SUBLANES - (CONV_W - 1):]
    return y, conv_new


def _ffn_sample(x2d, att, rwo, gates, batch, seq, conv0, lw, final_g, bt):
    n, d = x2d.shape
    d_ff = lw["conv_b"].shape[0]
    nrows = bt * seq
    e = jnp.pad(conv0, ((0, 0), (0, seq - (CONV_W - 1)), (0, 0))).reshape(n, d_ff)
    rows = lambda a: pl.BlockSpec((nrows, a.shape[1]), lambda i: (i, 0))
    full = lambda a: _resident(a.shape)
    consts = [lw["w_pa"], lw["w_pb"], lw["w_o"], lw["norm2_g"].reshape(1, d), lw["w_up"], lw["conv_w"],
              lw["conv_b"].reshape(1, d_ff), lw["w_down"], final_g.reshape(1, d)]
    acts = [x2d, att, rwo, gates, e]
    y, ug = pl.pallas_call(
        functools.partial(_ffn_sample_kernel, seq=seq),
        grid=(n // nrows,),
        in_specs=[rows(a) for a in acts] + [full(c) for c in consts],
        out_specs=[
            pl.BlockSpec((nrows, d), lambda i: (i, 0)),
            pl.BlockSpec((nrows, d_ff), lambda i: (i, 0)),
        ],
        out_shape=[jax.ShapeDtypeStruct((n, d), F32), jax.ShapeDtypeStruct((n, d_ff), F32)],
        compiler_params=_cparams(("parallel",)),
        name="mix_conv_ffn_sample",
    )(*acts, *consts)
    conv_new = ug.reshape(batch, seq, d_ff)[:, seq - (CONV_W - 1):]
    return y, conv_new


def _row_tile(n, want):
    t = min(want, n)
    while n % t:
        t //= 2
    return t


def _layer(x, is_prompt, state, rel_bias, lw, final_g):
    batch, seq, d = x.shape
    n = batch * seq
    x2d = x.reshape(n, d)
    n_qkv = Q_W + 2 * KV_W
    n_rw = lw["mu_shift"].shape[0]
    tm = _row_tile(n, 512)
    qkv, rw, gates = _in_proj(x2d, lw["norm1_g"], lw["w_in"], n_qkv, n_rw, tm)
    kv = qkv.reshape(batch, seq, n_qkv)
    k_new = kv[:, :, Q_W:Q_W + KV_W]
    v_new = kv[:, :, Q_W + KV_W:]
    if is_prompt:
        att = _attention_prompt(qkv, batch, seq, rel_bias, lw["sinks"])
        wb = state["win_buf"]
        k_rows, v_rows = k_new[:, seq - wb:], v_new[:, seq - wb:]
        shift0 = wkv0 = None
    else:
        cache_k, cache_v = state["cache_k"], state["cache_v"]
        wb = cache_k.shape[1]
        assert seq <= wb
        att, k_rows, v_rows = _attention_sample(qkv, batch, seq, cache_k, cache_v, rel_bias, lw["sinks"],
                                                _row_tile(batch, 16))
        shift0, wkv0 = state["shift"], state["wkv"]
    k_rows = k_rows.reshape(batch, -1, N_KV, HEAD_DIM)
    v_rows = v_rows.reshape(batch, -1, N_KV, HEAD_DIM)
    rwo, wkv_new = _rwkv_mixer(rw, batch, seq, shift0, wkv0, lw["mu_shift"], lw["w0"], lw["w2"], lw["a0"],
                               lw["a2"], lw["g2"], lw["k_k"], lw["k_a"], lw["r_k"], lw["lnx_g"], lw["lnx_b"])
    shift_new = rw.reshape(batch, seq, n_rw)[:, seq - 1]
    if is_prompt:
        y, conv_new = _ffn_prompt(x2d, att, rwo, gates, batch, seq, lw, final_g, _row_tile(seq, FFN_ROWS))
    else:
        y, conv_new = _ffn_sample(x2d, att, rwo, gates, batch, seq, state["conv"], lw, final_g,
                                  _row_tile(batch, FFN_ROWS_WITH_STATE // seq))
    return y.reshape(batch, seq, d), (k_rows, v_rows, shift_new, wkv_new, conv_new)


def kernel(x_prompt, x_sample, cache_win_k, cache_win_v, state_shift, state_wkv, state_conv, rel_bias, norm1_g,
           w_in, sinks, mu_shift, w0, w2, a0, a2, g2, k_k, k_a, r_k, lnx_g, lnx_b, w_pa, w_pb, w_o, norm2_g,
           w_up, conv_w, conv_b, w_down, final_g):
    depth = w_in.shape[0]
    assert depth == 1, "the final norm is fused into the layer's last kernel"
    l = 0
    lw = dict(norm1_g=norm1_g[l], w_in=w_in[l].astype(BF16), sinks=sinks[l], mu_shift=mu_shift[l], w0=w0[l],
              w2=w2[l], a0=a0[l], a2=a2[l], g2=g2[l], k_k=k_k[l], k_a=k_a[l], r_k=r_k[l].reshape(-1),
              lnx_g=lnx_g[l], lnx_b=lnx_b[l], w_pa=w_pa[l].astype(BF16), w_pb=w_pb[l].astype(BF16),
              w_o=w_o[l].astype(BF16), norm2_g=norm2_g[l], w_up=w_up[l].astype(BF16), conv_w=conv_w[l],
              conv_b=conv_b[l], w_down=w_down[l].astype(BF16))
    win_buf = cache_win_k.shape[2]
    y_p, st_p = _layer(x_prompt, True, dict(win_buf=win_buf), rel_bias, lw, final_g)
    y_s, st_s = _layer(x_sample, False,
                       dict(cache_k=cache_win_k[l], cache_v=cache_win_v[l], shift=state_shift[l],
                            wkv=state_wkv[l], conv=state_conv[l]), rel_bias, lw, final_g)
    stack = lambda t: t[None]
    return (y_p, y_s) + tuple(stack(t) for t in st_p) + tuple(stack(t) for t in st_s)
```

```python
import functools
import math

import numpy as np
import jax
import jax.numpy as jnp
from jax import lax
from jax.experimental import pallas as pl
from jax.experimental.pallas import tpu as pltpu

F32 = jnp.float32
BF16 = jnp.bfloat16

HEAD_DIM = 64
N_HEADS = 8
N_KV = 2
WINDOW = 128
N_BUCKETS = 32
MAX_EXACT = N_BUCKETS // 2
REL_MAX_DIST = 128
RW_N = 64
RW_HEADS = 8
RW = RW_HEADS * RW_N
NORM_EPS = 1e-6
GN_EPS = 64e-5
NEG = -1e30
CONV_W = 3

Q_W = N_HEADS * HEAD_DIM
KV_W = N_KV * HEAD_DIM
LANES = 128
SUBLANES = 8
MXU_WIDTH = 256
CHUNK = 64
VMEM_LIMIT = 56 * 1024 * 1024
FFN_COL_CHUNKS = 3
FFN_ROWS = 512
FFN_ROWS_WITH_STATE = 256
ATTN_BLOCKS = 2
WKV_UNITS = 8
WKV_UNITS_WITH_STATE = 4


def _resident(shape):
    return pl.BlockSpec(shape, lambda *_: (0,) * len(shape), pipeline_mode=pl.Buffered(1))


def _cparams(sem):
    return pltpu.CompilerParams(dimension_semantics=sem, vmem_limit_bytes=VMEM_LIMIT)


def _sigmoid(x):
    return 1.0 / (1.0 + jnp.exp(-x))


def _dg(a, b, kind):
    if kind == "nn":
        dn = (((1,), (0,)), ((), ()))
    else:
        dn = (((1,), (1,)), ((), ()))
    return lax.dot_general(a, b, dn, preferred_element_type=F32)


def _split(x):
    hi = x.astype(BF16)
    lo = (x - hi.astype(F32)).astype(BF16)
    return hi, lo


def _mm(a, b, kind="nn", passes=1):
    if passes == 1:
        return _dg(a.astype(BF16), b.astype(BF16), kind)
    ah, al = _split(a)
    bh, bl = _split(b)
    return _dg(ah, bh, kind) + (_dg(ah, bl, kind) + _dg(al, bh, kind))


def _mm_exact_lhs(a_bf16, b, n_terms):
    out = None
    rem = b
    for _ in range(n_terms):
        piece = rem.astype(BF16)
        term = _dg(a_bf16, piece, "nn")
        out = term if out is None else out + term
        rem = rem - piece.astype(F32)
    return out


def _mm_exact_rhs(a, b_bf16, n_terms):
    out = None
    rem = a
    for _ in range(n_terms):
        piece = rem.astype(BF16)
        term = _dg(piece, b_bf16, "nn")
        out = term if out is None else out + term
        rem = rem - piece.astype(F32)
    return out


def _inproj_kernel(x_ref, g_ref, w_ref, qkv_ref, rw_ref, gate_ref, *, n_qkv, n_rw):
    x = x_ref[...]
    ms = jnp.mean(x * x, axis=-1, keepdims=True)
    h = (x * lax.rsqrt(ms + NORM_EPS) * g_ref[...]).astype(BF16)
    n_gate = gate_ref.shape[1]
    step = 2 * MXU_WIDTH
    plain = [(qkv_ref, c, c, min(step, n_qkv - c)) for c in range(0, n_qkv, step)]
    plain += [(rw_ref, c, n_qkv + c, min(step, n_rw - c)) for c in range(0, n_rw, step)]
    gated = [(gate_ref, c, n_qkv + n_rw + c, min(step, n_gate - c)) for c in range(0, n_gate, step)]
    pieces = []
    while plain or gated:
        pieces += [gated.pop(0)] if gated else []
        pieces += [plain.pop(0)] if plain else []
    dot = lambda p: jnp.dot(h, w_ref[:, p[2]:p[2] + p[3]], preferred_element_type=F32)
    pending = dot(pieces[0])
    for j, (ref, c0, _, width) in enumerate(pieces):
        out = pending
        if j + 1 < len(pieces):
            pending = dot(pieces[j + 1])
        if ref is gate_ref:
            out = _sigmoid(out)
        ref[:, c0:c0 + width] = out.astype(ref.dtype)


def _in_proj(x2d, g, w_bf16, n_qkv, n_rw, tm):
    n, d = x2d.shape
    n_gate = w_bf16.shape[1] - n_qkv - n_rw
    return pl.pallas_call(
        functools.partial(_inproj_kernel, n_qkv=n_qkv, n_rw=n_rw),
        grid=(n // tm,),
        in_specs=[
            pl.BlockSpec((tm, d), lambda i: (i, 0)),
            _resident((1, d)),
            _resident(w_bf16.shape),
        ],
        out_specs=[
            pl.BlockSpec((tm, n_qkv), lambda i: (i, 0)),
            pl.BlockSpec((tm, n_rw), lambda i: (i, 0)),
            pl.BlockSpec((tm, n_gate), lambda i: (i, 0)),
        ],
        out_shape=[
            jax.ShapeDtypeStruct((n, n_qkv), F32),
            jax.ShapeDtypeStruct((n, n_rw), F32),
            jax.ShapeDtypeStruct((n, n_gate), BF16),
        ],
        compiler_params=_cparams(("parallel",)),
        name="in_proj",
    )(x2d, g.reshape(1, d), w_bf16)


def _t5_bucket_np(dist):
    n = np.maximum(dist, 0)
    nf = np.maximum(n, 1).astype(np.float32)
    large = MAX_EXACT + (np.log(nf / MAX_EXACT) / math.log(REL_MAX_DIST / MAX_EXACT)
                         * (N_BUCKETS - MAX_EXACT)).astype(np.int32)
    return np.where(n < MAX_EXACT, n, np.minimum(large, N_BUCKETS - 1)).astype(np.int32)


def _attn_kernel(q_ref, k1_ref, k2_ref, v1_ref, v2_ref, bucket_ref, relb_ref, sink_ref, o_ref, *rest,
                 nq, nk, n_blocks, first_block_axis, emit_window):
    bt = q_ref.shape[0]
    bias_ref = rest[-1]
    if emit_window:
        n_old, n_new = k1_ref.shape[1], k2_ref.shape[1]
        for w_ref, old_ref, new_ref in ((rest[0], k1_ref, k2_ref), (rest[1], v1_ref, v2_ref)):
            w_ref[:, :n_old - n_new] = old_ref[:, n_new:]
            w_ref[:, n_old - n_new:] = new_ref[...]
    first_step = pl.program_id(0) == 0
    if first_block_axis is not None:
        first_step = jnp.logical_and(first_step, pl.program_id(1) == 0)

    @pl.when(first_step)
    def _():
        bucket = bucket_ref[...]
        prev_key = lax.broadcasted_iota(jnp.int32, (1, nk), 1) < (nk // 2)
        for n in range(N_HEADS):
            acc = jnp.full((nq, nk), NEG, F32)
            for b in range(N_BUCKETS):
                acc = jnp.where(bucket == b, relb_ref[b, n], acc)
            c, half = divmod(n, 2)
            bias_ref[0, c, :, half * nk:(half + 1) * nk] = acc
            if first_block_axis is not None:
                bias_ref[1, c, :, half * nk:(half + 1) * nk] = jnp.where(prev_key, NEG, acc)

    n_keys = nk + (n_blocks - 1) * nq

    def padded(a_ref, b_ref):
        parts = [a_ref[...], b_ref[...]]
        n_now = a_ref.shape[1] + b_ref.shape[1]
        if n_now < n_keys:
            parts.append(jnp.zeros((bt, n_keys - n_now, LANES), F32))
        return jnp.concatenate(parts, axis=1)

    kk = padded(k1_ref, k2_ref)
    vv = padded(v1_ref, v2_ref)
    kk_r = pltpu.roll(kk, HEAD_DIM, 2)
    vv_r = pltpu.roll(vv, HEAD_DIM, 2)
    lane = lax.broadcasted_iota(jnp.int32, (1, 1, LANES), 2)
    lo = lane < HEAD_DIM

    def halves(x, x_r, kvh):
        src_lo, src_hi = (x, x_r) if kvh == 0 else (x_r, x)
        even = jnp.where(lo, src_lo, 0.0).astype(BF16)
        odd = jnp.where(lo, 0.0, src_hi).astype(BF16)
        return even, odd

    k_eo = [halves(kk, kk_r, h) for h in range(N_KV)]
    v_eo = [halves(vv, vv_r, h) for h in range(N_KV)]

    def window(eo, j):
        return jnp.concatenate([eo[0][:, j * nq:j * nq + nk], eo[1][:, j * nq:j * nq + nk]], axis=1)

    n_cols = N_HEADS // 2
    units = [(j, c) for j in range(n_blocks) for c in range(n_cols)]
    kvh_of = lambda c: (2 * c) // (N_HEADS // N_KV)
    k_win = {(j, h): window(k_eo[h], j) for j in range(n_blocks) for h in range(N_KV)}
    v_win = {(j, h): window(v_eo[h], j) for j in range(n_blocks) for h in range(N_KV)}

    scale = HEAD_DIM ** -0.5
    first_table = 0
    if first_block_axis is not None:
        first_table = jnp.where(pl.program_id(first_block_axis) == 0, 1, 0)
    s_all = {}
    for j, c in units:
        qc = (q_ref[:, j * nq:(j + 1) * nq, c * LANES:(c + 1) * LANES] * scale).astype(BF16)
        s_all[j, c] = jnp.einsum("bqd,bkd->bqk", qc, k_win[j, kvh_of(c)], preferred_element_type=F32)
    e_all, inv_all = {}, {}
    for j, c in units:
        s = s_all[j, c] + bias_ref[first_table if j == 0 else 0, c][None]
        es, invs = [], []
        for half in range(2):
            sh = s[:, :, half * nk:(half + 1) * nk]
            sink = sink_ref[2 * c + half]
            m = jnp.maximum(jnp.max(sh, axis=-1, keepdims=True), sink)
            e = jnp.exp(sh - m)
            invs.append(1.0 / (jnp.sum(e, axis=-1, keepdims=True) + jnp.exp(sink - m)))
            es.append(e.astype(BF16))
        e_all[j, c] = jnp.concatenate(es, axis=2)
        inv_all[j, c] = jnp.where(lo, invs[0], invs[1])
    for j, c in units:
        o = jnp.einsum("bqk,bkd->bqd", e_all[j, c], v_win[j, kvh_of(c)], preferred_element_type=F32)
        o_ref[:, j * nq:(j + 1) * nq, c * LANES:(c + 1) * LANES] = (o * inv_all[j, c]).astype(o_ref.dtype)


def _attention_prompt(qkv, batch, seq, rel_bias, sinks):
    nblk = seq // WINDOW
    nb = ATTN_BLOCKS if nblk % ATTN_BLOCKS == 0 else 1
    nsteps = nblk // nb
    width = qkv.shape[1]
    q_blk = qkv.reshape(batch * nblk, WINDOW, width)
    q_step = qkv.reshape(batch * nsteps, nb * WINDOW, width)
    kcol = Q_W // LANES
    vcol = (Q_W + KV_W) // LANES
    nk = 2 * WINDOW
    qi = np.arange(WINDOW)[:, None] + WINDOW
    kj = np.arange(nk)[None, :]
    dist = qi - kj
    bucket = np.where((dist >= 0) & (dist < WINDOW), _t5_bucket_np(dist), -1).astype(np.int32)
    cur = lambda c: (lambda b, i: (b * nsteps + i, 0, c))
    prev = lambda c: (lambda b, i: (b * nblk + jnp.maximum(i * nb - 1, 0), 0, c))
    out = pl.pallas_call(
        functools.partial(_attn_kernel, nq=WINDOW, nk=nk, n_blocks=nb, first_block_axis=1, emit_window=False),
        grid=(batch, nsteps),
        in_specs=[
            pl.BlockSpec((1, nb * WINDOW, Q_W), cur(0)),
            pl.BlockSpec((1, WINDOW, LANES), prev(kcol)),
            pl.BlockSpec((1, nb * WINDOW, LANES), cur(kcol)),
            pl.BlockSpec((1, WINDOW, LANES), prev(vcol)),
            pl.BlockSpec((1, nb * WINDOW, LANES), cur(vcol)),
            _resident(bucket.shape),
            pl.BlockSpec(memory_space=pltpu.SMEM),
            pl.BlockSpec(memory_space=pltpu.SMEM),
        ],
        out_specs=pl.BlockSpec((1, nb * WINDOW, Q_W), cur(0)),
        out_shape=jax.ShapeDtypeStruct((batch * nsteps, nb * WINDOW, Q_W), BF16),
        scratch_shapes=[pltpu.VMEM((2, N_HEADS // 2, WINDOW, 2 * nk), F32)],
        compiler_params=_cparams(("arbitrary", "arbitrary")),
        name="attn_prompt",
    )(q_step, q_blk, q_step, q_blk, q_step, jnp.asarray(bucket), rel_bias, sinks)
    return out.reshape(batch * seq, Q_W)


def _attention_sample(qkv, batch, seq, cache_k, cache_v, rel_bias, sinks, bt):
    wb = cache_k.shape[1]
    nk = 2 * WINDOW
    q3 = qkv.reshape(batch, seq, qkv.shape[1])
    ck = cache_k.reshape(batch, wb, KV_W)
    cv = cache_v.reshape(batch, wb, KV_W)
    kcol = Q_W // LANES
    vcol = (Q_W + KV_W) // LANES
    tq = np.arange(seq)[:, None]
    j = np.arange(nk)[None, :]
    dist = np.where(j < wb, tq + wb - j, tq - (j - wb))
    ok = (dist >= 0) & (dist < WINDOW) & (j < wb + seq)
    bucket = np.where(ok, _t5_bucket_np(dist), -1).astype(np.int32)
    win_spec = pl.BlockSpec((bt, wb, LANES), lambda b: (b, 0, 0))
    win_shape = jax.ShapeDtypeStruct((batch, wb, KV_W), F32)
    out, win_k, win_v = pl.pallas_call(
        functools.partial(_attn_kernel, nq=seq, nk=nk, n_blocks=1, first_block_axis=None, emit_window=True),
        grid=(batch // bt,),
        in_specs=[
            pl.BlockSpec((bt, seq, Q_W), lambda b: (b, 0, 0)),
            pl.BlockSpec((bt, wb, LANES), lambda b: (b, 0, 0)),
            pl.BlockSpec((bt, seq, LANES), lambda b: (b, 0, kcol)),
            pl.BlockSpec((bt, wb, LANES), lambda b: (b, 0, 0)),
            pl.BlockSpec((bt, seq, LANES), lambda b: (b, 0, vcol)),
            _resident(bucket.shape),
            pl.BlockSpec(memory_space=pltpu.SMEM),
            pl.BlockSpec(memory_space=pltpu.SMEM),
        ],
        out_specs=[pl.BlockSpec((bt, seq, Q_W), lambda b: (b, 0, 0)), win_spec, win_spec],
        out_shape=[jax.ShapeDtypeStruct((batch, seq, Q_W), BF16), win_shape, win_shape],
        scratch_shapes=[pltpu.VMEM((1, N_HEADS // 2, seq, 2 * nk), F32)],
        compiler_params=_cparams(("arbitrary",)),
        name="attn_sample",
    )(q3, ck, q3, cv, q3, jnp.asarray(bucket), rel_bias, sinks)
    return out.reshape(batch * seq, Q_W), win_k, win_v


P_SCORE = 1
P_TINV = 1
P_INTRA = 1
P_STATE = 1


def _wkv_kernel(*refs, n_units, n_seg, has_state, lora_w):
    C = CHUNK
    seg_len = C // n_seg
    n_pairs = RW // LANES
    it = iter(refs)
    p_ref = next(it)
    p0_ref, s0_ref = (next(it), next(it)) if has_state else (None, None)
    (mu_ref, w2a2_ref, w0_ref, a0_ref, g2_ref, kk_ref, ka_ref, rk_ref, lng_ref, lnb_ref, ltri_ref) = (
        next(it) for _ in range(11))
    elast_ref = next(it) if n_seg > 1 else None
    ones_ref, out_ref, sout_ref = next(it), next(it), next(it)
    last_ref, sbd_ref = (None, None) if has_state else (next(it), next(it))

    if not has_state:
        @pl.when(pl.program_id(1) == 0)
        def _():
            last_ref[...] = jnp.zeros_like(last_ref)
            sbd_ref[...] = jnp.zeros_like(sbd_ref)

    row = lax.broadcasted_iota(jnp.int32, (C, 1), 0)
    rows_of = lambda u: slice(u * C, (u + 1) * C)
    lane = lax.broadcasted_iota(jnp.int32, (1, LANES), 1)
    lo = lane < RW_N
    ones_bd = ones_ref[...]

    def headsum(x):
        xb = x.astype(BF16)
        return jnp.concatenate([_dg(xb[:, c:c + MXU_WIDTH], ones_bd, "nn") for c in range(0, RW, MXU_WIDTH)],
                               axis=1)

    def prepare(us):
        xs_parts = []
        for u in us:
            p = p_ref[u, 0]
            rolled = pltpu.roll(p, 1, 0)
            if has_state:
                prev = jnp.where(row % seg_len == 0, p0_ref[u, 0], rolled)
            else:
                prev = jnp.where(row == 0, last_ref[u], rolled)
                last_ref[u] = p_ref[u, 0, C - 1:C, :]
            xs_parts.append(p + mu_ref[...] * (prev - p))
        xs = jnp.concatenate(xs_parts, axis=0)
        r = xs[:, 0:RW]
        k = xs[:, RW:2 * RW]
        v = xs[:, 2 * RW:3 * RW]
        lwla = xs[:, 3 * RW:3 * RW + LANES]
        lg = xs[:, 3 * RW + LANES:3 * RW + 2 * LANES]
        lwla = jnp.where(lane < lora_w, jnp.tanh(lwla), lwla)
        wa = jnp.dot(lwla.astype(BF16), w2a2_ref[...], preferred_element_type=F32)
        logw = -math.exp(-0.5) * _sigmoid(w0_ref[...] + wa[:, :RW])
        a_sig = _sigmoid(a0_ref[...] + wa[:, RW:])
        g = jnp.dot(_sigmoid(lg).astype(BF16), g2_ref[...], preferred_element_type=F32)
        kk = k * kk_ref[...]
        kk = kk * (1.0 / jnp.maximum(jnp.sqrt(headsum(kk * kk)), 1e-12))
        k = k * (1.0 + (a_sig - 1.0) * ka_ref[...])
        a = -kk
        b = kk * a_sig
        bonus = headsum(r * k * rk_ref[...]) * v
        cws, cwl = [], []
        for i in range(len(us)):
            cw_u = _mm_exact_lhs(ltri_ref[...], logw[rows_of(i)], 3)
            cws.append(cw_u)
            if n_seg == 1:
                cwl.append(jnp.broadcast_to(cw_u[C - 1:C, :], (C, RW)))
            else:
                cwl.append(_mm_exact_lhs(elast_ref[...], cw_u, 3))
        cw = jnp.concatenate(cws, axis=0)
        cw_last = jnp.concatenate(cwl, axis=0)
        w_inv = jnp.exp(-cw)
        b_t, k_t = b * w_inv, k * w_inv
        w_tail = jnp.exp(cw_last - cw)
        b_h, k_h = b * w_tail, k * w_tail
        a_t, r_t = a * jnp.exp(cw - logw), r * jnp.exp(cw)
        w_c = jnp.exp(cw_last)
        return a_t, r_t, b_t, k_t, b_h, k_h, v, w_c, bonus, g

    def bd(y):
        return jnp.concatenate([jnp.where(lo, y, 0.0), jnp.where(lo, 0.0, y)], axis=0)

    zeros_head = jnp.zeros((RW_N, RW_N), F32)

    def pack_pair(s_even, s_odd):
        return jnp.concatenate([jnp.concatenate([s_even, zeros_head], axis=1),
                                jnp.concatenate([zeros_head, s_odd], axis=1)], axis=0)

    def store_pair(ref, i, q, s_pair):
        ref[i, 2 * q] = s_pair[:RW_N, :RW_N]
        ref[i, 2 * q + 1] = s_pair[RW_N:, RW_N:]

    s_idx = lane % C
    strict = s_idx < row
    incl = s_idx <= row
    if n_seg > 1:
        same_seg = (s_idx // seg_len) == (row // seg_len)
        strict = jnp.logical_and(strict, same_seg)
        incl = jnp.logical_and(incl, same_seg)
    row2 = lax.broadcasted_iota(jnp.int32, (2 * C, 1), 0)
    same_head = (row2 < RW_N) == lo

    part = lambda x, i, q: x[i * C:(i + 1) * C, q * LANES:(q + 1) * LANES]

    def recurrence(us, ops):
        a_t, r_t, b_t, k_t, b_h, k_h, v, w_c = ops[:8]
        units = [(i, q) for i in range(len(us)) for q in range(n_pairs)]
        sc = {uq: _mm(jnp.concatenate([part(a_t, *uq), part(r_t, *uq)], axis=0),
                      jnp.concatenate([bd(part(b_t, *uq)), bd(part(k_t, *uq))], axis=0),
                      "nt", P_SCORE) for uq in units}
        pw = {uq: jnp.where(strict, sc[uq][:C, :LANES], 0.0) for uq in units}
        m_rb = {uq: jnp.where(incl, sc[uq][C:, :LANES], 0.0) for uq in units}
        lm_v = {uq: _mm(jnp.concatenate([jnp.where(strict, sc[uq][:C, LANES:], 0.0),
                                         jnp.where(incl, sc[uq][C:, LANES:], 0.0)], axis=0),
                        bd(part(v, *uq)), "nn", P_INTRA) for uq in units}
        tinv = {uq: pw[uq] + jnp.where(s_idx == row, 1.0, 0.0) for uq in units}
        n_lvl = int(math.log2(seg_len))
        for lvl in range(1, n_lvl):
            last = lvl + 1 == n_lvl
            if lvl == 1:
                for uq in units:
                    pw[uq] = _mm(pw[uq], bd(pw[uq]), "nn", P_TINV)
            for uq in units:
                rhs = bd(tinv[uq]) if last else jnp.concatenate([bd(tinv[uq]), bd(pw[uq])], axis=1)
                upd = _mm(pw[uq], rhs, "nn", P_TINV)
                tinv[uq] = tinv[uq] + upd[:, :LANES]
                if not last:
                    pw[uq] = upd[:, LANES:]
        x = {uq: _mm(tinv[uq], jnp.concatenate([bd(part(a_t, *uq)), bd(lm_v[uq][:C])], axis=1), "nn", P_TINV)
             for uq in units}
        a_hat = {uq: x[uq][:, :LANES] for uq in units}
        v_hat = {uq: x[uq][:, LANES:] for uq in units}
        z = {uq: _mm(m_rb[uq], jnp.concatenate([bd(a_hat[uq]), bd(v_hat[uq])], axis=1), "nn", P_INTRA)
             for uq in units}
        r_hat = {uq: part(r_t, *uq) + z[uq][:, :LANES] for uq in units}
        y_intra = {uq: z[uq][:, LANES:] + lm_v[uq][C:] for uq in units}
        ys = {}
        if n_seg == 1:
            s_old = {(i, q): sbd_ref[us[i], q] for i, q in units}
            t1 = {uq: _mm(jnp.concatenate([a_hat[uq], r_hat[uq]], axis=0), s_old[uq], "nt", P_STATE)
                  for uq in units}
            for uq in units:
                ys[uq] = t1[uq][C:] + y_intra[uq]
                uv = jnp.concatenate([t1[uq][:C] + v_hat[uq], part(v, *uq)], axis=0)
                bkh = jnp.concatenate([part(b_h, *uq), part(k_h, *uq)], axis=0)
                ds = _mm(uv.T, bkh, "nn", P_STATE)
                sbd_ref[us[uq[0]], uq[1]] = s_old[uq] * part(w_c, *uq)[0:1] + jnp.where(same_head, ds, 0.0)
        else:
            row_seg = (row2 % C) // seg_len
            for uq in units:
                i, q = uq
                first_seq = us[i] * n_seg
                u_parts, y_parts, s_olds = [], [], []
                for sg in range(n_seg):
                    rs = slice(sg * seg_len, (sg + 1) * seg_len)
                    s_sg = pack_pair(s0_ref[first_seq + sg, 2 * q], s0_ref[first_seq + sg, 2 * q + 1])
                    t1 = _mm(jnp.concatenate([a_hat[uq][rs], r_hat[uq][rs]], axis=0), s_sg, "nt", P_STATE)
                    u_parts.append(t1[:seg_len] + v_hat[uq][rs])
                    y_parts.append(t1[seg_len:] + y_intra[uq][rs])
                    s_olds.append(s_sg)
                ys[uq] = jnp.concatenate(y_parts, axis=0)
                uv_t = jnp.concatenate(u_parts + [part(v, *uq)], axis=0).T
                bkh = jnp.concatenate([part(b_h, *uq), part(k_h, *uq)], axis=0)
                w_cq = part(w_c, *uq)
                for sg in range(n_seg):
                    ds = _mm(uv_t, jnp.where(row_seg == sg, bkh, 0.0), "nn", P_STATE)
                    store_pair(sout_ref, first_seq + sg, q,
                               s_olds[sg] * w_cq[sg * seg_len:sg * seg_len + 1] + jnp.where(same_head, ds, 0.0))
        return ys

    def finish(us, ys, ops):
        bonus, g = ops[8:]
        y = jnp.concatenate([jnp.concatenate([ys[i, q] for q in range(n_pairs)], axis=1)
                             for i in range(len(us))], axis=0)
        mean = headsum(y) * (1.0 / RW_N)
        d = y - mean
        var = headsum(d * d) * (1.0 / RW_N)
        y = d * lax.rsqrt(var + GN_EPS) * lng_ref[...] + lnb_ref[...]
        y = ((y + bonus) * g).astype(out_ref.dtype)
        for i, u in enumerate(us):
            out_ref[u, 0] = y[rows_of(i)]

    all_units = list(range(n_units))
    ops = prepare(all_units)
    finish(all_units, recurrence(all_units, ops), ops)
    units = [(u, q) for u in range(n_units) for q in range(n_pairs)]

    if not has_state:
        @pl.when(pl.program_id(1) == pl.num_programs(1) - 1)
        def _():
            for u, q in units:
                store_pair(sout_ref, u, q, sbd_ref[u, q])


def _rwkv_mixer(rw, batch, seq, shift0, wkv0, mu, w0, w2, a0, a2, g2, k_k, k_a, r_k, lnx_g, lnx_b):
    n_shift = rw.shape[1]
    lora_w, lora_a = w2.shape[0], a2.shape[0]
    assert lora_w + lora_a == LANES and g2.shape[0] == LANES and n_shift == 3 * RW + 2 * LANES
    has_state = shift0 is not None
    C = CHUNK
    n_pairs = RW // LANES
    if has_state:
        assert C % seq == 0 and batch % (C // seq) == 0
        n_seg, n_chunks, n_groups = C // seq, 1, batch * seq // C
    else:
        assert seq % C == 0
        n_seg, n_chunks, n_groups = 1, seq // C, batch
    want = WKV_UNITS_WITH_STATE if has_state else WKV_UNITS
    nu = want if n_groups % want == 0 else 1
    seg_len = C // n_seg
    w2a2 = jnp.zeros((LANES, 2 * RW), F32).at[:lora_w, :RW].set(w2).at[lora_w:, RW:].set(a2).astype(BF16)
    t = np.arange(C)
    same_seg = (t[:, None] // seg_len) == (t[None, :] // seg_len)
    ltri = jnp.asarray(((t[:, None] >= t[None, :]) & same_seg).astype(np.float32), BF16)
    elast = jnp.asarray((t[None, :] == (t[:, None] // seg_len) * seg_len + seg_len - 1).astype(np.float32), BF16)
    ones_bd = jnp.asarray(np.kron(np.eye(MXU_WIDTH // RW_N, dtype=np.float32),
                                  np.ones((RW_N, RW_N), np.float32)), BF16)
    row = lambda x: x.reshape(1, -1).astype(F32)
    rw4 = rw.reshape(n_groups, n_chunks, C, n_shift)
    blk = lambda w: pl.BlockSpec((nu, 1, C, w), lambda i, c: (i, c, 0, 0))
    st_blk = pl.BlockSpec((nu * n_seg, RW_HEADS, RW_N, RW_N), lambda i, c: (i, 0, 0, 0))
    args, specs = [rw4], [blk(n_shift)]
    if has_state:
        p0 = jnp.pad(shift0[:, None, :], ((0, 0), (0, seq - 1), (0, 0))).reshape(n_groups, 1, C, n_shift)
        args += [p0, wkv0]
        specs += [blk(n_shift), st_blk]
    consts = [row(mu), w2a2, row(w0), row(a0), g2.astype(BF16), row(k_k), row(k_a), row(r_k), row(lnx_g),
              row(lnx_b), ltri] + ([elast] if n_seg > 1 else []) + [ones_bd]
    args += consts
    specs += [_resident(c.shape) for c in consts]
    scratch = [] if has_state else [pltpu.VMEM((nu, 1, n_shift), F32),
                                    pltpu.VMEM((nu, n_pairs, LANES, LANES), F32)]
    out, s_new = pl.pallas_call(
        functools.partial(_wkv_kernel, n_units=nu, n_seg=n_seg, has_state=has_state, lora_w=lora_w),
        grid=(n_groups // nu, n_chunks),
        in_specs=specs,
        out_specs=[blk(RW), st_blk],
        out_shape=[
            jax.ShapeDtypeStruct((n_groups, n_chunks, C, RW), BF16),
            jax.ShapeDtypeStruct((batch, RW_HEADS, RW_N, RW_N), F32),
        ],
        scratch_shapes=scratch,
        compiler_params=_cparams(("arbitrary", "arbitrary")),
        name="rwkv7",
    )(*args)
    return out.reshape(batch * seq, RW), s_new


def _branch_mix(x_ref, att_ref, rwo_ref, gate_ref, wpa_ref, wpb_ref, wo_ref):
    d = x_ref.shape[1]
    pa = jnp.dot(att_ref[...], wpa_ref[...], preferred_element_type=F32)
    pb = jnp.dot(rwo_ref[...], wpb_ref[...], preferred_element_type=F32)
    mix = gate_ref[:, :d] * pa + gate_ref[:, d:] * pb
    return x_ref[...] + jnp.dot(mix.astype(BF16), wo_ref[...], preferred_element_type=F32)


def _rms(x, g):
    ms = jnp.mean(x * x, axis=-1, keepdims=True)
    return x * lax.rsqrt(ms + NORM_EPS) * g


def _gelu_tanh(c):
    return c * (0.5 * (1.0 + jnp.tanh(math.sqrt(2.0 / math.pi) * (c + 0.044715 * (c * c * c)))))


def _conv_ffn_cols(h, x, wup_ref, cw_ref, cb_ref, wdown_ref, shifted, ug_sink):
    d_ff = cb_ref.shape[1]
    tiles = -(-d_ff // MXU_WIDTH)
    edges = [min(d_ff, MXU_WIDTH * ((tiles * j + FFN_COL_CHUNKS - 1) // FFN_COL_CHUNKS))
             for j in range(FFN_COL_CHUNKS + 1)]
    spans = list(zip(edges[:-1], edges[1:]))

    def up(span):
        lo_c, hi_c = span
        return (jnp.dot(h, wup_ref[:, lo_c:hi_c], preferred_element_type=F32),
                jnp.dot(h, wup_ref[:, d_ff + lo_c:d_ff + hi_c], preferred_element_type=F32))

    acc = x
    pending = up(spans[0])
    for j, (lo_c, hi_c) in enumerate(spans):
        cs = slice(lo_c, hi_c)
        ug_full, uv_full = pending
        if j + 1 < len(spans):
            pending = up(spans[j + 1])
        ug, ug_m1, ug_m2, uv = shifted(ug_full, uv_full, cs)
        c = cb_ref[:, cs] + cw_ref[0:1, cs] * ug_m2 + cw_ref[1:2, cs] * ug_m1 + cw_ref[2:3, cs] * ug
        act = (_gelu_tanh(c) * uv).astype(BF16)
        acc = acc + jnp.dot(act, wdown_ref[lo_c:hi_c, :], preferred_element_type=F32)
        ug_sink(ug, cs)
    return acc


def _ffn_prompt_kernel(x_ref, att_ref, rwo_ref, gate_ref, wpa_ref, wpb_ref, wo_ref, g2_ref, wup_ref, cw_ref,
                       cb_ref, wdown_ref, gf_ref, y_ref, ug_ref, carry_ref):
    tm = x_ref.shape[0]
    x = _branch_mix(x_ref, att_ref, rwo_ref, gate_ref, wpa_ref, wpb_ref, wo_ref)
    h = _rms(x, g2_ref[...]).astype(BF16)
    seq_start = pl.program_id(1) == 0

    def shifted(ug, uv, cs):
        before = jnp.where(seq_start, 0.0, carry_ref[:, cs])
        carry_ref[:, cs] = ug[tm - SUBLANES:]
        ug_e = jnp.concatenate([before, ug], axis=0)
        return ug, pltpu.roll(ug_e, 1, 0)[SUBLANES:], pltpu.roll(ug_e, 2, 0)[SUBLANES:], uv

    def ug_sink(ug, cs):
        ug_ref[:, cs] = ug[tm - SUBLANES:]

    x2 = _conv_ffn_cols(h, x, wup_ref, cw_ref, cb_ref, wdown_ref, shifted, ug_sink)
    y_ref[...] = _rms(x2, gf_ref[...])


def _ffn_sample_kernel(x_ref, att_ref, rwo_ref, gate_ref, e_ref, wpa_ref, wpb_ref, wo_ref, g2_ref, wup_ref,
                       cw_ref, cb_ref, wdown_ref, gf_ref, y_ref, ug_ref, *, seq):
    rows = x_ref.shape[0]
    x = _branch_mix(x_ref, att_ref, rwo_ref, gate_ref, wpa_ref, wpb_ref, wo_ref)
    h = _rms(x, g2_ref[...]).astype(BF16)
    t = lax.broadcasted_iota(jnp.int32, (rows, 1), 0) % seq

    def shifted(ug, uv, cs):
        e = e_ref[:, cs]
        ug_m1 = jnp.where(t == 0, pltpu.roll(e, rows - 1, 0), pltpu.roll(ug, 1, 0))
        ug_m2 = jnp.where(t < 2, e, pltpu.roll(ug, 2, 0))
        return ug, ug_m1, ug_m2, uv

    def ug_sink(ug, cs):
        ug_ref[:, cs] = ug

    x2 = _conv_ffn_cols(h, x, wup_ref, cw_ref, cb_ref, wdown_ref, shifted, ug_sink)
    y_ref[...] = _rms(x2, gf_ref[...])


def _ffn_prompt(x2d, att, rwo, gates, batch, seq, lw, final_g, tm):
    n, d = x2d.shape
    d_ff = lw["conv_b"].shape[0]
    nt = seq // tm
    rows = lambda a: pl.BlockSpec((tm, a.shape[1]), lambda b, i: (b * nt + i, 0))
    full = lambda a: _resident(a.shape)
    consts = [lw["w_pa"], lw["w_pb"], lw["w_o"], lw["norm2_g"].reshape(1, d), lw["w_up"], lw["conv_w"],
              lw["conv_b"].reshape(1, d_ff), lw["w_down"], final_g.reshape(1, d)]
    acts = [x2d, att, rwo, gates]
    y, ug_last = pl.pallas_call(
        _ffn_prompt_kernel,
        grid=(batch, nt),
        in_specs=[rows(a) for a in acts] + [full(c) for c in consts],
        out_specs=[
            pl.BlockSpec((tm, d), lambda b, i: (b * nt + i, 0)),
            pl.BlockSpec((SUBLANES, d_ff), lambda b, i: (b, 0)),
        ],
        out_shape=[
            jax.ShapeDtypeStruct((n, d), F32),
            jax.ShapeDtypeStruct((batch * SUBLANES, d_ff), F32),
        ],
        scratch_shapes=[pltpu.VMEM((SUBLANES, d_ff), F32)],
        compiler_params=_cparams(("arbitrary", "arbitrary")),
        name="mix_conv_ffn_prompt",
    )(*acts, *consts)
    conv_new = ug_last.reshape(batch, SUBLANES, d_ff)[:, SUBLANES - (CONV_W - 1):]
    return y, conv_new


def _ffn_sample(x2d, att, rwo, gates, batch, seq, conv0, lw, final_g, bt):
    n, d = x2d.shape
    d_ff = lw["conv_b"].shape[0]
    nrows = bt * seq
    e = jnp.pad(conv0, ((0, 0), (0, seq - (CONV_W - 1)), (0, 0))).reshape(n, d_ff)
    rows = lambda a: pl.BlockSpec((nrows, a.shape[1]), lambda i: (i, 0))
    full = lambda a: _resident(a.shape)
    consts = [lw["w_pa"], lw["w_pb"], lw["w_o"], lw["norm2_g"].reshape(1, d), lw["w_up"], lw["conv_w"],
              lw["conv_b"].reshape(1, d_ff), lw["w_down"], final_g.reshape(1, d)]
    acts = [x2d, att, rwo, gates, e]
    y, ug = pl.pallas_call(
        functools.partial(_ffn_sample_kernel, seq=seq),
        grid=(n // nrows,),
        in_specs=[rows(a) for a in acts] + [full(c) for c in consts],
        out_specs=[
            pl.BlockSpec((nrows, d), lambda i: (i, 0)),
            pl.BlockSpec((nrows, d_ff), lambda i: (i, 0)),
        ],
        out_shape=[jax.ShapeDtypeStruct((n, d), F32), jax.ShapeDtypeStruct((n, d_ff), F32)],
        compiler_params=_cparams(("parallel",)),
        name="mix_conv_ffn_sample",
    )(*acts, *consts)
    conv_new = ug.reshape(batch, seq, d_ff)[:, seq - (CONV_W - 1):]
    return y, conv_new


def _row_tile(n, want):
    t = min(want, n)
    while n % t:
        t //= 2
    return t


def _layer(x, is_prompt, state, rel_bias, lw, final_g):
    batch, seq, d = x.shape
    n = batch * seq
    x2d = x.reshape(n, d)
    n_qkv = Q_W + 2 * KV_W
    n_rw = lw["mu_shift"].shape[0]
    tm = _row_tile(n, 512)
    qkv, rw, gates = _in_proj(x2d, lw["norm1_g"], lw["w_in"], n_qkv, n_rw, tm)
    kv = qkv.reshape(batch, seq, n_qkv)
    k_new = kv[:, :, Q_W:Q_W + KV_W]
    v_new = kv[:, :, Q_W + KV_W:]
    if is_prompt:
        att = _attention_prompt(qkv, batch, seq, rel_bias, lw["sinks"])
        wb = state["win_buf"]
        k_rows, v_rows = k_new[:, seq - wb:], v_new[:, seq - wb:]
        shift0 = wkv0 = None
    else:
        cache_k, cache_v = state["cache_k"], state["cache_v"]
        wb = cache_k.shape[1]
        assert seq <= wb
        att, k_rows, v_rows = _attention_sample(qkv, batch, seq, cache_k, cache_v, rel_bias, lw["sinks"],
                                                _row_tile(batch, 16))
        shift0, wkv0 = state["shift"], state["wkv"]
    k_rows = k_rows.reshape(batch, -1, N_KV, HEAD_DIM)
    v_rows = v_rows.reshape(batch, -1, N_KV, HEAD_DIM)
    rwo, wkv_new = _rwkv_mixer(rw, batch, seq, shift0, wkv0, lw["mu_shift"], lw["w0"], lw["w2"], lw["a0"],
                               lw["a2"], lw["g2"], lw["k_k"], lw["k_a"], lw["r_k"], lw["lnx_g"], lw["lnx_b"])
    shift_new = rw.reshape(batch, seq, n_rw)[:, seq - 1]
    if is_prompt:
        y, conv_new = _ffn_prompt(x2d, att, rwo, gates, batch, seq, lw, final_g, _row_tile(seq, FFN_ROWS))
    else:
        y, conv_new = _ffn_sample(x2d, att, rwo, gates, batch, seq, state["conv"], lw, final_g,
                                  _row_tile(batch, FFN_ROWS_WITH_STATE // seq))
    return y.reshape(batch, seq, d), (k_rows, v_rows, shift_new, wkv_new, conv_new)


def kernel(x_prompt, x_sample, cache_win_k, cache_win_v, state_shift, state_wkv, state_conv, rel_bias, norm1_g,
           w_in, sinks, mu_shift, w0, w2, a0, a2, g2, k_k, k_a, r_k, lnx_g, lnx_b, w_pa, w_pb, w_o, norm2_g,
           w_up, conv_w, conv_b, w_down, final_g):
    depth = w_in.shape[0]
    assert depth == 1, "the final norm is fused into the layer's last kernel"
    l = 0
    lw = dict(norm1_g=norm1_g[l], w_in=w_in[l].astype(BF16), sinks=sinks[l], mu_shift=mu_shift[l], w0=w0[l],
              w2=w2[l], a0=a0[l], a2=a2[l], g2=g2[l], k_k=k_k[l], k_a=k_a[l], r_k=r_k[l].reshape(-1),
              lnx_g=lnx_g[l], lnx_b=lnx_b[l], w_pa=w_pa[l].astype(BF16), w_pb=w_pb[l].astype(BF16),
              w_o=w_o[l].astype(BF16), norm2_g=norm2_g[l], w_up=w_up[l].astype(BF16), conv_w=conv_w[l],
              conv_b=conv_b[l], w_down=w_down[l].astype(BF16))
    win_buf = cache_win_k.shape[2]
    y_p, st_p = _layer(x_prompt, True, dict(win_buf=win_buf), rel_bias, lw, final_g)
    y_s, st_s = _layer(x_sample, False,
                       dict(cache_k=cache_win_k[l], cache_v=cache_win_v[l], shift=state_shift[l],
                            wkv=state_wkv[l], conv=state_conv[l]), rel_bias, lw, final_g)
    stack = lambda t: t[None]
    return (y_p, y_s) + tuple(stack(t) for t in st_p) + tuple(stack(t) for t in st_s)
```

```python
import functools
import math

import numpy as np
import jax
import jax.numpy as jnp
from jax import lax
from jax.experimental import pallas as pl
from jax.experimental.pallas import tpu as pltpu

F32 = jnp.float32
BF16 = jnp.bfloat16

HEAD_DIM = 64
N_HEADS = 8
N_KV = 2
WINDOW = 128
N_BUCKETS = 32
MAX_EXACT = N_BUCKETS // 2
REL_MAX_DIST = 128
RW_N = 64
RW_HEADS = 8
RW = RW_HEADS * RW_N
NORM_EPS = 1e-6
GN_EPS = 64e-5
NEG = -1e30
CONV_W = 3

Q_W = N_HEADS * HEAD_DIM
KV_W = N_KV * HEAD_DIM
LANES = 128
SUBLANES = 8
MXU_WIDTH = 256
CHUNK = 64
VMEM_LIMIT = 56 * 1024 * 1024
FFN_COL_CHUNKS = 3
FFN_ROWS = 512
FFN_ROWS_WITH_STATE = 256
ATTN_BLOCKS = 4
WKV_UNITS = 8
WKV_UNITS_WITH_STATE = 4


def _resident(shape):
    return pl.BlockSpec(shape, lambda *_: (0,) * len(shape), pipeline_mode=pl.Buffered(1))


def _cparams(sem):
    return pltpu.CompilerParams(dimension_semantics=sem, vmem_limit_bytes=VMEM_LIMIT)


def _sigmoid(x):
    return 1.0 / (1.0 + jnp.exp(-x))


def _dg(a, b, kind):
    if kind == "nn":
        dn = (((1,), (0,)), ((), ()))
    else:
        dn = (((1,), (1,)), ((), ()))
    return lax.dot_general(a, b, dn, preferred_element_type=F32)


def _split(x):
    hi = x.astype(BF16)
    lo = (x - hi.astype(F32)).astype(BF16)
    return hi, lo


def _mm(a, b, kind="nn", passes=1):
    if passes == 1:
        return _dg(a.astype(BF16), b.astype(BF16), kind)
    ah, al = _split(a)
    bh, bl = _split(b)
    return _dg(ah, bh, kind) + (_dg(ah, bl, kind) + _dg(al, bh, kind))


def _mm_exact_lhs(a_bf16, b, n_terms):
    out = None
    rem = b
    for _ in range(n_terms):
        piece = rem.astype(BF16)
        term = _dg(a_bf16, piece, "nn")
        out = term if out is None else out + term
        rem = rem - piece.astype(F32)
    return out


def _mm_exact_rhs(a, b_bf16, n_terms):
    out = None
    rem = a
    for _ in range(n_terms):
        piece = rem.astype(BF16)
        term = _dg(piece, b_bf16, "nn")
        out = term if out is None else out + term
        rem = rem - piece.astype(F32)
    return out


def _inproj_kernel(x_ref, g_ref, w_ref, qkv_ref, rw_ref, gate_ref, *, n_qkv, n_rw):
    x = x_ref[...]
    ms = jnp.mean(x * x, axis=-1, keepdims=True)
    h = (x * lax.rsqrt(ms + NORM_EPS) * g_ref[...]).astype(BF16)
    n_gate = gate_ref.shape[1]
    step = 2 * MXU_WIDTH
    plain = [(qkv_ref, c, c, min(step, n_qkv - c)) for c in range(0, n_qkv, step)]
    plain += [(rw_ref, c, n_qkv + c, min(step, n_rw - c)) for c in range(0, n_rw, step)]
    gated = [(gate_ref, c, n_qkv + n_rw + c, min(step, n_gate - c)) for c in range(0, n_gate, step)]
    pieces = []
    while plain or gated:
        pieces += [gated.pop(0)] if gated else []
        pieces += [plain.pop(0)] if plain else []
    dot = lambda p: jnp.dot(h, w_ref[:, p[2]:p[2] + p[3]], preferred_element_type=F32)
    pending = dot(pieces[0])
    for j, (ref, c0, _, width) in enumerate(pieces):
        out = pending
        if j + 1 < len(pieces):
            pending = dot(pieces[j + 1])
        if ref is gate_ref:
            out = _sigmoid(out)
        ref[:, c0:c0 + width] = out.astype(ref.dtype)


def _in_proj(x2d, g, w_bf16, n_qkv, n_rw, tm):
    n, d = x2d.shape
    n_gate = w_bf16.shape[1] - n_qkv - n_rw
    return pl.pallas_call(
        functools.partial(_inproj_kernel, n_qkv=n_qkv, n_rw=n_rw),
        grid=(n // tm,),
        in_specs=[
            pl.BlockSpec((tm, d), lambda i: (i, 0)),
            _resident((1, d)),
            _resident(w_bf16.shape),
        ],
        out_specs=[
            pl.BlockSpec((tm, n_qkv), lambda i: (i, 0)),
            pl.BlockSpec((tm, n_rw), lambda i: (i, 0)),
            pl.BlockSpec((tm, n_gate), lambda i: (i, 0)),
        ],
        out_shape=[
            jax.ShapeDtypeStruct((n, n_qkv), F32),
            jax.ShapeDtypeStruct((n, n_rw), F32),
            jax.ShapeDtypeStruct((n, n_gate), BF16),
        ],
        compiler_params=_cparams(("parallel",)),
        name="in_proj",
    )(x2d, g.reshape(1, d), w_bf16)


def _t5_bucket_np(dist):
    n = np.maximum(dist, 0)
    nf = np.maximum(n, 1).astype(np.float32)
    large = MAX_EXACT + (np.log(nf / MAX_EXACT) / math.log(REL_MAX_DIST / MAX_EXACT)
                         * (N_BUCKETS - MAX_EXACT)).astype(np.int32)
    return np.where(n < MAX_EXACT, n, np.minimum(large, N_BUCKETS - 1)).astype(np.int32)


def _attn_kernel(q_ref, k1_ref, k2_ref, v1_ref, v2_ref, bucket_ref, relb_ref, sink_ref, o_ref, *rest,
                 nq, nk, n_blocks, first_block_axis, emit_window):
    bt = q_ref.shape[0]
    bias_ref = rest[-1]
    if emit_window:
        n_old, n_new = k1_ref.shape[1], k2_ref.shape[1]
        for w_ref, old_ref, new_ref in ((rest[0], k1_ref, k2_ref), (rest[1], v1_ref, v2_ref)):
            w_ref[:, :n_old - n_new] = old_ref[:, n_new:]
            w_ref[:, n_old - n_new:] = new_ref[...]
    first_step = pl.program_id(0) == 0
    if first_block_axis is not None:
        first_step = jnp.logical_and(first_step, pl.program_id(1) == 0)

    @pl.when(first_step)
    def _():
        bucket = bucket_ref[...]
        prev_key = lax.broadcasted_iota(jnp.int32, (1, nk), 1) < (nk // 2)
        for n in range(N_HEADS):
            acc = jnp.full((nq, nk), NEG, F32)
            for b in range(N_BUCKETS):
                acc = jnp.where(bucket == b, relb_ref[b, n], acc)
            c, half = divmod(n, 2)
            bias_ref[0, c, :, half * nk:(half + 1) * nk] = acc
            if first_block_axis is not None:
                bias_ref[1, c, :, half * nk:(half + 1) * nk] = jnp.where(prev_key, NEG, acc)

    n_keys = nk + (n_blocks - 1) * nq

    def padded(a_ref, b_ref):
        parts = [a_ref[...], b_ref[...]]
        n_now = a_ref.shape[1] + b_ref.shape[1]
        if n_now < n_keys:
            parts.append(jnp.zeros((bt, n_keys - n_now, LANES), F32))
        return jnp.concatenate(parts, axis=1)

    kk = padded(k1_ref, k2_ref)
    vv = padded(v1_ref, v2_ref)
    kk_r = pltpu.roll(kk, HEAD_DIM, 2)
    vv_r = pltpu.roll(vv, HEAD_DIM, 2)
    lane = lax.broadcasted_iota(jnp.int32, (1, 1, LANES), 2)
    lo = lane < HEAD_DIM

    def halves(x, x_r, kvh):
        src_lo, src_hi = (x, x_r) if kvh == 0 else (x_r, x)
        even = jnp.where(lo, src_lo, 0.0).astype(BF16)
        odd = jnp.where(lo, 0.0, src_hi).astype(BF16)
        return even, odd

    k_eo = [halves(kk, kk_r, h) for h in range(N_KV)]
    v_eo = [halves(vv, vv_r, h) for h in range(N_KV)]

    def window(eo, j):
        return jnp.concatenate([eo[0][:, j * nq:j * nq + nk], eo[1][:, j * nq:j * nq + nk]], axis=1)

    n_cols = N_HEADS // 2
    units = [(j, c) for j in range(n_blocks) for c in range(n_cols)]
    kvh_of = lambda c: (2 * c) // (N_HEADS // N_KV)
    k_win = {(j, h): window(k_eo[h], j) for j in range(n_blocks) for h in range(N_KV)}
    v_win = {(j, h): window(v_eo[h], j) for j in range(n_blocks) for h in range(N_KV)}
    sums_on_mxu = bt == 1
    if sums_on_mxu:
        key_row = lax.broadcasted_iota(jnp.int32, (1, 2 * nk, 1), 1)
        ones_cols = jnp.where((key_row < nk) == lo, 1.0, 0.0).astype(BF16)
        v_win = {jh: jnp.concatenate([v, ones_cols], axis=2) for jh, v in v_win.items()}

    scale = HEAD_DIM ** -0.5
    first_table = 0
    if first_block_axis is not None:
        first_table = jnp.where(pl.program_id(first_block_axis) == 0, 1, 0)
    s_all = {}
    for j, c in units:
        qc = (q_ref[:, j * nq:(j + 1) * nq, c * LANES:(c + 1) * LANES] * scale).astype(BF16)
        s_all[j, c] = jnp.einsum("bqd,bkd->bqk", qc, k_win[j, kvh_of(c)], preferred_element_type=F32)
    e_all, sink_all = {}, {}
    for j, c in units:
        s = s_all[j, c] + bias_ref[first_table if j == 0 else 0, c][None]
        es, sink_terms = [], []
        for half in range(2):
            sh = s[:, :, half * nk:(half + 1) * nk]
            sink = sink_ref[2 * c + half]
            m = jnp.maximum(jnp.max(sh, axis=-1, keepdims=True), sink)
            e = jnp.exp(sh - m)
            es.append(e.astype(BF16))
            sink_terms.append(jnp.exp(sink - m) + (0.0 if sums_on_mxu else jnp.sum(e, axis=-1, keepdims=True)))
        e_all[j, c] = jnp.concatenate(es, axis=2)
        sink_all[j, c] = jnp.where(lo, sink_terms[0], sink_terms[1])
    for j, c in units:
        o = jnp.einsum("bqk,bkd->bqd", e_all[j, c], v_win[j, kvh_of(c)], preferred_element_type=F32)
        denom = sink_all[j, c] + (o[:, :, LANES:] if sums_on_mxu else 0.0)
        o_ref[:, j * nq:(j + 1) * nq, c * LANES:(c + 1) * LANES] = (o[:, :, :LANES] / denom).astype(o_ref.dtype)


def _attention_prompt(qkv, batch, seq, rel_bias, sinks):
    nblk = seq // WINDOW
    nb = ATTN_BLOCKS if nblk % ATTN_BLOCKS == 0 else 1
    nsteps = nblk // nb
    width = qkv.shape[1]
    q_blk = qkv.reshape(batch * nblk, WINDOW, width)
    q_step = qkv.reshape(batch * nsteps, nb * WINDOW, width)
    kcol = Q_W // LANES
    vcol = (Q_W + KV_W) // LANES
    nk = 2 * WINDOW
    qi = np.arange(WINDOW)[:, None] + WINDOW
    kj = np.arange(nk)[None, :]
    dist = qi - kj
    bucket = np.where((dist >= 0) & (dist < WINDOW), _t5_bucket_np(dist), -1).astype(np.int32)
    cur = lambda c: (lambda b, i: (b * nsteps + i, 0, c))
    prev = lambda c: (lambda b, i: (b * nblk + jnp.maximum(i * nb - 1, 0), 0, c))
    out = pl.pallas_call(
        functools.partial(_attn_kernel, nq=WINDOW, nk=nk, n_blocks=nb, first_block_axis=1, emit_window=False),
        grid=(batch, nsteps),
        in_specs=[
            pl.BlockSpec((1, nb * WINDOW, Q_W), cur(0)),
            pl.BlockSpec((1, WINDOW, LANES), prev(kcol)),
            pl.BlockSpec((1, nb * WINDOW, LANES), cur(kcol)),
            pl.BlockSpec((1, WINDOW, LANES), prev(vcol)),
            pl.BlockSpec((1, nb * WINDOW, LANES), cur(vcol)),
            _resident(bucket.shape),
            pl.BlockSpec(memory_space=pltpu.SMEM),
            pl.BlockSpec(memory_space=pltpu.SMEM),
        ],
        out_specs=pl.BlockSpec((1, nb * WINDOW, Q_W), cur(0)),
        out_shape=jax.ShapeDtypeStruct((batch * nsteps, nb * WINDOW, Q_W), BF16),
        scratch_shapes=[pltpu.VMEM((2, N_HEADS // 2, WINDOW, 2 * nk), F32)],
        compiler_params=_cparams(("arbitrary", "arbitrary")),
        name="attn_prompt",
    )(q_step, q_blk, q_step, q_blk, q_step, jnp.asarray(bucket), rel_bias, sinks)
    return out.reshape(batch * seq, Q_W)


def _attention_sample(qkv, batch, seq, cache_k, cache_v, rel_bias, sinks, bt):
    wb = cache_k.shape[1]
    nk = 2 * WINDOW
    q3 = qkv.reshape(batch, seq, qkv.shape[1])
    ck = cache_k.reshape(batch, wb, KV_W)
    cv = cache_v.reshape(batch, wb, KV_W)
    kcol = Q_W // LANES
    vcol = (Q_W + KV_W) // LANES
    tq = np.arange(seq)[:, None]
    j = np.arange(nk)[None, :]
    dist = np.where(j < wb, tq + wb - j, tq - (j - wb))
    ok = (dist >= 0) & (dist < WINDOW) & (j < wb + seq)
    bucket = np.where(ok, _t5_bucket_np(dist), -1).astype(np.int32)
    win_spec = pl.BlockSpec((bt, wb, LANES), lambda b: (b, 0, 0))
    win_shape = jax.ShapeDtypeStruct((batch, wb, KV_W), F32)
    out, win_k, win_v = pl.pallas_call(
        functools.partial(_attn_kernel, nq=seq, nk=nk, n_blocks=1, first_block_axis=None, emit_window=True),
        grid=(batch // bt,),
        in_specs=[
            pl.BlockSpec((bt, seq, Q_W), lambda b: (b, 0, 0)),
            pl.BlockSpec((bt, wb, LANES), lambda b: (b, 0, 0)),
            pl.BlockSpec((bt, seq, LANES), lambda b: (b, 0, kcol)),
            pl.BlockSpec((bt, wb, LANES), lambda b: (b, 0, 0)),
            pl.BlockSpec((bt, seq, LANES), lambda b: (b, 0, vcol)),
            _resident(bucket.shape),
            pl.BlockSpec(memory_space=pltpu.SMEM),
            pl.BlockSpec(memory_space=pltpu.SMEM),
        ],
        out_specs=[pl.BlockSpec((bt, seq, Q_W), lambda b: (b, 0, 0)), win_spec, win_spec],
        out_shape=[jax.ShapeDtypeStruct((batch, seq, Q_W), BF16), win_shape, win_shape],
        scratch_shapes=[pltpu.VMEM((1, N_HEADS // 2, seq, 2 * nk), F32)],
        compiler_params=_cparams(("arbitrary",)),
        name="attn_sample",
    )(q3, ck, q3, cv, q3, jnp.asarray(bucket), rel_bias, sinks)
    return out.reshape(batch * seq, Q_W), win_k, win_v


P_SCORE = 1
P_TINV = 1
P_INTRA = 1
P_STATE = 1


def _wkv_kernel(*refs, n_units, n_seg, has_state, lora_w):
    C = CHUNK
    seg_len = C // n_seg
    n_pairs = RW // LANES
    it = iter(refs)
    p_ref = next(it)
    p0_ref, s0_ref = (next(it), next(it)) if has_state else (None, None)
    (mu_ref, w2a2_ref, w0_ref, a0_ref, g2_ref, kk_ref, ka_ref, rk_ref, lng_ref, lnb_ref, ltri_ref) = (
        next(it) for _ in range(11))
    elast_ref = next(it) if n_seg > 1 else None
    ones_ref, out_ref, sout_ref = next(it), next(it), next(it)
    last_ref, sbd_ref = (None, None) if has_state else (next(it), next(it))

    if not has_state:
        @pl.when(pl.program_id(1) == 0)
        def _():
            last_ref[...] = jnp.zeros_like(last_ref)
            sbd_ref[...] = jnp.zeros_like(sbd_ref)

    row = lax.broadcasted_iota(jnp.int32, (C, 1), 0)
    rows_of = lambda u: slice(u * C, (u + 1) * C)
    lane = lax.broadcasted_iota(jnp.int32, (1, LANES), 1)
    lo = lane < RW_N
    ones_bd = ones_ref[...]

    def headsum(x):
        xb = x.astype(BF16)
        return jnp.concatenate([_dg(xb[:, c:c + MXU_WIDTH], ones_bd, "nn") for c in range(0, RW, MXU_WIDTH)],
                               axis=1)

    def prepare(us):
        xs_parts = []
        for u in us:
            p = p_ref[u, 0]
            rolled = pltpu.roll(p, 1, 0)
            if has_state:
                prev = jnp.where(row % seg_len == 0, p0_ref[u, 0], rolled)
            else:
                prev = jnp.where(row == 0, last_ref[u], rolled)
                last_ref[u] = p_ref[u, 0, C - 1:C, :]
            xs_parts.append(p + mu_ref[...] * (prev - p))
        xs = jnp.concatenate(xs_parts, axis=0)
        r = xs[:, 0:RW]
        k = xs[:, RW:2 * RW]
        v = xs[:, 2 * RW:3 * RW]
        lwla = xs[:, 3 * RW:3 * RW + LANES]
        lg = xs[:, 3 * RW + LANES:3 * RW + 2 * LANES]
        lwla = jnp.where(lane < lora_w, jnp.tanh(lwla), lwla)
        wa = jnp.dot(lwla.astype(BF16), w2a2_ref[...], preferred_element_type=F32)
        logw = -math.exp(-0.5) * _sigmoid(w0_ref[...] + wa[:, :RW])
        a_sig = _sigmoid(a0_ref[...] + wa[:, RW:])
        g = jnp.dot(_sigmoid(lg).astype(BF16), g2_ref[...], preferred_element_type=F32)
        kk = k * kk_ref[...]
        kk = kk * (1.0 / jnp.maximum(jnp.sqrt(headsum(kk * kk)), 1e-12))
        k = k * (1.0 + (a_sig - 1.0) * ka_ref[...])
        a = -kk
        b = kk * a_sig
        bonus = headsum(r * k * rk_ref[...]) * v
        cws, cwl = [], []
        for i in range(len(us)):
            cw_u = _mm_exact_lhs(ltri_ref[...], logw[rows_of(i)], 3)
            cws.append(cw_u)
            if n_seg == 1:
                cwl.append(jnp.broadcast_to(cw_u[C - 1:C, :], (C, RW)))
            else:
                cwl.append(_mm_exact_lhs(elast_ref[...], cw_u, 3))
        cw = jnp.concatenate(cws, axis=0)
        cw_last = jnp.concatenate(cwl, axis=0)
        w_inv = jnp.exp(-cw)
        b_t, k_t = b * w_inv, k * w_inv
        w_tail = jnp.exp(cw_last - cw)
        b_h, k_h = b * w_tail, k * w_tail
        a_t, r_t = a * jnp.exp(cw - logw), r * jnp.exp(cw)
        w_c = jnp.exp(cw_last)
        return a_t, r_t, b_t, k_t, b_h, k_h, v, w_c, bonus, g

    def bd(y):
        return jnp.concatenate([jnp.where(lo, y, 0.0), jnp.where(lo, 0.0, y)], axis=0)

    zeros_head = jnp.zeros((RW_N, RW_N), F32)

    def pack_pair(s_even, s_odd):
        return jnp.concatenate([jnp.concatenate([s_even, zeros_head], axis=1),
                                jnp.concatenate([zeros_head, s_odd], axis=1)], axis=0)

    def store_pair(ref, i, q, s_pair):
        ref[i, 2 * q] = s_pair[:RW_N, :RW_N]
        ref[i, 2 * q + 1] = s_pair[RW_N:, RW_N:]

    s_idx = lane % C
    strict = s_idx < row
    incl = s_idx <= row
    if n_seg > 1:
        same_seg = (s_idx // seg_len) == (row // seg_len)
        strict = jnp.logical_and(strict, same_seg)
        incl = jnp.logical_and(incl, same_seg)
    row2 = lax.broadcasted_iota(jnp.int32, (2 * C, 1), 0)
    same_head = (row2 < RW_N) == lo

    part = lambda x, i, q: x[i * C:(i + 1) * C, q * LANES:(q + 1) * LANES]

    def recurrence(us, ops):
        a_t, r_t, b_t, k_t, b_h, k_h, v, w_c = ops[:8]
        units = [(i, q) for i in range(len(us)) for q in range(n_pairs)]
        sc = {uq: _mm(jnp.concatenate([part(a_t, *uq), part(r_t, *uq)], axis=0),
                      jnp.concatenate([bd(part(b_t, *uq)), bd(part(k_t, *uq))], axis=0),
                      "nt", P_SCORE) for uq in units}
        pw = {uq: jnp.where(strict, sc[uq][:C, :LANES], 0.0) for uq in units}
        m_rb = {uq: jnp.where(incl, sc[uq][C:, :LANES], 0.0) for uq in units}
        lm_v = {uq: _mm(jnp.concatenate([jnp.where(strict, sc[uq][:C, LANES:], 0.0),
                                         jnp.where(incl, sc[uq][C:, LANES:], 0.0)], axis=0),
                        bd(part(v, *uq)), "nn", P_INTRA) for uq in units}
        tinv = {uq: pw[uq] + jnp.where(s_idx == row, 1.0, 0.0) for uq in units}
        n_lvl = int(math.log2(seg_len))
        for lvl in range(1, n_lvl):
            last = lvl + 1 == n_lvl
            if lvl == 1:
                for uq in units:
                    pw[uq] = _mm(pw[uq], bd(pw[uq]), "nn", P_TINV)
            for uq in units:
                rhs = bd(tinv[uq]) if last else jnp.concatenate([bd(tinv[uq]), bd(pw[uq])], axis=1)
                upd = _mm(pw[uq], rhs, "nn", P_TINV)
                tinv[uq] = tinv[uq] + upd[:, :LANES]
                if not last:
                    pw[uq] = upd[:, LANES:]
        x = {uq: _mm(tinv[uq], jnp.concatenate([bd(part(a_t, *uq)), bd(lm_v[uq][:C])], axis=1), "nn", P_TINV)
             for uq in units}
        a_hat = {uq: x[uq][:, :LANES] for uq in units}
        v_hat = {uq: x[uq][:, LANES:] for uq in units}
        z = {uq: _mm(m_rb[uq], jnp.concatenate([bd(a_hat[uq]), bd(v_hat[uq])], axis=1), "nn", P_INTRA)
             for uq in units}
        r_hat = {uq: part(r_t, *uq) + z[uq][:, :LANES] for uq in units}
        y_intra = {uq: z[uq][:, LANES:] + lm_v[uq][C:] for uq in units}
        ys = {}
        if n_seg == 1:
            s_old = {(i, q): sbd_ref[us[i], q] for i, q in units}
            t1 = {uq: _mm(jnp.concatenate([a_hat[uq], r_hat[uq]], axis=0), s_old[uq], "nt", P_STATE)
                  for uq in units}
            for uq in units:
                ys[uq] = t1[uq][C:] + y_intra[uq]
                uv = jnp.concatenate([t1[uq][:C] + v_hat[uq], part(v, *uq)], axis=0)
                bkh = jnp.concatenate([part(b_h, *uq), part(k_h, *uq)], axis=0)
                ds = _mm(uv.T, bkh, "nn", P_STATE)
                sbd_ref[us[uq[0]], uq[1]] = s_old[uq] * part(w_c, *uq)[0:1] + jnp.where(same_head, ds, 0.0)
        else:
            row_seg = (row2 % C) // seg_len
            for uq in units:
                i, q = uq
                first_seq = us[i] * n_seg
                u_parts, y_parts, s_olds = [], [], []
                for sg in range(n_seg):
                    rs = slice(sg * seg_len, (sg + 1) * seg_len)
                    s_sg = pack_pair(s0_ref[first_seq + sg, 2 * q], s0_ref[first_seq + sg, 2 * q + 1])
                    t1 = _mm(jnp.concatenate([a_hat[uq][rs], r_hat[uq][rs]], axis=0), s_sg, "nt", P_STATE)
                    u_parts.append(t1[:seg_len] + v_hat[uq][rs])
                    y_parts.append(t1[seg_len:] + y_intra[uq][rs])
                    s_olds.append(s_sg)
                ys[uq] = jnp.concatenate(y_parts, axis=0)
                uv_t = jnp.concatenate(u_parts + [part(v, *uq)], axis=0).T
                bkh = jnp.concatenate([part(b_h, *uq), part(k_h, *uq)], axis=0)
                w_cq = part(w_c, *uq)
                for sg in range(n_seg):
                    ds = _mm(uv_t, jnp.where(row_seg == sg, bkh, 0.0), "nn", P_STATE)
                    store_pair(sout_ref, first_seq + sg, q,
                               s_olds[sg] * w_cq[sg * seg_len:sg * seg_len + 1] + jnp.where(same_head, ds, 0.0))
        return ys

    def finish(us, ys, ops):
        bonus, g = ops[8:]
        y = jnp.concatenate([jnp.concatenate([ys[i, q] for q in range(n_pairs)], axis=1)
                             for i in range(len(us))], axis=0)
        mean = headsum(y) * (1.0 / RW_N)
        d = y - mean
        var = headsum(d * d) * (1.0 / RW_N)
        y = d * lax.rsqrt(var + GN_EPS) * lng_ref[...] + lnb_ref[...]
        y = ((y + bonus) * g).astype(out_ref.dtype)
        for i, u in enumerate(us):
            out_ref[u, 0] = y[rows_of(i)]

    all_units = list(range(n_units))
    ops = prepare(all_units)
    finish(all_units, recurrence(all_units, ops), ops)
    units = [(u, q) for u in range(n_units) for q in range(n_pairs)]

    if not has_state:
        @pl.when(pl.program_id(1) == pl.num_programs(1) - 1)
        def _():
            for u, q in units:
                store_pair(sout_ref, u, q, sbd_ref[u, q])


def _rwkv_mixer(rw, batch, seq, shift0, wkv0, mu, w0, w2, a0, a2, g2, k_k, k_a, r_k, lnx_g, lnx_b):
    n_shift = rw.shape[1]
    lora_w, lora_a = w2.shape[0], a2.shape[0]
    assert lora_w + lora_a == LANES and g2.shape[0] == LANES and n_shift == 3 * RW + 2 * LANES
    has_state = shift0 is not None
    C = CHUNK
    n_pairs = RW // LANES
    if has_state:
        assert C % seq == 0 and batch % (C // seq) == 0
        n_seg, n_chunks, n_groups = C // seq, 1, batch * seq // C
    else:
        assert seq % C == 0
        n_seg, n_chunks, n_groups = 1, seq // C, batch
    want = WKV_UNITS_WITH_STATE if has_state else WKV_UNITS
    nu = want if n_groups % want == 0 else 1
    seg_len = C // n_seg
    w2a2 = jnp.zeros((LANES, 2 * RW), F32).at[:lora_w, :RW].set(w2).at[lora_w:, RW:].set(a2).astype(BF16)
    t = np.arange(C)
    same_seg = (t[:, None] // seg_len) == (t[None, :] // seg_len)
    ltri = jnp.asarray(((t[:, None] >= t[None, :]) & same_seg).astype(np.float32), BF16)
    elast = jnp.asarray((t[None, :] == (t[:, None] // seg_len) * seg_len + seg_len - 1).astype(np.float32), BF16)
    ones_bd = jnp.asarray(np.kron(np.eye(MXU_WIDTH // RW_N, dtype=np.float32),
                                  np.ones((RW_N, RW_N), np.float32)), BF16)
    row = lambda x: x.reshape(1, -1).astype(F32)
    rw4 = rw.reshape(n_groups, n_chunks, C, n_shift)
    blk = lambda w: pl.BlockSpec((nu, 1, C, w), lambda i, c: (i, c, 0, 0))
    st_blk = pl.BlockSpec((nu * n_seg, RW_HEADS, RW_N, RW_N), lambda i, c: (i, 0, 0, 0))
    args, specs = [rw4], [blk(n_shift)]
    if has_state:
        p0 = jnp.pad(shift0[:, None, :], ((0, 0), (0, seq - 1), (0, 0))).reshape(n_groups, 1, C, n_shift)
        args += [p0, wkv0]
        specs += [blk(n_shift), st_blk]
    consts = [row(mu), w2a2, row(w0), row(a0), g2.astype(BF16), row(k_k), row(k_a), row(r_k), row(lnx_g),
              row(lnx_b), ltri] + ([elast] if n_seg > 1 else []) + [ones_bd]
    args += consts
    specs += [_resident(c.shape) for c in consts]
    scratch = [] if has_state else [pltpu.VMEM((nu, 1, n_shift), F32),
                                    pltpu.VMEM((nu, n_pairs, LANES, LANES), F32)]
    out, s_new = pl.pallas_call(
        functools.partial(_wkv_kernel, n_units=nu, n_seg=n_seg, has_state=has_state, lora_w=lora_w),
        grid=(n_groups // nu, n_chunks),
        in_specs=specs,
        out_specs=[blk(RW), st_blk],
        out_shape=[
            jax.ShapeDtypeStruct((n_groups, n_chunks, C, RW), BF16),
            jax.ShapeDtypeStruct((batch, RW_HEADS, RW_N, RW_N), F32),
        ],
        scratch_shapes=scratch,
        compiler_params=_cparams(("arbitrary", "arbitrary")),
        name="rwkv7",
    )(*args)
    return out.reshape(batch * seq, RW), s_new


def _branch_mix(x_ref, att_ref, rwo_ref, gate_ref, wpa_ref, wpb_ref, wo_ref):
    d = x_ref.shape[1]
    pa = jnp.dot(att_ref[...], wpa_ref[...], preferred_element_type=F32)
    pb = jnp.dot(rwo_ref[...], wpb_ref[...], preferred_element_type=F32)
    mix = gate_ref[:, :d] * pa + gate_ref[:, d:] * pb
    return x_ref[...] + jnp.dot(mix.astype(BF16), wo_ref[...], preferred_element_type=F32)


def _rms(x, g):
    ms = jnp.mean(x * x, axis=-1, keepdims=True)
    return x * lax.rsqrt(ms + NORM_EPS) * g


def _gelu_tanh(c):
    return c * (0.5 * (1.0 + jnp.tanh(math.sqrt(2.0 / math.pi) * (c + 0.044715 * (c * c * c)))))


def _conv_ffn_cols(h, x, wup_ref, cw_ref, cb_ref, wdown_ref, shifted, ug_sink):
    d_ff = cb_ref.shape[1]
    tiles = -(-d_ff // MXU_WIDTH)
    edges = [min(d_ff, MXU_WIDTH * ((tiles * j + FFN_COL_CHUNKS - 1) // FFN_COL_CHUNKS))
             for j in range(FFN_COL_CHUNKS + 1)]
    spans = list(zip(edges[:-1], edges[1:]))

    def up(span):
        lo_c, hi_c = span
        return (jnp.dot(h, wup_ref[:, lo_c:hi_c], preferred_element_type=F32),
                jnp.dot(h, wup_ref[:, d_ff + lo_c:d_ff + hi_c], preferred_element_type=F32))

    acc = x
    pending = up(spans[0])
    for j, (lo_c, hi_c) in enumerate(spans):
        cs = slice(lo_c, hi_c)
        ug_full, uv_full = pending
        if j + 1 < len(spans):
            pending = up(spans[j + 1])
        ug, ug_m1, ug_m2, uv = shifted(ug_full, uv_full, cs)
        c = cb_ref[:, cs] + cw_ref[0:1, cs] * ug_m2 + cw_ref[1:2, cs] * ug_m1 + cw_ref[2:3, cs] * ug
        act = (_gelu_tanh(c) * uv).astype(BF16)
        acc = acc + jnp.dot(act, wdown_ref[lo_c:hi_c, :], preferred_element_type=F32)
        ug_sink(ug, cs)
    return acc


def _ffn_prompt_kernel(x_ref, att_ref, rwo_ref, gate_ref, wpa_ref, wpb_ref, wo_ref, g2_ref, wup_ref, cw_ref,
                       cb_ref, wdown_ref, gf_ref, y_ref, ug_ref, carry_ref):
    tm = x_ref.shape[0]
    x = _branch_mix(x_ref, att_ref, rwo_ref, gate_ref, wpa_ref, wpb_ref, wo_ref)
    h = _rms(x, g2_ref[...]).astype(BF16)
    seq_start = pl.program_id(1) == 0

    def shifted(ug, uv, cs):
        before = jnp.where(seq_start, 0.0, carry_ref[:, cs])
        carry_ref[:, cs] = ug[tm - SUBLANES:]
        ug_e = jnp.concatenate([before, ug], axis=0)
        return ug, pltpu.roll(ug_e, 1, 0)[SUBLANES:], pltpu.roll(ug_e, 2, 0)[SUBLANES:], uv

    def ug_sink(ug, cs):
        ug_ref[:, cs] = ug[tm - SUBLANES:]

    x2 = _conv_ffn_cols(h, x, wup_ref, cw_ref, cb_ref, wdown_ref, shifted, ug_sink)
    y_ref[...] = _rms(x2, gf_ref[...])


def _ffn_sample_kernel(x_ref, att_ref, rwo_ref, gate_ref, e_ref, wpa_ref, wpb_ref, wo_ref, g2_ref, wup_ref,
                       cw_ref, cb_ref, wdown_ref, gf_ref, y_ref, ug_ref, *, seq):
    rows = x_ref.shape[0]
    x = _branch_mix(x_ref, att_ref, rwo_ref, gate_ref, wpa_ref, wpb_ref, wo_ref)
    h = _rms(x, g2_ref[...]).astype(BF16)
    t = lax.broadcasted_iota(jnp.int32, (rows, 1), 0) % seq

    def shifted(ug, uv, cs):
        e = e_ref[:, cs]
        ug_m1 = jnp.where(t == 0, pltpu.roll(e, rows - 1, 0), pltpu.roll(ug, 1, 0))
        ug_m2 = jnp.where(t < 2, e, pltpu.roll(ug, 2, 0))
        return ug, ug_m1, ug_m2, uv

    def ug_sink(ug, cs):
        ug_ref[:, cs] = ug

    x2 = _conv_ffn_cols(h, x, wup_ref, cw_ref, cb_ref, wdown_ref, shifted, ug_sink)
    y_ref[...] = _rms(x2, gf_ref[...])


def _ffn_prompt(x2d, att, rwo, gates, batch, seq, lw, final_g, tm):
    n, d = x2d.shape
    d_ff = lw["conv_b"].shape[0]
    nt = seq // tm
    rows = lambda a: pl.BlockSpec((tm, a.shape[1]), lambda b, i: (b * nt + i, 0))
    full = lambda a: _resident(a.shape)
    consts = [lw["w_pa"], lw["w_pb"], lw["w_o"], lw["norm2_g"].reshape(1, d), lw["w_up"], lw["conv_w"],
              lw["conv_b"].reshape(1, d_ff), lw["w_down"], final_g.reshape(1, d)]
    acts = [x2d, att, rwo, gates]
    y, ug_last = pl.pallas_call(
        _ffn_prompt_kernel,
        grid=(batch, nt),
        in_specs=[rows(a) for a in acts] + [full(c) for c in consts],
        out_specs=[
            pl.BlockSpec((tm, d), lambda b, i: (b * nt + i, 0)),
            pl.BlockSpec((SUBLANES, d_ff), lambda b, i: (b, 0)),
        ],
        out_shape=[
            jax.ShapeDtypeStruct((n, d), F32),
            jax.ShapeDtypeStruct((batch * SUBLANES, d_ff), F32),
        ],
        scratch_shapes=[pltpu.VMEM((SUBLANES, d_ff), F32)],
        compiler_params=_cparams(("arbitrary", "arbitrary")),
        name="mix_conv_ffn_prompt",
    )(*acts, *consts)
    conv_new = ug_last.reshape(batch, SUBLANES, d_ff)[:, SUBLANES - (CONV_W - 1):]
    return y, conv_new


def _ffn_sample(x2d, att, rwo, gates, batch, seq, conv0, lw, final_g, bt):
    n, d = x2d.shape
    d_ff = lw["conv_b"].shape[0]
    nrows = bt * seq
    e = jnp.pad(conv0, ((0, 0), (0, seq - (CONV_W - 1)), (0, 0))).reshape(n, d_ff)
    rows = lambda a: pl.BlockSpec((nrows, a.shape[1]), lambda i: (i, 0))
    full = lambda a: _resident(a.shape)
    consts = [lw["w_pa"], lw["w_pb"], lw["w_o"], lw["norm2_g"].reshape(1, d), lw["w_up"], lw["conv_w"],
              lw["conv_b"].reshape(1, d_ff), lw["w_down"], final_g.reshape(1, d)]
    acts = [x2d, att, rwo, gates, e]
    y, ug = pl.pallas_call(
        functools.partial(_ffn_sample_kernel, seq=seq),
        grid=(n // nrows,),
        in_specs=[rows(a) for a in acts] + [full(c) for c in consts],
        out_specs=[
            pl.BlockSpec((nrows, d), lambda i: (i, 0)),
            pl.BlockSpec((nrows, d_ff), lambda i: (i, 0)),
        ],
        out_shape=[jax.ShapeDtypeStruct((n, d), F32), jax.ShapeDtypeStruct((n, d_ff), F32)],
        compiler_params=_cparams(("parallel",)),
        name="mix_conv_ffn_sample",
    )(*acts, *consts)
    conv_new = ug.reshape(batch, seq, d_ff)[:, seq - (CONV_W - 1):]
    return y, conv_new


def _row_tile(n, want):
    t = min(want, n)
    while n % t:
        t //= 2
    return t


def _layer(x, is_prompt, state, rel_bias, lw, final_g):
    batch, seq, d = x.shape
    n = batch * seq
    x2d = x.reshape(n, d)
    n_qkv = Q_W + 2 * KV_W
    n_rw = lw["mu_shift"].shape[0]
    tm = _row_tile(n, 1024)
    qkv, rw, gates = _in_proj(x2d, lw["norm1_g"], lw["w_in"], n_qkv, n_rw, tm)
    kv = qkv.reshape(batch, seq, n_qkv)
    k_new = kv[:, :, Q_W:Q_W + KV_W]
    v_new = kv[:, :, Q_W + KV_W:]
    if is_prompt:
        att = _attention_prompt(qkv, batch, seq, rel_bias, lw["sinks"])
        wb = state["win_buf"]
        k_rows, v_rows = k_new[:, seq - wb:], v_new[:, seq - wb:]
        shift0 = wkv0 = None
    else:
        cache_k, cache_v = state["cache_k"], state["cache_v"]
        wb = cache_k.shape[1]
        assert seq <= wb
        att, k_rows, v_rows = _attention_sample(qkv, batch, seq, cache_k, cache_v, rel_bias, lw["sinks"],
                                                _row_tile(batch, 16))
        shift0, wkv0 = state["shift"], state["wkv"]
    k_rows = k_rows.reshape(batch, -1, N_KV, HEAD_DIM)
    v_rows = v_rows.reshape(batch, -1, N_KV, HEAD_DIM)
    rwo, wkv_new = _rwkv_mixer(rw, batch, seq, shift0, wkv0, lw["mu_shift"], lw["w0"], lw["w2"], lw["a0"],
                               lw["a2"], lw["g2"], lw["k_k"], lw["k_a"], lw["r_k"], lw["lnx_g"], lw["lnx_b"])
    shift_new = rw.reshape(batch, seq, n_rw)[:, seq - 1]
    if is_prompt:
        y, conv_new = _ffn_prompt(x2d, att, rwo, gates, batch, seq, lw, final_g, _row_tile(seq, FFN_ROWS))
    else:
        y, conv_new = _ffn_sample(x2d, att, rwo, gates, batch, seq, state["conv"], lw, final_g,
                                  _row_tile(batch, FFN_ROWS_WITH_STATE // seq))
    return y.reshape(batch, seq, d), (k_rows, v_rows, shift_new, wkv_new, conv_new)


def kernel(x_prompt, x_sample, cache_win_k, cache_win_v, state_shift, state_wkv, state_conv, rel_bias, norm1_g,
           w_in, sinks, mu_shift, w0, w2, a0, a2, g2, k_k, k_a, r_k, lnx_g, lnx_b, w_pa, w_pb, w_o, norm2_g,
           w_up, conv_w, conv_b, w_down, final_g):
    depth = w_in.shape[0]
    assert depth == 1, "the final norm is fused into the layer's last kernel"
    l = 0
    lw = dict(norm1_g=norm1_g[l], w_in=w_in[l].astype(BF16), sinks=sinks[l], mu_shift=mu_shift[l], w0=w0[l],
              w2=w2[l], a0=a0[l], a2=a2[l], g2=g2[l], k_k=k_k[l], k_a=k_a[l], r_k=r_k[l].reshape(-1),
              lnx_g=lnx_g[l], lnx_b=lnx_b[l], w_pa=w_pa[l].astype(BF16), w_pb=w_pb[l].astype(BF16),
              w_o=w_o[l].astype(BF16), norm2_g=norm2_g[l], w_up=w_up[l].astype(BF16), conv_w=conv_w[l],
              conv_b=conv_b[l], w_down=w_down[l].astype(BF16))
    win_buf = cache_win_k.shape[2]
    y_p, st_p = _layer(x_prompt, True, dict(win_buf=win_buf), rel_bias, lw, final_g)
    y_s, st_s = _layer(x_sample, False,
                       dict(cache_k=cache_win_k[l], cache_v=cache_win_v[l], shift=state_shift[l],
                            wkv=state_wkv[l], conv=state_conv[l]), rel_bias, lw, final_g)
    stack = lambda t: t[None]
    return (y_p, y_s) + tuple(stack(t) for t in st_p) + tuple(stack(t) for t in st_s)
```

```python
import functools
import math

import numpy as np
import jax
import jax.numpy as jnp
from jax import lax
from jax.experimental import pallas as pl
from jax.experimental.pallas import tpu as pltpu

F32 = jnp.float32
BF16 = jnp.bfloat16

HEAD_DIM = 64
N_HEADS = 8
N_KV = 2
WINDOW = 128
N_BUCKETS = 32
MAX_EXACT = N_BUCKETS // 2
REL_MAX_DIST = 128
RW_N = 64
RW_HEADS = 8
RW = RW_HEADS * RW_N
NORM_EPS = 1e-6
GN_EPS = 64e-5
NEG = -1e30
CONV_W = 3

Q_W = N_HEADS * HEAD_DIM
KV_W = N_KV * HEAD_DIM
LANES = 128
SUBLANES = 8
MXU_WIDTH = 256
CHUNK = 64
VMEM_LIMIT = 56 * 1024 * 1024
FFN_COL_CHUNKS = 3
FFN_ROWS = 512
FFN_ROWS_WITH_STATE = 256
ATTN_BLOCKS = 8
WKV_UNITS = 8
WKV_UNITS_WITH_STATE = 4


def _resident(shape):
    return pl.BlockSpec(shape, lambda *_: (0,) * len(shape), pipeline_mode=pl.Buffered(1))


def _cparams(sem):
    return pltpu.CompilerParams(dimension_semantics=sem, vmem_limit_bytes=VMEM_LIMIT)


def _sigmoid(x):
    return 1.0 / (1.0 + jnp.exp(-x))


def _dg(a, b, kind):
    if kind == "nn":
        dn = (((1,), (0,)), ((), ()))
    else:
        dn = (((1,), (1,)), ((), ()))
    return lax.dot_general(a, b, dn, preferred_element_type=F32)


def _split(x):
    hi = x.astype(BF16)
    lo = (x - hi.astype(F32)).astype(BF16)
    return hi, lo


def _mm(a, b, kind="nn", passes=1):
    if passes == 1:
        return _dg(a.astype(BF16), b.astype(BF16), kind)
    ah, al = _split(a)
    bh, bl = _split(b)
    return _dg(ah, bh, kind) + (_dg(ah, bl, kind) + _dg(al, bh, kind))


def _mm_exact_lhs(a_bf16, b, n_terms):
    out = None
    rem = b
    for _ in range(n_terms):
        piece = rem.astype(BF16)
        term = _dg(a_bf16, piece, "nn")
        out = term if out is None else out + term
        rem = rem - piece.astype(F32)
    return out


def _mm_exact_rhs(a, b_bf16, n_terms):
    out = None
    rem = a
    for _ in range(n_terms):
        piece = rem.astype(BF16)
        term = _dg(piece, b_bf16, "nn")
        out = term if out is None else out + term
        rem = rem - piece.astype(F32)
    return out


def _inproj_kernel(x_ref, g_ref, w_ref, qkv_ref, rw_ref, gate_ref, *, n_qkv, n_rw):
    x = x_ref[...]
    ms = jnp.mean(x * x, axis=-1, keepdims=True)
    h = (x * lax.rsqrt(ms + NORM_EPS) * g_ref[...]).astype(BF16)
    n_gate = gate_ref.shape[1]
    step = 2 * MXU_WIDTH
    plain = [(qkv_ref, c, c, min(step, n_qkv - c)) for c in range(0, n_qkv, step)]
    plain += [(rw_ref, c, n_qkv + c, min(step, n_rw - c)) for c in range(0, n_rw, step)]
    gated = [(gate_ref, c, n_qkv + n_rw + c, min(step, n_gate - c)) for c in range(0, n_gate, step)]
    pieces = []
    while plain or gated:
        pieces += [gated.pop(0)] if gated else []
        pieces += [plain.pop(0)] if plain else []
    dot = lambda p: jnp.dot(h, w_ref[:, p[2]:p[2] + p[3]], preferred_element_type=F32)
    pending = dot(pieces[0])
    for j, (ref, c0, _, width) in enumerate(pieces):
        out = pending
        if j + 1 < len(pieces):
            pending = dot(pieces[j + 1])
        if ref is gate_ref:
            out = _sigmoid(out)
        ref[:, c0:c0 + width] = out.astype(ref.dtype)


def _in_proj(x2d, g, w_bf16, n_qkv, n_rw, tm):
    n, d = x2d.shape
    n_gate = w_bf16.shape[1] - n_qkv - n_rw
    return pl.pallas_call(
        functools.partial(_inproj_kernel, n_qkv=n_qkv, n_rw=n_rw),
        grid=(n // tm,),
        in_specs=[
            pl.BlockSpec((tm, d), lambda i: (i, 0)),
            _resident((1, d)),
            _resident(w_bf16.shape),
        ],
        out_specs=[
            pl.BlockSpec((tm, n_qkv), lambda i: (i, 0)),
            pl.BlockSpec((tm, n_rw), lambda i: (i, 0)),
            pl.BlockSpec((tm, n_gate), lambda i: (i, 0)),
        ],
        out_shape=[
            jax.ShapeDtypeStruct((n, n_qkv), F32),
            jax.ShapeDtypeStruct((n, n_rw), F32),
            jax.ShapeDtypeStruct((n, n_gate), BF16),
        ],
        compiler_params=_cparams(("parallel",)),
        name="in_proj",
    )(x2d, g.reshape(1, d), w_bf16)


def _t5_bucket_np(dist):
    n = np.maximum(dist, 0)
    nf = np.maximum(n, 1).astype(np.float32)
    large = MAX_EXACT + (np.log(nf / MAX_EXACT) / math.log(REL_MAX_DIST / MAX_EXACT)
                         * (N_BUCKETS - MAX_EXACT)).astype(np.int32)
    return np.where(n < MAX_EXACT, n, np.minimum(large, N_BUCKETS - 1)).astype(np.int32)


def _attn_kernel(q_ref, k1_ref, k2_ref, v1_ref, v2_ref, bucket_ref, relb_ref, sink_ref, o_ref, *rest,
                 nq, nk, n_blocks, first_block_axis, emit_window):
    bt = q_ref.shape[0]
    bias_ref = rest[-1]
    if emit_window:
        n_old, n_new = k1_ref.shape[1], k2_ref.shape[1]
        for w_ref, old_ref, new_ref in ((rest[0], k1_ref, k2_ref), (rest[1], v1_ref, v2_ref)):
            w_ref[:, :n_old - n_new] = old_ref[:, n_new:]
            w_ref[:, n_old - n_new:] = new_ref[...]
    first_step = pl.program_id(0) == 0
    if first_block_axis is not None:
        first_step = jnp.logical_and(first_step, pl.program_id(1) == 0)

    @pl.when(first_step)
    def _():
        bucket = bucket_ref[...]
        prev_key = lax.broadcasted_iota(jnp.int32, (1, nk), 1) < (nk // 2)
        for n in range(N_HEADS):
            acc = jnp.full((nq, nk), NEG, F32)
            for b in range(N_BUCKETS):
                acc = jnp.where(bucket == b, relb_ref[b, n], acc)
            c, half = divmod(n, 2)
            bias_ref[0, c, :, half * nk:(half + 1) * nk] = acc
            if first_block_axis is not None:
                bias_ref[1, c, :, half * nk:(half + 1) * nk] = jnp.where(prev_key, NEG, acc)

    n_keys = nk + (n_blocks - 1) * nq

    def padded(a_ref, b_ref):
        parts = [a_ref[...], b_ref[...]]
        n_now = a_ref.shape[1] + b_ref.shape[1]
        if n_now < n_keys:
            parts.append(jnp.zeros((bt, n_keys - n_now, LANES), F32))
        return jnp.concatenate(parts, axis=1)

    kk = padded(k1_ref, k2_ref)
    vv = padded(v1_ref, v2_ref)
    kk_r = pltpu.roll(kk, HEAD_DIM, 2)
    vv_r = pltpu.roll(vv, HEAD_DIM, 2)
    lane = lax.broadcasted_iota(jnp.int32, (1, 1, LANES), 2)
    lo = lane < HEAD_DIM

    def halves(x, x_r, kvh):
        src_lo, src_hi = (x, x_r) if kvh == 0 else (x_r, x)
        even = jnp.where(lo, src_lo, 0.0).astype(BF16)
        odd = jnp.where(lo, 0.0, src_hi).astype(BF16)
        return even, odd

    k_eo = [halves(kk, kk_r, h) for h in range(N_KV)]
    v_eo = [halves(vv, vv_r, h) for h in range(N_KV)]

    def window(eo, j):
        return jnp.concatenate([eo[0][:, j * nq:j * nq + nk], eo[1][:, j * nq:j * nq + nk]], axis=1)

    n_cols = N_HEADS // 2
    units = [(j, c) for j in range(n_blocks) for c in range(n_cols)]
    kvh_of = lambda c: (2 * c) // (N_HEADS // N_KV)
    k_win = {(j, h): window(k_eo[h], j) for j in range(n_blocks) for h in range(N_KV)}
    v_win = {(j, h): window(v_eo[h], j) for j in range(n_blocks) for h in range(N_KV)}
    sums_on_mxu = bt == 1
    if sums_on_mxu:
        key_row = lax.broadcasted_iota(jnp.int32, (1, 2 * nk, 1), 1)
        ones_cols = jnp.where((key_row < nk) == lo, 1.0, 0.0).astype(BF16)
        v_win = {jh: jnp.concatenate([v, ones_cols], axis=2) for jh, v in v_win.items()}

    scale = HEAD_DIM ** -0.5
    first_table = 0
    if first_block_axis is not None:
        first_table = jnp.where(pl.program_id(first_block_axis) == 0, 1, 0)
    s_all = {}
    for j, c in units:
        qc = (q_ref[:, j * nq:(j + 1) * nq, c * LANES:(c + 1) * LANES] * scale).astype(BF16)
        s_all[j, c] = jnp.einsum("bqd,bkd->bqk", qc, k_win[j, kvh_of(c)], preferred_element_type=F32)
    e_all, sink_all = {}, {}
    for j, c in units:
        s = s_all[j, c] + bias_ref[first_table if j == 0 else 0, c][None]
        es, sink_terms = [], []
        for half in range(2):
            sh = s[:, :, half * nk:(half + 1) * nk]
            sink = sink_ref[2 * c + half]
            m = jnp.maximum(jnp.max(sh, axis=-1, keepdims=True), sink)
            e = jnp.exp(sh - m)
            es.append(e.astype(BF16))
            sink_terms.append(jnp.exp(sink - m) + (0.0 if sums_on_mxu else jnp.sum(e, axis=-1, keepdims=True)))
        e_all[j, c] = jnp.concatenate(es, axis=2)
        sink_all[j, c] = jnp.where(lo, sink_terms[0], sink_terms[1])
    for j, c in units:
        o = jnp.einsum("bqk,bkd->bqd", e_all[j, c], v_win[j, kvh_of(c)], preferred_element_type=F32)
        denom = sink_all[j, c] + (o[:, :, LANES:] if sums_on_mxu else 0.0)
        o = (o[:, :, :LANES] / denom)
        if len(o_ref.shape) == 2:
            o_ref[:, c * LANES:(c + 1) * LANES] = o.reshape(bt * nq, LANES).astype(o_ref.dtype)
        else:
            o_ref[:, j * nq:(j + 1) * nq, c * LANES:(c + 1) * LANES] = o.astype(o_ref.dtype)


def _attention_prompt(qkv, batch, seq, rel_bias, sinks):
    nblk = seq // WINDOW
    nb = ATTN_BLOCKS if nblk % ATTN_BLOCKS == 0 else 1
    nsteps = nblk // nb
    width = qkv.shape[1]
    q_blk = qkv.reshape(batch * nblk, WINDOW, width)
    q_step = qkv.reshape(batch * nsteps, nb * WINDOW, width)
    kcol = Q_W // LANES
    vcol = (Q_W + KV_W) // LANES
    nk = 2 * WINDOW
    qi = np.arange(WINDOW)[:, None] + WINDOW
    kj = np.arange(nk)[None, :]
    dist = qi - kj
    bucket = np.where((dist >= 0) & (dist < WINDOW), _t5_bucket_np(dist), -1).astype(np.int32)
    cur = lambda c: (lambda b, i: (b * nsteps + i, 0, c))
    prev = lambda c: (lambda b, i: (b * nblk + jnp.maximum(i * nb - 1, 0), 0, c))
    out = pl.pallas_call(
        functools.partial(_attn_kernel, nq=WINDOW, nk=nk, n_blocks=nb, first_block_axis=1, emit_window=False),
        grid=(batch, nsteps),
        in_specs=[
            pl.BlockSpec((1, nb * WINDOW, Q_W), cur(0)),
            pl.BlockSpec((1, WINDOW, LANES), prev(kcol)),
            pl.BlockSpec((1, nb * WINDOW, LANES), cur(kcol)),
            pl.BlockSpec((1, WINDOW, LANES), prev(vcol)),
            pl.BlockSpec((1, nb * WINDOW, LANES), cur(vcol)),
            _resident(bucket.shape),
            pl.BlockSpec(memory_space=pltpu.SMEM),
            pl.BlockSpec(memory_space=pltpu.SMEM),
        ],
        out_specs=pl.BlockSpec((1, nb * WINDOW, Q_W), cur(0)),
        out_shape=jax.ShapeDtypeStruct((batch * nsteps, nb * WINDOW, Q_W), BF16),
        scratch_shapes=[pltpu.VMEM((2, N_HEADS // 2, WINDOW, 2 * nk), F32)],
        compiler_params=_cparams(("arbitrary", "arbitrary")),
        name="attn_prompt",
    )(q_step, q_blk, q_step, q_blk, q_step, jnp.asarray(bucket), rel_bias, sinks)
    return out.reshape(batch * seq, Q_W)


def _attention_sample(qkv, batch, seq, cache_k, cache_v, rel_bias, sinks, bt):
    wb = cache_k.shape[1]
    nk = 2 * WINDOW
    q3 = qkv.reshape(batch, seq, qkv.shape[1])
    ck = cache_k.reshape(batch, wb, KV_W)
    cv = cache_v.reshape(batch, wb, KV_W)
    kcol = Q_W // LANES
    vcol = (Q_W + KV_W) // LANES
    tq = np.arange(seq)[:, None]
    j = np.arange(nk)[None, :]
    dist = np.where(j < wb, tq + wb - j, tq - (j - wb))
    ok = (dist >= 0) & (dist < WINDOW) & (j < wb + seq)
    bucket = np.where(ok, _t5_bucket_np(dist), -1).astype(np.int32)
    win_spec = pl.BlockSpec((bt, wb, LANES), lambda b: (b, 0, 0))
    win_shape = jax.ShapeDtypeStruct((batch, wb, KV_W), F32)
    out, win_k, win_v = pl.pallas_call(
        functools.partial(_attn_kernel, nq=seq, nk=nk, n_blocks=1, first_block_axis=None, emit_window=True),
        grid=(batch // bt,),
        in_specs=[
            pl.BlockSpec((bt, seq, Q_W), lambda b: (b, 0, 0)),
            pl.BlockSpec((bt, wb, LANES), lambda b: (b, 0, 0)),
            pl.BlockSpec((bt, seq, LANES), lambda b: (b, 0, kcol)),
            pl.BlockSpec((bt, wb, LANES), lambda b: (b, 0, 0)),
            pl.BlockSpec((bt, seq, LANES), lambda b: (b, 0, vcol)),
            _resident(bucket.shape),
            pl.BlockSpec(memory_space=pltpu.SMEM),
            pl.BlockSpec(memory_space=pltpu.SMEM),
        ],
        out_specs=[pl.BlockSpec((bt * seq, Q_W), lambda b: (b, 0)), win_spec, win_spec],
        out_shape=[jax.ShapeDtypeStruct((batch * seq, Q_W), BF16), win_shape, win_shape],
        scratch_shapes=[pltpu.VMEM((1, N_HEADS // 2, seq, 2 * nk), F32)],
        compiler_params=_cparams(("arbitrary",)),
        name="attn_sample",
    )(q3, ck, q3, cv, q3, jnp.asarray(bucket), rel_bias, sinks)
    return out, win_k, win_v


P_SCORE = 1
P_TINV = 1
P_INTRA = 1
P_STATE = 1


def _wkv_kernel(*refs, n_units, n_seg, has_state, lora_w):
    C = CHUNK
    seg_len = C // n_seg
    n_pairs = RW // LANES
    it = iter(refs)
    p_ref = next(it)
    p0_ref, s0_ref = (next(it), next(it)) if has_state else (None, None)
    (mu_ref, w2a2_ref, w0_ref, a0_ref, g2_ref, kk_ref, ka_ref, rk_ref, lng_ref, lnb_ref, ltri_ref) = (
        next(it) for _ in range(11))
    elast_ref = next(it) if n_seg > 1 else None
    ones_ref, out_ref, sout_ref = next(it), next(it), next(it)
    last_ref, sbd_ref = (None, None) if has_state else (next(it), next(it))

    if not has_state:
        @pl.when(pl.program_id(1) == 0)
        def _():
            last_ref[...] = jnp.zeros_like(last_ref)
            sbd_ref[...] = jnp.zeros_like(sbd_ref)

    row = lax.broadcasted_iota(jnp.int32, (C, 1), 0)
    rows_of = lambda u: slice(u * C, (u + 1) * C)
    lane = lax.broadcasted_iota(jnp.int32, (1, LANES), 1)
    lo = lane < RW_N
    ones_bd = ones_ref[...]

    def headsum(x):
        xb = x.astype(BF16)
        return jnp.concatenate([_dg(xb[:, c:c + MXU_WIDTH], ones_bd, "nn") for c in range(0, RW, MXU_WIDTH)],
                               axis=1)

    def prepare(us):
        xs_parts = []
        for u in us:
            p = p_ref[u, 0]
            rolled = pltpu.roll(p, 1, 0)
            if has_state:
                prev = jnp.where(row % seg_len == 0, p0_ref[u, 0], rolled)
            else:
                prev = jnp.where(row == 0, last_ref[u], rolled)
                last_ref[u] = p_ref[u, 0, C - 1:C, :]
            xs_parts.append(p + mu_ref[...] * (prev - p))
        xs = jnp.concatenate(xs_parts, axis=0)
        r = xs[:, 0:RW]
        k = xs[:, RW:2 * RW]
        v = xs[:, 2 * RW:3 * RW]
        lwla = xs[:, 3 * RW:3 * RW + LANES]
        lg = xs[:, 3 * RW + LANES:3 * RW + 2 * LANES]
        lwla = jnp.where(lane < lora_w, jnp.tanh(lwla), lwla)
        wa = jnp.dot(lwla.astype(BF16), w2a2_ref[...], preferred_element_type=F32)
        logw = -math.exp(-0.5) * _sigmoid(w0_ref[...] + wa[:, :RW])
        a_sig = _sigmoid(a0_ref[...] + wa[:, RW:])
        g = jnp.dot(_sigmoid(lg).astype(BF16), g2_ref[...], preferred_element_type=F32)
        kk = k * kk_ref[...]
        kk = kk * (1.0 / jnp.maximum(jnp.sqrt(headsum(kk * kk)), 1e-12))
        k = k * (1.0 + (a_sig - 1.0) * ka_ref[...])
        a = -kk
        b = kk * a_sig
        bonus = headsum(r * k * rk_ref[...]) * v
        cws, cwl = [], []
        for i in range(len(us)):
            cw_u = _mm_exact_lhs(ltri_ref[...], logw[rows_of(i)], 3)
            cws.append(cw_u)
            if n_seg == 1:
                cwl.append(jnp.broadcast_to(cw_u[C - 1:C, :], (C, RW)))
            else:
                cwl.append(_mm_exact_lhs(elast_ref[...], cw_u, 3))
        cw = jnp.concatenate(cws, axis=0)
        cw_last = jnp.concatenate(cwl, axis=0)
        w_inv = jnp.exp(-cw)
        b_t, k_t = b * w_inv, k * w_inv
        w_tail = jnp.exp(cw_last - cw)
        b_h, k_h = b * w_tail, k * w_tail
        a_t, r_t = a * jnp.exp(cw - logw), r * jnp.exp(cw)
        w_c = jnp.exp(cw_last)
        return a_t, r_t, b_t, k_t, b_h, k_h, v, w_c, bonus, g

    def bd(y):
        return jnp.concatenate([jnp.where(lo, y, 0.0), jnp.where(lo, 0.0, y)], axis=0)

    zeros_head = jnp.zeros((RW_N, RW_N), F32)

    def pack_pair(s_even, s_odd):
        return jnp.concatenate([jnp.concatenate([s_even, zeros_head], axis=1),
                                jnp.concatenate([zeros_head, s_odd], axis=1)], axis=0)

    def store_pair(ref, i, q, s_pair):
        ref[i, 2 * q] = s_pair[:RW_N, :RW_N]
        ref[i, 2 * q + 1] = s_pair[RW_N:, RW_N:]

    s_idx = lane % C
    strict = s_idx < row
    incl = s_idx <= row
    if n_seg > 1:
        same_seg = (s_idx // seg_len) == (row // seg_len)
        strict = jnp.logical_and(strict, same_seg)
        incl = jnp.logical_and(incl, same_seg)
    row2 = lax.broadcasted_iota(jnp.int32, (2 * C, 1), 0)
    same_head = (row2 < RW_N) == lo

    part = lambda x, i, q: x[i * C:(i + 1) * C, q * LANES:(q + 1) * LANES]

    def recurrence(us, ops):
        a_t, r_t, b_t, k_t, b_h, k_h, v, w_c = ops[:8]
        units = [(i, q) for i in range(len(us)) for q in range(n_pairs)]
        sc = {uq: _mm(jnp.concatenate([part(a_t, *uq), part(r_t, *uq)], axis=0),
                      jnp.concatenate([bd(part(b_t, *uq)), bd(part(k_t, *uq))], axis=0),
                      "nt", P_SCORE) for uq in units}
        pw = {uq: jnp.where(strict, sc[uq][:C, :LANES], 0.0) for uq in units}
        m_rb = {uq: jnp.where(incl, sc[uq][C:, :LANES], 0.0) for uq in units}
        lm_v = {uq: _mm(jnp.concatenate([jnp.where(strict, sc[uq][:C, LANES:], 0.0),
                                         jnp.where(incl, sc[uq][C:, LANES:], 0.0)], axis=0),
                        bd(part(v, *uq)), "nn", P_INTRA) for uq in units}
        tinv = {uq: pw[uq] + jnp.where(s_idx == row, 1.0, 0.0) for uq in units}
        n_lvl = int(math.log2(seg_len))
        for lvl in range(1, n_lvl):
            last = lvl + 1 == n_lvl
            if lvl == 1:
                for uq in units:
                    pw[uq] = _mm(pw[uq], bd(pw[uq]), "nn", P_TINV)
            for uq in units:
                rhs = bd(tinv[uq]) if last else jnp.concatenate([bd(tinv[uq]), bd(pw[uq])], axis=1)
                upd = _mm(pw[uq], rhs, "nn", P_TINV)
                tinv[uq] = tinv[uq] + upd[:, :LANES]
                if not last:
                    pw[uq] = upd[:, LANES:]
        x = {uq: _mm(tinv[uq], jnp.concatenate([bd(part(a_t, *uq)), bd(lm_v[uq][:C])], axis=1), "nn", P_TINV)
             for uq in units}
        a_hat = {uq: x[uq][:, :LANES] for uq in units}
        v_hat = {uq: x[uq][:, LANES:] for uq in units}
        z = {uq: _mm(m_rb[uq], jnp.concatenate([bd(a_hat[uq]), bd(v_hat[uq])], axis=1), "nn", P_INTRA)
             for uq in units}
        r_hat = {uq: part(r_t, *uq) + z[uq][:, :LANES] for uq in units}
        y_intra = {uq: z[uq][:, LANES:] + lm_v[uq][C:] for uq in units}
        ys = {}
        if n_seg == 1:
            s_old = {(i, q): sbd_ref[us[i], q] for i, q in units}
            t1 = {uq: _mm(jnp.concatenate([a_hat[uq], r_hat[uq]], axis=0), s_old[uq], "nt", P_STATE)
                  for uq in units}
            for uq in units:
                ys[uq] = t1[uq][C:] + y_intra[uq]
                uv = jnp.concatenate([t1[uq][:C] + v_hat[uq], part(v, *uq)], axis=0)
                bkh = jnp.concatenate([part(b_h, *uq), part(k_h, *uq)], axis=0)
                ds = _mm(uv.T, bkh, "nn", P_STATE)
                sbd_ref[us[uq[0]], uq[1]] = s_old[uq] * part(w_c, *uq)[0:1] + jnp.where(same_head, ds, 0.0)
        else:
            row_seg = (row2 % C) // seg_len
            for uq in units:
                i, q = uq
                first_seq = us[i] * n_seg
                u_parts, y_parts, s_olds = [], [], []
                for sg in range(n_seg):
                    rs = slice(sg * seg_len, (sg + 1) * seg_len)
                    s_sg = pack_pair(s0_ref[first_seq + sg, 2 * q], s0_ref[first_seq + sg, 2 * q + 1])
                    t1 = _mm(jnp.concatenate([a_hat[uq][rs], r_hat[uq][rs]], axis=0), s_sg, "nt", P_STATE)
                    u_parts.append(t1[:seg_len] + v_hat[uq][rs])
                    y_parts.append(t1[seg_len:] + y_intra[uq][rs])
                    s_olds.append(s_sg)
                ys[uq] = jnp.concatenate(y_parts, axis=0)
                uv_t = jnp.concatenate(u_parts + [part(v, *uq)], axis=0).T
                bkh = jnp.concatenate([part(b_h, *uq), part(k_h, *uq)], axis=0)
                w_cq = part(w_c, *uq)
                for sg in range(n_seg):
                    ds = _mm(uv_t, jnp.where(row_seg == sg, bkh, 0.0), "nn", P_STATE)
                    store_pair(sout_ref, first_seq + sg, q,
                               s_olds[sg] * w_cq[sg * seg_len:sg * seg_len + 1] + jnp.where(same_head, ds, 0.0))
        return ys

    def finish(us, ys, ops):
        bonus, g = ops[8:]
        y = jnp.concatenate([jnp.concatenate([ys[i, q] for q in range(n_pairs)], axis=1)
                             for i in range(len(us))], axis=0)
        mean = headsum(y) * (1.0 / RW_N)
        d = y - mean
        var = headsum(d * d) * (1.0 / RW_N)
        y = d * lax.rsqrt(var + GN_EPS) * lng_ref[...] + lnb_ref[...]
        y = ((y + bonus) * g).astype(out_ref.dtype)
        for i, u in enumerate(us):
            out_ref[u, 0] = y[rows_of(i)]

    all_units = list(range(n_units))
    ops = prepare(all_units)
    finish(all_units, recurrence(all_units, ops), ops)
    units = [(u, q) for u in range(n_units) for q in range(n_pairs)]

    if not has_state:
        @pl.when(pl.program_id(1) == pl.num_programs(1) - 1)
        def _():
            for u, q in units:
                store_pair(sout_ref, u, q, sbd_ref[u, q])


def _rwkv_mixer(rw, batch, seq, shift0, wkv0, mu, w0, w2, a0, a2, g2, k_k, k_a, r_k, lnx_g, lnx_b):
    n_shift = rw.shape[1]
    lora_w, lora_a = w2.shape[0], a2.shape[0]
    assert lora_w + lora_a == LANES and g2.shape[0] == LANES and n_shift == 3 * RW + 2 * LANES
    has_state = shift0 is not None
    C = CHUNK
    n_pairs = RW // LANES
    if has_state:
        assert C % seq == 0 and batch % (C // seq) == 0
        n_seg, n_chunks, n_groups = C // seq, 1, batch * seq // C
    else:
        assert seq % C == 0
        n_seg, n_chunks, n_groups = 1, seq // C, batch
    want = WKV_UNITS_WITH_STATE if has_state else WKV_UNITS
    nu = want if n_groups % want == 0 else 1
    seg_len = C // n_seg
    w2a2 = jnp.zeros((LANES, 2 * RW), F32).at[:lora_w, :RW].set(w2).at[lora_w:, RW:].set(a2).astype(BF16)
    t = np.arange(C)
    same_seg = (t[:, None] // seg_len) == (t[None, :] // seg_len)
    ltri = jnp.asarray(((t[:, None] >= t[None, :]) & same_seg).astype(np.float32), BF16)
    elast = jnp.asarray((t[None, :] == (t[:, None] // seg_len) * seg_len + seg_len - 1).astype(np.float32), BF16)
    ones_bd = jnp.asarray(np.kron(np.eye(MXU_WIDTH // RW_N, dtype=np.float32),
                                  np.ones((RW_N, RW_N), np.float32)), BF16)
    row = lambda x: x.reshape(1, -1).astype(F32)
    rw4 = rw.reshape(n_groups, n_chunks, C, n_shift)
    blk = lambda w: pl.BlockSpec((nu, 1, C, w), lambda i, c: (i, c, 0, 0))
    st_blk = pl.BlockSpec((nu * n_seg, RW_HEADS, RW_N, RW_N), lambda i, c: (i, 0, 0, 0))
    args, specs = [rw4], [blk(n_shift)]
    if has_state:
        p0 = jnp.pad(shift0[:, None, :], ((0, 0), (0, seq - 1), (0, 0))).reshape(n_groups, 1, C, n_shift)
        args += [p0, wkv0]
        specs += [blk(n_shift), st_blk]
    consts = [row(mu), w2a2, row(w0), row(a0), g2.astype(BF16), row(k_k), row(k_a), row(r_k), row(lnx_g),
              row(lnx_b), ltri] + ([elast] if n_seg > 1 else []) + [ones_bd]
    args += consts
    specs += [_resident(c.shape) for c in consts]
    scratch = [] if has_state else [pltpu.VMEM((nu, 1, n_shift), F32),
                                    pltpu.VMEM((nu, n_pairs, LANES, LANES), F32)]
    out, s_new = pl.pallas_call(
        functools.partial(_wkv_kernel, n_units=nu, n_seg=n_seg, has_state=has_state, lora_w=lora_w),
        grid=(n_groups // nu, n_chunks),
        in_specs=specs,
        out_specs=[blk(RW), st_blk],
        out_shape=[
            jax.ShapeDtypeStruct((n_groups, n_chunks, C, RW), BF16),
            jax.ShapeDtypeStruct((batch, RW_HEADS, RW_N, RW_N), F32),
        ],
        scratch_shapes=scratch,
        compiler_params=_cparams(("arbitrary", "arbitrary")),
        name="rwkv7",
    )(*args)
    return out.reshape(batch * seq, RW), s_new


def _branch_mix(x_ref, att_ref, rwo_ref, gate_ref, wpa_ref, wpb_ref, wo_ref):
    d = x_ref.shape[1]
    pa = jnp.dot(att_ref[...], wpa_ref[...], preferred_element_type=F32)
    pb = jnp.dot(rwo_ref[...], wpb_ref[...], preferred_element_type=F32)
    mix = gate_ref[:, :d] * pa + gate_ref[:, d:] * pb
    return x_ref[...] + jnp.dot(mix.astype(BF16), wo_ref[...], preferred_element_type=F32)


def _rms(x, g):
    ms = jnp.mean(x * x, axis=-1, keepdims=True)
    return x * lax.rsqrt(ms + NORM_EPS) * g


def _gelu_tanh(c):
    return c * (0.5 * (1.0 + jnp.tanh(math.sqrt(2.0 / math.pi) * (c + 0.044715 * (c * c * c)))))


def _conv_ffn_cols(h, x, wup_ref, cw_ref, cb_ref, wdown_ref, shifted, ug_sink):
    d_ff = cb_ref.shape[1]
    tiles = -(-d_ff // MXU_WIDTH)
    edges = [min(d_ff, MXU_WIDTH * ((tiles * j + FFN_COL_CHUNKS - 1) // FFN_COL_CHUNKS))
             for j in range(FFN_COL_CHUNKS + 1)]
    spans = list(zip(edges[:-1], edges[1:]))

    def up(span):
        lo_c, hi_c = span
        return (jnp.dot(h, wup_ref[:, lo_c:hi_c], preferred_element_type=F32),
                jnp.dot(h, wup_ref[:, d_ff + lo_c:d_ff + hi_c], preferred_element_type=F32))

    acc = x
    pending = up(spans[0])
    for j, (lo_c, hi_c) in enumerate(spans):
        cs = slice(lo_c, hi_c)
        ug_full, uv_full = pending
        if j + 1 < len(spans):
            pending = up(spans[j + 1])
        ug, ug_m1, ug_m2, uv = shifted(ug_full, uv_full, cs)
        c = cb_ref[:, cs] + cw_ref[0:1, cs] * ug_m2 + cw_ref[1:2, cs] * ug_m1 + cw_ref[2:3, cs] * ug
        act = (_gelu_tanh(c) * uv).astype(BF16)
        acc = acc + jnp.dot(act, wdown_ref[lo_c:hi_c, :], preferred_element_type=F32)
        ug_sink(ug, cs)
    return acc


def _ffn_prompt_kernel(x_ref, att_ref, rwo_ref, gate_ref, wpa_ref, wpb_ref, wo_ref, g2_ref, wup_ref, cw_ref,
                       cb_ref, wdown_ref, gf_ref, y_ref, ug_ref, carry_ref):
    tm = x_ref.shape[0]
    x = _branch_mix(x_ref, att_ref, rwo_ref, gate_ref, wpa_ref, wpb_ref, wo_ref)
    h = _rms(x, g2_ref[...]).astype(BF16)
    seq_start = pl.program_id(1) == 0

    def shifted(ug, uv, cs):
        before = jnp.where(seq_start, 0.0, carry_ref[:, cs])
        carry_ref[:, cs] = ug[tm - SUBLANES:]
        ug_e = jnp.concatenate([before, ug], axis=0)
        return ug, pltpu.roll(ug_e, 1, 0)[SUBLANES:], pltpu.roll(ug_e, 2, 0)[SUBLANES:], uv

    def ug_sink(ug, cs):
        ug_ref[:, cs] = ug[tm - SUBLANES:]

    x2 = _conv_ffn_cols(h, x, wup_ref, cw_ref, cb_ref, wdown_ref, shifted, ug_sink)
    y_ref[...] = _rms(x2, gf_ref[...])


def _ffn_sample_kernel(x_ref, att_ref, rwo_ref, gate_ref, e_ref, wpa_ref, wpb_ref, wo_ref, g2_ref, wup_ref,
                       cw_ref, cb_ref, wdown_ref, gf_ref, y_ref, ug_ref, *, seq):
    rows = x_ref.shape[0]
    x = _branch_mix(x_ref, att_ref, rwo_ref, gate_ref, wpa_ref, wpb_ref, wo_ref)
    h = _rms(x, g2_ref[...]).astype(BF16)
    t = lax.broadcasted_iota(jnp.int32, (rows, 1), 0) % seq

    def shifted(ug, uv, cs):
        e = e_ref[:, cs]
        ug_m1 = jnp.where(t == 0, pltpu.roll(e, rows - 1, 0), pltpu.roll(ug, 1, 0))
        ug_m2 = jnp.where(t < 2, e, pltpu.roll(ug, 2, 0))
        return ug, ug_m1, ug_m2, uv

    def ug_sink(ug, cs):
        ug_ref[:, cs] = ug

    x2 = _conv_ffn_cols(h, x, wup_ref, cw_ref, cb_ref, wdown_ref, shifted, ug_sink)
    y_ref[...] = _rms(x2, gf_ref[...])


def _ffn_prompt(x2d, att, rwo, gates, batch, seq, lw, final_g, tm):
    n, d = x2d.shape
    d_ff = lw["conv_b"].shape[0]
    nt = seq // tm
    rows = lambda a: pl.BlockSpec((tm, a.shape[1]), lambda b, i: (b * nt + i, 0))
    full = lambda a: _resident(a.shape)
    consts = [lw["w_pa"], lw["w_pb"], lw["w_o"], lw["norm2_g"].reshape(1, d), lw["w_up"], lw["conv_w"],
              lw["conv_b"].reshape(1, d_ff), lw["w_down"], final_g.reshape(1, d)]
    acts = [x2d, att, rwo, gates]
    y, ug_last = pl.pallas_call(
        _ffn_prompt_kernel,
        grid=(batch, nt),
        in_specs=[rows(a) for a in acts] + [full(c) for c in consts],
        out_specs=[
            pl.BlockSpec((tm, d), lambda b, i: (b * nt + i, 0)),
            pl.BlockSpec((SUBLANES, d_ff), lambda b, i: (b, 0)),
        ],
        out_shape=[
            jax.ShapeDtypeStruct((n, d), F32),
            jax.ShapeDtypeStruct((batch * SUBLANES, d_ff), F32),
        ],
        scratch_shapes=[pltpu.VMEM((SUBLANES, d_ff), F32)],
        compiler_params=_cparams(("arbitrary", "arbitrary")),
        name="mix_conv_ffn_prompt",
    )(*acts, *consts)
    conv_new = ug_last.reshape(batch, SUBLANES, d_ff)[:, SUBLANES - (CONV_W - 1):]
    return y, conv_new


def _ffn_sample(x2d, att, rwo, gates, batch, seq, conv0, lw, final_g, bt):
    n, d = x2d.shape
    d_ff = lw["conv_b"].shape[0]
    nrows = bt * seq
    e = jnp.pad(conv0, ((0, 0), (0, seq - (CONV_W - 1)), (0, 0))).reshape(n, d_ff)
    rows = lambda a: pl.BlockSpec((nrows, a.shape[1]), lambda i: (i, 0))
    full = lambda a: _resident(a.shape)
    consts = [lw["w_pa"], lw["w_pb"], lw["w_o"], lw["norm2_g"].reshape(1, d), lw["w_up"], lw["conv_w"],
              lw["conv_b"].reshape(1, d_ff), lw["w_down"], final_g.reshape(1, d)]
    acts = [x2d, att, rwo, gates, e]
    y, ug = pl.pallas_call(
        functools.partial(_ffn_sample_kernel, seq=seq),
        grid=(n // nrows,),
        in_specs=[rows(a) for a in acts] + [full(c) for c in consts],
        out_specs=[
            pl.BlockSpec((nrows, d), lambda i: (i, 0)),
            pl.BlockSpec((nrows, d_ff), lambda i: (i, 0)),
        ],
        out_shape=[jax.ShapeDtypeStruct((n, d), F32), jax.ShapeDtypeStruct((n, d_ff), F32)],
        compiler_params=_cparams(("parallel",)),
        name="mix_conv_ffn_sample",
    )(*acts, *consts)
    conv_new = ug.reshape(batch, seq, d_ff)[:, seq - (CONV_W - 1):]
    return y, conv_new


def _row_tile(n, want):
    t = min(want, n)
    while n % t:
        t //= 2
    return t


def _layer(x, is_prompt, state, rel_bias, lw, final_g):
    batch, seq, d = x.shape
    n = batch * seq
    x2d = x.reshape(n, d)
    n_qkv = Q_W + 2 * KV_W
    n_rw = lw["mu_shift"].shape[0]
    tm = _row_tile(n, min(1024, n // 4))
    qkv, rw, gates = _in_proj(x2d, lw["norm1_g"], lw["w_in"], n_qkv, n_rw, tm)
    kv = qkv.reshape(batch, seq, n_qkv)
    k_new = kv[:, :, Q_W:Q_W + KV_W]
    v_new = kv[:, :, Q_W + KV_W:]
    if is_prompt:
        att = _attention_prompt(qkv, batch, seq, rel_bias, lw["sinks"])
        wb = state["win_buf"]
        k_rows, v_rows = k_new[:, seq - wb:], v_new[:, seq - wb:]
        shift0 = wkv0 = None
    else:
        cache_k, cache_v = state["cache_k"], state["cache_v"]
        wb = cache_k.shape[1]
        assert seq <= wb
        att, k_rows, v_rows = _attention_sample(qkv, batch, seq, cache_k, cache_v, rel_bias, lw["sinks"],
                                                _row_tile(batch, 16))
        shift0, wkv0 = state["shift"], state["wkv"]
    k_rows = k_rows.reshape(batch, -1, N_KV, HEAD_DIM)
    v_rows = v_rows.reshape(batch, -1, N_KV, HEAD_DIM)
    rwo, wkv_new = _rwkv_mixer(rw, batch, seq, shift0, wkv0, lw["mu_shift"], lw["w0"], lw["w2"], lw["a0"],
                               lw["a2"], lw["g2"], lw["k_k"], lw["k_a"], lw["r_k"], lw["lnx_g"], lw["lnx_b"])
    shift_new = rw.reshape(batch, seq, n_rw)[:, seq - 1]
    if is_prompt:
        y, conv_new = _ffn_prompt(x2d, att, rwo, gates, batch, seq, lw, final_g, _row_tile(seq, FFN_ROWS))
    else:
        y, conv_new = _ffn_sample(x2d, att, rwo, gates, batch, seq, state["conv"], lw, final_g,
                                  _row_tile(batch, FFN_ROWS_WITH_STATE // seq))
    return y.reshape(batch, seq, d), (k_rows, v_rows, shift_new, wkv_new, conv_new)


def kernel(x_prompt, x_sample, cache_win_k, cache_win_v, state_shift, state_wkv, state_conv, rel_bias, norm1_g,
           w_in, sinks, mu_shift, w0, w2, a0, a2, g2, k_k, k_a, r_k, lnx_g, lnx_b, w_pa, w_pb, w_o, norm2_g,
           w_up, conv_w, conv_b, w_down, final_g):
    depth = w_in.shape[0]
    assert depth == 1, "the final norm is fused into the layer's last kernel"
    l = 0
    lw = dict(norm1_g=norm1_g[l], w_in=w_in[l].astype(BF16), sinks=sinks[l], mu_shift=mu_shift[l], w0=w0[l],
              w2=w2[l], a0=a0[l], a2=a2[l], g2=g2[l], k_k=k_k[l], k_a=k_a[l], r_k=r_k[l].reshape(-1),
              lnx_g=lnx_g[l], lnx_b=lnx_b[l], w_pa=w_pa[l].astype(BF16), w_pb=w_pb[l].astype(BF16),
              w_o=w_o[l].astype(BF16), norm2_g=norm2_g[l], w_up=w_up[l].astype(BF16), conv_w=conv_w[l],
              conv_b=conv_b[l], w_down=w_down[l].astype(BF16))
    win_buf = cache_win_k.shape[2]
    y_p, st_p = _layer(x_prompt, True, dict(win_buf=win_buf), rel_bias, lw, final_g)
    y_s, st_s = _layer(x_sample, False,
                       dict(cache_k=cache_win_k[l], cache_v=cache_win_v[l], shift=state_shift[l],
                            wkv=state_wkv[l], conv=state_conv[l]), rel_bias, lw, final_g)
    stack = lambda t: t[None]
    return (y_p, y_s) + tuple(stack(t) for t in st_p) + tuple(stack(t) for t in st_s)
```

```python
import functools
import math

import numpy as np
import jax
import jax.numpy as jnp
from jax import lax
from jax.experimental import pallas as pl
from jax.experimental.pallas import tpu as pltpu

F32 = jnp.float32
BF16 = jnp.bfloat16

HEAD_DIM = 64
N_HEADS = 8
N_KV = 2
WINDOW = 128
N_BUCKETS = 32
MAX_EXACT = N_BUCKETS // 2
REL_MAX_DIST = 128
RW_N = 64
RW_HEADS = 8
RW = RW_HEADS * RW_N
NORM_EPS = 1e-6
GN_EPS = 64e-5
NEG = -1e30
CONV_W = 3

Q_W = N_HEADS * HEAD_DIM
KV_W = N_KV * HEAD_DIM
LANES = 128
SUBLANES = 8
MXU_WIDTH = 256
CHUNK = 64
VMEM_LIMIT = 56 * 1024 * 1024
FFN_COL_CHUNKS = 3
FFN_ROWS = 512
FFN_ROWS_WITH_STATE = 256
ATTN_BLOCKS = 8
WKV_UNITS = 8
WKV_UNITS_WITH_STATE = 4


def _resident(shape):
    return pl.BlockSpec(shape, lambda *_: (0,) * len(shape), pipeline_mode=pl.Buffered(1))


def _cparams(sem):
    return pltpu.CompilerParams(dimension_semantics=sem, vmem_limit_bytes=VMEM_LIMIT)


def _sigmoid(x):
    return 1.0 / (1.0 + jnp.exp(-x))


def _dg(a, b, kind):
    if kind == "nn":
        dn = (((1,), (0,)), ((), ()))
    else:
        dn = (((1,), (1,)), ((), ()))
    return lax.dot_general(a, b, dn, preferred_element_type=F32)


def _mm(a, b, kind="nn"):
    return _dg(a.astype(BF16), b.astype(BF16), kind)


def _mm_exact_lhs(a_bf16, b, n_terms):
    out = None
    rem = b
    for _ in range(n_terms):
        piece = rem.astype(BF16)
        term = _dg(a_bf16, piece, "nn")
        out = term if out is None else out + term
        rem = rem - piece.astype(F32)
    return out


def _inproj_kernel(x_ref, g_ref, w_ref, qkv_ref, rw_ref, gate_ref, *, n_qkv, n_rw):
    x = x_ref[...]
    ms = jnp.mean(x * x, axis=-1, keepdims=True)
    h = (x * lax.rsqrt(ms + NORM_EPS) * g_ref[...]).astype(BF16)
    n_gate = gate_ref.shape[1]
    step = 2 * MXU_WIDTH
    plain = [(qkv_ref, c, c, min(step, n_qkv - c)) for c in range(0, n_qkv, step)]
    plain += [(rw_ref, c, n_qkv + c, min(step, n_rw - c)) for c in range(0, n_rw, step)]
    gated = [(gate_ref, c, n_qkv + n_rw + c, min(step, n_gate - c)) for c in range(0, n_gate, step)]
    pieces = []
    while plain or gated:
        pieces += [gated.pop(0)] if gated else []
        pieces += [plain.pop(0)] if plain else []
    dot = lambda p: jnp.dot(h, w_ref[:, p[2]:p[2] + p[3]], preferred_element_type=F32)
    pending = dot(pieces[0])
    for j, (ref, c0, _, width) in enumerate(pieces):
        out = pending
        if j + 1 < len(pieces):
            pending = dot(pieces[j + 1])
        if ref is gate_ref:
            out = _sigmoid(out)
        ref[:, c0:c0 + width] = out.astype(ref.dtype)


def _in_proj(x2d, g, w_bf16, n_qkv, n_rw, tm):
    n, d = x2d.shape
    n_gate = w_bf16.shape[1] - n_qkv - n_rw
    return pl.pallas_call(
        functools.partial(_inproj_kernel, n_qkv=n_qkv, n_rw=n_rw),
        grid=(n // tm,),
        in_specs=[
            pl.BlockSpec((tm, d), lambda i: (i, 0)),
            _resident((1, d)),
            _resident(w_bf16.shape),
        ],
        out_specs=[
            pl.BlockSpec((tm, n_qkv), lambda i: (i, 0)),
            pl.BlockSpec((tm, n_rw), lambda i: (i, 0)),
            pl.BlockSpec((tm, n_gate), lambda i: (i, 0)),
        ],
        out_shape=[
            jax.ShapeDtypeStruct((n, n_qkv), F32),
            jax.ShapeDtypeStruct((n, n_rw), F32),
            jax.ShapeDtypeStruct((n, n_gate), BF16),
        ],
        compiler_params=_cparams(("parallel",)),
        name="in_proj",
    )(x2d, g.reshape(1, d), w_bf16)


def _t5_bucket_np(dist):
    n = np.maximum(dist, 0)
    nf = np.maximum(n, 1).astype(np.float32)
    large = MAX_EXACT + (np.log(nf / MAX_EXACT) / math.log(REL_MAX_DIST / MAX_EXACT)
                         * (N_BUCKETS - MAX_EXACT)).astype(np.int32)
    return np.where(n < MAX_EXACT, n, np.minimum(large, N_BUCKETS - 1)).astype(np.int32)


def _attn_kernel(q_ref, k1_ref, k2_ref, v1_ref, v2_ref, bucket_ref, relb_ref, sink_ref, o_ref, *rest,
                 nq, nk, n_blocks, first_block_axis, emit_window):
    bt = q_ref.shape[0]
    bias_ref = rest[-1]
    if emit_window == "tail":
        @pl.when(pl.program_id(1) == pl.num_programs(1) - 1)
        def _():
            n_new, n_win = k2_ref.shape[1], rest[0].shape[1]
            rest[0][...] = k2_ref[:, n_new - n_win:]
            rest[1][...] = v2_ref[:, n_new - n_win:]
    else:
        n_old, n_new = k1_ref.shape[1], k2_ref.shape[1]
        for w_ref, old_ref, new_ref in ((rest[0], k1_ref, k2_ref), (rest[1], v1_ref, v2_ref)):
            w_ref[:, :n_old - n_new] = old_ref[:, n_new:]
            w_ref[:, n_old - n_new:] = new_ref[...]
    first_step = pl.program_id(0) == 0
    if first_block_axis is not None:
        first_step = jnp.logical_and(first_step, pl.program_id(1) == 0)

    @pl.when(first_step)
    def _():
        bucket = bucket_ref[...]
        prev_key = lax.broadcasted_iota(jnp.int32, (1, nk), 1) < (nk // 2)
        for n in range(N_HEADS):
            acc = jnp.full((nq, nk), NEG, F32)
            for b in range(N_BUCKETS):
                acc = jnp.where(bucket == b, relb_ref[b, n], acc)
            c, half = divmod(n, 2)
            bias_ref[0, c, :, half * nk:(half + 1) * nk] = acc
            if first_block_axis is not None:
                bias_ref[1, c, :, half * nk:(half + 1) * nk] = jnp.where(prev_key, NEG, acc)

    n_keys = nk + (n_blocks - 1) * nq

    def padded(a_ref, b_ref):
        parts = [a_ref[...], b_ref[...]]
        n_now = a_ref.shape[1] + b_ref.shape[1]
        if n_now < n_keys:
            parts.append(jnp.zeros((bt, n_keys - n_now, LANES), F32))
        return jnp.concatenate(parts, axis=1)

    kk = padded(k1_ref, k2_ref)
    vv = padded(v1_ref, v2_ref)
    kk_r = pltpu.roll(kk, HEAD_DIM, 2)
    vv_r = pltpu.roll(vv, HEAD_DIM, 2)
    lane = lax.broadcasted_iota(jnp.int32, (1, 1, LANES), 2)
    lo = lane < HEAD_DIM

    def halves(x, x_r, kvh):
        src_lo, src_hi = (x, x_r) if kvh == 0 else (x_r, x)
        even = jnp.where(lo, src_lo, 0.0).astype(BF16)
        odd = jnp.where(lo, 0.0, src_hi).astype(BF16)
        return even, odd

    k_eo = [halves(kk, kk_r, h) for h in range(N_KV)]
    v_eo = [halves(vv, vv_r, h) for h in range(N_KV)]

    def window(eo, j):
        return jnp.concatenate([eo[0][:, j * nq:j * nq + nk], eo[1][:, j * nq:j * nq + nk]], axis=1)

    n_cols = N_HEADS // 2
    units = [(j, c) for j in range(n_blocks) for c in range(n_cols)]
    kvh_of = lambda c: (2 * c) // (N_HEADS // N_KV)
    k_win = {(j, h): window(k_eo[h], j) for j in range(n_blocks) for h in range(N_KV)}
    v_win = {(j, h): window(v_eo[h], j) for j in range(n_blocks) for h in range(N_KV)}
    sums_on_mxu = bt == 1
    if sums_on_mxu:
        key_row = lax.broadcasted_iota(jnp.int32, (1, 2 * nk, 1), 1)
        ones_cols = jnp.where((key_row < nk) == lo, 1.0, 0.0).astype(BF16)
        v_win = {jh: jnp.concatenate([v, ones_cols], axis=2) for jh, v in v_win.items()}

    scale = HEAD_DIM ** -0.5
    first_table = 0
    if first_block_axis is not None:
        first_table = jnp.where(pl.program_id(first_block_axis) == 0, 1, 0)
    s_all = {}
    for j, c in units:
        qc = (q_ref[:, j * nq:(j + 1) * nq, c * LANES:(c + 1) * LANES] * scale).astype(BF16)
        s_all[j, c] = jnp.einsum("bqd,bkd->bqk", qc, k_win[j, kvh_of(c)], preferred_element_type=F32)
    e_all, sink_all = {}, {}
    for j, c in units:
        s = s_all[j, c] + bias_ref[first_table if j == 0 else 0, c][None]
        es, sink_terms = [], []
        for half in range(2):
            sh = s[:, :, half * nk:(half + 1) * nk]
            sink = sink_ref[2 * c + half]
            m = jnp.maximum(jnp.max(sh, axis=-1, keepdims=True), sink)
            e = jnp.exp(sh - m)
            es.append(e.astype(BF16))
            sink_terms.append(jnp.exp(sink - m) + (0.0 if sums_on_mxu else jnp.sum(e, axis=-1, keepdims=True)))
        e_all[j, c] = jnp.concatenate(es, axis=2)
        sink_all[j, c] = jnp.where(lo, sink_terms[0], sink_terms[1])
    for j, c in units:
        o = jnp.einsum("bqk,bkd->bqd", e_all[j, c], v_win[j, kvh_of(c)], preferred_element_type=F32)
        denom = sink_all[j, c] + (o[:, :, LANES:] if sums_on_mxu else 0.0)
        o = (o[:, :, :LANES] / denom)
        if len(o_ref.shape) == 2:
            o_ref[:, c * LANES:(c + 1) * LANES] = o.reshape(bt * nq, LANES).astype(o_ref.dtype)
        else:
            o_ref[:, j * nq:(j + 1) * nq, c * LANES:(c + 1) * LANES] = o.astype(o_ref.dtype)


def _attention_prompt(qkv, batch, seq, rel_bias, sinks, win_buf):
    nblk = seq // WINDOW
    nb = ATTN_BLOCKS if nblk % ATTN_BLOCKS == 0 else 1
    nsteps = nblk // nb
    width = qkv.shape[1]
    q_blk = qkv.reshape(batch * nblk, WINDOW, width)
    q_step = qkv.reshape(batch * nsteps, nb * WINDOW, width)
    kcol = Q_W // LANES
    vcol = (Q_W + KV_W) // LANES
    nk = 2 * WINDOW
    qi = np.arange(WINDOW)[:, None] + WINDOW
    kj = np.arange(nk)[None, :]
    dist = qi - kj
    bucket = np.where((dist >= 0) & (dist < WINDOW), _t5_bucket_np(dist), -1).astype(np.int32)
    cur = lambda c: (lambda b, i: (b * nsteps + i, 0, c))
    prev = lambda c: (lambda b, i: (b * nblk + jnp.maximum(i * nb - 1, 0), 0, c))
    assert win_buf <= nb * WINDOW
    win_spec = pl.BlockSpec((1, win_buf, LANES), lambda b, i: (b, 0, 0))
    win_shape = jax.ShapeDtypeStruct((batch, win_buf, KV_W), F32)
    out, win_k, win_v = pl.pallas_call(
        functools.partial(_attn_kernel, nq=WINDOW, nk=nk, n_blocks=nb, first_block_axis=1, emit_window="tail"),
        grid=(batch, nsteps),
        in_specs=[
            pl.BlockSpec((1, nb * WINDOW, Q_W), cur(0)),
            pl.BlockSpec((1, WINDOW, LANES), prev(kcol)),
            pl.BlockSpec((1, nb * WINDOW, LANES), cur(kcol)),
            pl.BlockSpec((1, WINDOW, LANES), prev(vcol)),
            pl.BlockSpec((1, nb * WINDOW, LANES), cur(vcol)),
            _resident(bucket.shape),
            pl.BlockSpec(memory_space=pltpu.SMEM),
            pl.BlockSpec(memory_space=pltpu.SMEM),
        ],
        out_specs=[pl.BlockSpec((1, nb * WINDOW, Q_W), cur(0)), win_spec, win_spec],
        out_shape=[jax.ShapeDtypeStruct((batch * nsteps, nb * WINDOW, Q_W), BF16), win_shape, win_shape],
        scratch_shapes=[pltpu.VMEM((2, N_HEADS // 2, WINDOW, 2 * nk), F32)],
        compiler_params=_cparams(("arbitrary", "arbitrary")),
        name="attn_prompt",
    )(q_step, q_blk, q_step, q_blk, q_step, jnp.asarray(bucket), rel_bias, sinks)
    return out.reshape(batch * seq, Q_W), win_k, win_v


def _attention_sample(qkv, batch, seq, cache_k, cache_v, rel_bias, sinks, bt):
    wb = cache_k.shape[1]
    nk = 2 * WINDOW
    q3 = qkv.reshape(batch, seq, qkv.shape[1])
    ck = cache_k.reshape(batch, wb, KV_W)
    cv = cache_v.reshape(batch, wb, KV_W)
    kcol = Q_W // LANES
    vcol = (Q_W + KV_W) // LANES
    tq = np.arange(seq)[:, None]
    j = np.arange(nk)[None, :]
    dist = np.where(j < wb, tq + wb - j, tq - (j - wb))
    ok = (dist >= 0) & (dist < WINDOW) & (j < wb + seq)
    bucket = np.where(ok, _t5_bucket_np(dist), -1).astype(np.int32)
    win_spec = pl.BlockSpec((bt, wb, LANES), lambda b: (b, 0, 0))
    win_shape = jax.ShapeDtypeStruct((batch, wb, KV_W), F32)
    out, win_k, win_v = pl.pallas_call(
        functools.partial(_attn_kernel, nq=seq, nk=nk, n_blocks=1, first_block_axis=None, emit_window="shift"),
        grid=(batch // bt,),
        in_specs=[
            pl.BlockSpec((bt, seq, Q_W), lambda b: (b, 0, 0)),
            pl.BlockSpec((bt, wb, LANES), lambda b: (b, 0, 0)),
            pl.BlockSpec((bt, seq, LANES), lambda b: (b, 0, kcol)),
            pl.BlockSpec((bt, wb, LANES), lambda b: (b, 0, 0)),
            pl.BlockSpec((bt, seq, LANES), lambda b: (b, 0, vcol)),
            _resident(bucket.shape),
            pl.BlockSpec(memory_space=pltpu.SMEM),
            pl.BlockSpec(memory_space=pltpu.SMEM),
        ],
        out_specs=[pl.BlockSpec((bt * seq, Q_W), lambda b: (b, 0)), win_spec, win_spec],
        out_shape=[jax.ShapeDtypeStruct((batch * seq, Q_W), BF16), win_shape, win_shape],
        scratch_shapes=[pltpu.VMEM((1, N_HEADS // 2, seq, 2 * nk), F32)],
        compiler_params=_cparams(("arbitrary",)),
        name="attn_sample",
    )(q3, ck, q3, cv, q3, jnp.asarray(bucket), rel_bias, sinks)
    return out, win_k, win_v


def _wkv_kernel(*refs, n_units, n_seg, has_state, lora_w):
    C = CHUNK
    seg_len = C // n_seg
    n_pairs = RW // LANES
    it = iter(refs)
    p_ref = next(it)
    p0_ref, s0_ref = (next(it), next(it)) if has_state else (None, None)
    (mu_ref, w2a2_ref, w0_ref, a0_ref, g2_ref, kk_ref, ka_ref, rk_ref, lng_ref, lnb_ref, ltri_ref) = (
        next(it) for _ in range(11))
    elast_ref = next(it) if n_seg > 1 else None
    ones_ref, out_ref, sout_ref = next(it), next(it), next(it)
    last_ref, sbd_ref = (None, None) if has_state else (next(it), next(it))

    if not has_state:
        @pl.when(pl.program_id(1) == 0)
        def _():
            last_ref[...] = jnp.zeros_like(last_ref)
            sbd_ref[...] = jnp.zeros_like(sbd_ref)

    row = lax.broadcasted_iota(jnp.int32, (C, 1), 0)
    rows_of = lambda u: slice(u * C, (u + 1) * C)
    lane = lax.broadcasted_iota(jnp.int32, (1, LANES), 1)
    lo = lane < RW_N
    ones_bd = ones_ref[...]

    def headsum(x):
        xb = x.astype(BF16)
        return jnp.concatenate([_dg(xb[:, c:c + MXU_WIDTH], ones_bd, "nn") for c in range(0, RW, MXU_WIDTH)],
                               axis=1)

    def prepare(us):
        xs_parts = []
        for u in us:
            p = p_ref[u, 0]
            rolled = pltpu.roll(p, 1, 0)
            if has_state:
                prev = jnp.where(row % seg_len == 0, p0_ref[u, 0], rolled)
            else:
                prev = jnp.where(row == 0, last_ref[u], rolled)
                last_ref[u] = p_ref[u, 0, C - 1:C, :]
            xs_parts.append(p + mu_ref[...] * (prev - p))
        xs = jnp.concatenate(xs_parts, axis=0)
        r = xs[:, 0:RW]
        k = xs[:, RW:2 * RW]
        v = xs[:, 2 * RW:3 * RW]
        lwla = xs[:, 3 * RW:3 * RW + LANES]
        lg = xs[:, 3 * RW + LANES:3 * RW + 2 * LANES]
        lwla = jnp.where(lane < lora_w, jnp.tanh(lwla), lwla)
        wa = jnp.dot(lwla.astype(BF16), w2a2_ref[...], preferred_element_type=F32)
        logw = -math.exp(-0.5) * _sigmoid(w0_ref[...] + wa[:, :RW])
        a_sig = _sigmoid(a0_ref[...] + wa[:, RW:])
        g = jnp.dot(_sigmoid(lg).astype(BF16), g2_ref[...], preferred_element_type=F32)
        kk = k * kk_ref[...]
        kk = kk * (1.0 / jnp.maximum(jnp.sqrt(headsum(kk * kk)), 1e-12))
        k = k * (1.0 + (a_sig - 1.0) * ka_ref[...])
        a = -kk
        b = kk * a_sig
        bonus = headsum(r * k * rk_ref[...]) * v
        cws, cwl = [], []
        for i in range(len(us)):
            cw_u = _mm_exact_lhs(ltri_ref[...], logw[rows_of(i)], 3)
            cws.append(cw_u)
            if n_seg == 1:
                cwl.append(jnp.broadcast_to(cw_u[C - 1:C, :], (C, RW)))
            else:
                cwl.append(_mm_exact_lhs(elast_ref[...], cw_u, 3))
        cw = jnp.concatenate(cws, axis=0)
        cw_last = jnp.concatenate(cwl, axis=0)
        w_inv = jnp.exp(-cw)
        b_t, k_t = b * w_inv, k * w_inv
        w_tail = jnp.exp(cw_last - cw)
        b_h, k_h = b * w_tail, k * w_tail
        a_t, r_t = a * jnp.exp(cw - logw), r * jnp.exp(cw)
        w_c = jnp.exp(cw_last)
        return a_t, r_t, b_t, k_t, b_h, k_h, v, w_c, bonus, g

    def bd(y):
        return jnp.concatenate([jnp.where(lo, y, 0.0), jnp.where(lo, 0.0, y)], axis=0)

    zeros_head = jnp.zeros((RW_N, RW_N), F32)

    def pack_pair(s_even, s_odd):
        return jnp.concatenate([jnp.concatenate([s_even, zeros_head], axis=1),
                                jnp.concatenate([zeros_head, s_odd], axis=1)], axis=0)

    def store_pair(ref, i, q, s_pair):
        ref[i, 2 * q] = s_pair[:RW_N, :RW_N]
        ref[i, 2 * q + 1] = s_pair[RW_N:, RW_N:]

    s_idx = lane % C
    strict = s_idx < row
    incl = s_idx <= row
    if n_seg > 1:
        same_seg = (s_idx // seg_len) == (row // seg_len)
        strict = jnp.logical_and(strict, same_seg)
        incl = jnp.logical_and(incl, same_seg)
    row2 = lax.broadcasted_iota(jnp.int32, (2 * C, 1), 0)
    same_head = (row2 < RW_N) == lo

    part = lambda x, i, q: x[i * C:(i + 1) * C, q * LANES:(q + 1) * LANES]

    def recurrence(us, ops):
        a_t, r_t, b_t, k_t, b_h, k_h, v, w_c = ops[:8]
        units = [(i, q) for i in range(len(us)) for q in range(n_pairs)]
        sc = {uq: _mm(jnp.concatenate([part(a_t, *uq), part(r_t, *uq)], axis=0),
                      jnp.concatenate([bd(part(b_t, *uq)), bd(part(k_t, *uq))], axis=0),
                      "nt") for uq in units}
        pw = {uq: jnp.where(strict, sc[uq][:C, :LANES], 0.0) for uq in units}
        m_rb = {uq: jnp.where(incl, sc[uq][C:, :LANES], 0.0) for uq in units}
        lm_v = {uq: _mm(jnp.concatenate([jnp.where(strict, sc[uq][:C, LANES:], 0.0),
                                         jnp.where(incl, sc[uq][C:, LANES:], 0.0)], axis=0),
                        bd(part(v, *uq)), "nn") for uq in units}
        tinv = {uq: pw[uq] + jnp.where(s_idx == row, 1.0, 0.0) for uq in units}
        n_lvl = int(math.log2(seg_len))
        for lvl in range(1, n_lvl):
            last = lvl + 1 == n_lvl
            if lvl == 1:
                for uq in units:
                    pw[uq] = _mm(pw[uq], bd(pw[uq]), "nn")
            for uq in units:
                rhs = bd(tinv[uq]) if last else jnp.concatenate([bd(tinv[uq]), bd(pw[uq])], axis=1)
                upd = _mm(pw[uq], rhs, "nn")
                tinv[uq] = tinv[uq] + upd[:, :LANES]
                if not last:
                    pw[uq] = upd[:, LANES:]
        x = {uq: _mm(tinv[uq], jnp.concatenate([bd(part(a_t, *uq)), bd(lm_v[uq][:C])], axis=1), "nn")
             for uq in units}
        a_hat = {uq: x[uq][:, :LANES] for uq in units}
        v_hat = {uq: x[uq][:, LANES:] for uq in units}
        z = {uq: _mm(m_rb[uq], jnp.concatenate([bd(a_hat[uq]), bd(v_hat[uq])], axis=1), "nn")
             for uq in units}
        r_hat = {uq: part(r_t, *uq) + z[uq][:, :LANES] for uq in units}
        y_intra = {uq: z[uq][:, LANES:] + lm_v[uq][C:] for uq in units}
        ys = {}
        if n_seg == 1:
            s_old = {(i, q): sbd_ref[us[i], q] for i, q in units}
            t1 = {uq: _mm(jnp.concatenate([a_hat[uq], r_hat[uq]], axis=0), s_old[uq], "nt")
                  for uq in units}
            for uq in units:
                ys[uq] = t1[uq][C:] + y_intra[uq]
                uv = jnp.concatenate([t1[uq][:C] + v_hat[uq], part(v, *uq)], axis=0)
                bkh = jnp.concatenate([part(b_h, *uq), part(k_h, *uq)], axis=0)
                ds = _mm(uv.T, bkh, "nn")
                sbd_ref[us[uq[0]], uq[1]] = s_old[uq] * part(w_c, *uq)[0:1] + jnp.where(same_head, ds, 0.0)
        else:
            row_seg = (row2 % C) // seg_len
            for uq in units:
                i, q = uq
                first_seq = us[i] * n_seg
                u_parts, y_parts, s_olds = [], [], []
                for sg in range(n_seg):
                    rs = slice(sg * seg_len, (sg + 1) * seg_len)
                    s_sg = pack_pair(s0_ref[first_seq + sg, 2 * q], s0_ref[first_seq + sg, 2 * q + 1])
                    t1 = _mm(jnp.concatenate([a_hat[uq][rs], r_hat[uq][rs]], axis=0), s_sg, "nt")
                    u_parts.append(t1[:seg_len] + v_hat[uq][rs])
                    y_parts.append(t1[seg_len:] + y_intra[uq][rs])
                    s_olds.append(s_sg)
                ys[uq] = jnp.concatenate(y_parts, axis=0)
                uv_t = jnp.concatenate(u_parts + [part(v, *uq)], axis=0).T
                bkh = jnp.concatenate([part(b_h, *uq), part(k_h, *uq)], axis=0)
                w_cq = part(w_c, *uq)
                for sg in range(n_seg):
                    ds = _mm(uv_t, jnp.where(row_seg == sg, bkh, 0.0), "nn")
                    store_pair(sout_ref, first_seq + sg, q,
                               s_olds[sg] * w_cq[sg * seg_len:sg * seg_len + 1] + jnp.where(same_head, ds, 0.0))
        return ys

    def finish(us, ys, ops):
        bonus, g = ops[8:]
        y = jnp.concatenate([jnp.concatenate([ys[i, q] for q in range(n_pairs)], axis=1)
                             for i in range(len(us))], axis=0)
        mean = headsum(y) * (1.0 / RW_N)
        d = y - mean
        var = headsum(d * d) * (1.0 / RW_N)
        y = d * lax.rsqrt(var + GN_EPS) * lng_ref[...] + lnb_ref[...]
        y = ((y + bonus) * g).astype(out_ref.dtype)
        for i, u in enumerate(us):
            out_ref[u, 0] = y[rows_of(i)]

    all_units = list(range(n_units))
    ops = prepare(all_units)
    finish(all_units, recurrence(all_units, ops), ops)
    units = [(u, q) for u in range(n_units) for q in range(n_pairs)]

    if not has_state:
        @pl.when(pl.program_id(1) == pl.num_programs(1) - 1)
        def _():
            for u, q in units:
                store_pair(sout_ref, u, q, sbd_ref[u, q])


def _rwkv_mixer(rw, batch, seq, shift0, wkv0, mu, w0, w2, a0, a2, g2, k_k, k_a, r_k, lnx_g, lnx_b):
    n_shift = rw.shape[1]
    lora_w, lora_a = w2.shape[0], a2.shape[0]
    assert lora_w + lora_a == LANES and g2.shape[0] == LANES and n_shift == 3 * RW + 2 * LANES
    has_state = shift0 is not None
    C = CHUNK
    n_pairs = RW // LANES
    if has_state:
        assert C % seq == 0 and batch % (C // seq) == 0
        n_seg, n_chunks, n_groups = C // seq, 1, batch * seq // C
    else:
        assert seq % C == 0
        n_seg, n_chunks, n_groups = 1, seq // C, batch
    want = WKV_UNITS_WITH_STATE if has_state else WKV_UNITS
    nu = want if n_groups % want == 0 else 1
    seg_len = C // n_seg
    w2a2 = jnp.zeros((LANES, 2 * RW), F32).at[:lora_w, :RW].set(w2).at[lora_w:, RW:].set(a2).astype(BF16)
    t = np.arange(C)
    same_seg = (t[:, None] // seg_len) == (t[None, :] // seg_len)
    ltri = jnp.asarray(((t[:, None] >= t[None, :]) & same_seg).astype(np.float32), BF16)
    elast = jnp.asarray((t[None, :] == (t[:, None] // seg_len) * seg_len + seg_len - 1).astype(np.float32), BF16)
    ones_bd = jnp.asarray(np.kron(np.eye(MXU_WIDTH // RW_N, dtype=np.float32),
                                  np.ones((RW_N, RW_N), np.float32)), BF16)
    row = lambda x: x.reshape(1, -1).astype(F32)
    rw4 = rw.reshape(n_groups, n_chunks, C, n_shift)
    blk = lambda w: pl.BlockSpec((nu, 1, C, w), lambda i, c: (i, c, 0, 0))
    st_blk = pl.BlockSpec((nu * n_seg, RW_HEADS, RW_N, RW_N), lambda i, c: (i, 0, 0, 0))
    args, specs = [rw4], [blk(n_shift)]
    if has_state:
        p0 = jnp.pad(shift0[:, None, :], ((0, 0), (0, seq - 1), (0, 0))).reshape(n_groups, 1, C, n_shift)
        args += [p0, wkv0]
        specs += [blk(n_shift), st_blk]
    consts = [row(mu), w2a2, row(w0), row(a0), g2.astype(BF16), row(k_k), row(k_a), row(r_k), row(lnx_g),
              row(lnx_b), ltri] + ([elast] if n_seg > 1 else []) + [ones_bd]
    args += consts
    specs += [_resident(c.shape) for c in consts]
    scratch = [] if has_state else [pltpu.VMEM((nu, 1, n_shift), F32),
                                    pltpu.VMEM((nu, n_pairs, LANES, LANES), F32)]
    out, s_new = pl.pallas_call(
        functools.partial(_wkv_kernel, n_units=nu, n_seg=n_seg, has_state=has_state, lora_w=lora_w),
        grid=(n_groups // nu, n_chunks),
        in_specs=specs,
        out_specs=[blk(RW), st_blk],
        out_shape=[
            jax.ShapeDtypeStruct((n_groups, n_chunks, C, RW), BF16),
            jax.ShapeDtypeStruct((batch, RW_HEADS, RW_N, RW_N), F32),
        ],
        scratch_shapes=scratch,
        compiler_params=_cparams(("arbitrary", "arbitrary")),
        name="rwkv7",
    )(*args)
    return out.reshape(batch * seq, RW), s_new


def _branch_mix(x_ref, att_ref, rwo_ref, gate_ref, wpa_ref, wpb_ref, wo_ref):
    d = x_ref.shape[1]
    pa = jnp.dot(att_ref[...], wpa_ref[...], preferred_element_type=F32)
    pb = jnp.dot(rwo_ref[...], wpb_ref[...], preferred_element_type=F32)
    mix = gate_ref[:, :d] * pa + gate_ref[:, d:] * pb
    return x_ref[...] + jnp.dot(mix.astype(BF16), wo_ref[...], preferred_element_type=F32)


def _rms(x, g):
    ms = jnp.mean(x * x, axis=-1, keepdims=True)
    return x * lax.rsqrt(ms + NORM_EPS) * g


def _gelu_tanh(c):
    return c * (0.5 * (1.0 + jnp.tanh(math.sqrt(2.0 / math.pi) * (c + 0.044715 * (c * c * c)))))


def _conv_ffn_cols(h, x, wup_ref, cw_ref, cb_ref, wdown_ref, shifted, ug_sink):
    d_ff = cb_ref.shape[1]
    tiles = -(-d_ff // MXU_WIDTH)
    edges = [min(d_ff, MXU_WIDTH * ((tiles * j + FFN_COL_CHUNKS - 1) // FFN_COL_CHUNKS))
             for j in range(FFN_COL_CHUNKS + 1)]
    spans = list(zip(edges[:-1], edges[1:]))

    def up(span):
        lo_c, hi_c = span
        return (jnp.dot(h, wup_ref[:, lo_c:hi_c], preferred_element_type=F32),
                jnp.dot(h, wup_ref[:, d_ff + lo_c:d_ff + hi_c], preferred_element_type=F32))

    acc = x
    pending = up(spans[0])
    for j, (lo_c, hi_c) in enumerate(spans):
        cs = slice(lo_c, hi_c)
        ug_full, uv_full = pending
        if j + 1 < len(spans):
            pending = up(spans[j + 1])
        ug, ug_m1, ug_m2, uv = shifted(ug_full, uv_full, cs)
        c = cb_ref[:, cs] + cw_ref[0:1, cs] * ug_m2 + cw_ref[1:2, cs] * ug_m1 + cw_ref[2:3, cs] * ug
        act = (_gelu_tanh(c) * uv).astype(BF16)
        acc = acc + jnp.dot(act, wdown_ref[lo_c:hi_c, :], preferred_element_type=F32)
        ug_sink(ug, cs)
    return acc


def _ffn_prompt_kernel(x_ref, att_ref, rwo_ref, gate_ref, wpa_ref, wpb_ref, wo_ref, g2_ref, wup_ref, cw_ref,
                       cb_ref, wdown_ref, gf_ref, y_ref, ug_ref, carry_ref):
    tm = x_ref.shape[0]
    x = _branch_mix(x_ref, att_ref, rwo_ref, gate_ref, wpa_ref, wpb_ref, wo_ref)
    h = _rms(x, g2_ref[...]).astype(BF16)
    seq_start = pl.program_id(1) == 0

    def shifted(ug, uv, cs):
        before = jnp.where(seq_start, 0.0, carry_ref[:, cs])
        carry_ref[:, cs] = ug[tm - SUBLANES:]
        ug_e = jnp.concatenate([before, ug], axis=0)
        return ug, pltpu.roll(ug_e, 1, 0)[SUBLANES:], pltpu.roll(ug_e, 2, 0)[SUBLANES:], uv

    def ug_sink(ug, cs):
        ug_ref[:, cs] = ug[tm - SUBLANES:]

    x2 = _conv_ffn_cols(h, x, wup_ref, cw_ref, cb_ref, wdown_ref, shifted, ug_sink)
    y_ref[...] = _rms(x2, gf_ref[...])


def _ffn_sample_kernel(x_ref, att_ref, rwo_ref, gate_ref, e_ref, wpa_ref, wpb_ref, wo_ref, g2_ref, wup_ref,
                       cw_ref, cb_ref, wdown_ref, gf_ref, y_ref, ug_ref, *, seq):
    rows = x_ref.shape[0]
    x = _branch_mix(x_ref, att_ref, rwo_ref, gate_ref, wpa_ref, wpb_ref, wo_ref)
    h = _rms(x, g2_ref[...]).astype(BF16)
    t = lax.broadcasted_iota(jnp.int32, (rows, 1), 0) % seq

    def shifted(ug, uv, cs):
        e = e_ref[:, cs]
        ug_m1 = jnp.where(t == 0, pltpu.roll(e, rows - 1, 0), pltpu.roll(ug, 1, 0))
        ug_m2 = jnp.where(t < 2, e, pltpu.roll(ug, 2, 0))
        return ug, ug_m1, ug_m2, uv

    def ug_sink(ug, cs):
        ug_ref[:, cs] = ug

    x2 = _conv_ffn_cols(h, x, wup_ref, cw_ref, cb_ref, wdown_ref, shifted, ug_sink)
    y_ref[...] = _rms(x2, gf_ref[...])


def _ffn_prompt(x2d, att, rwo, gates, batch, seq, lw, final_g, tm):
    n, d = x2d.shape
    d_ff = lw["conv_b"].shape[0]
    nt = seq // tm
    rows = lambda a: pl.BlockSpec((tm, a.shape[1]), lambda b, i: (b * nt + i, 0))
    full = lambda a: _resident(a.shape)
    consts = [lw["w_pa"], lw["w_pb"], lw["w_o"], lw["norm2_g"].reshape(1, d), lw["w_up"], lw["conv_w"],
              lw["conv_b"].reshape(1, d_ff), lw["w_down"], final_g.reshape(1, d)]
    acts = [x2d, att, rwo, gates]
    y, ug_last = pl.pallas_call(
        _ffn_prompt_kernel,
        grid=(batch, nt),
        in_specs=[rows(a) for a in acts] + [full(c) for c in consts],
        out_specs=[
            pl.BlockSpec((tm, d), lambda b, i: (b * nt + i, 0)),
            pl.BlockSpec((SUBLANES, d_ff), lambda b, i: (b, 0)),
        ],
        out_shape=[
            jax.ShapeDtypeStruct((n, d), F32),
            jax.ShapeDtypeStruct((batch * SUBLANES, d_ff), F32),
        ],
        scratch_shapes=[pltpu.VMEM((SUBLANES, d_ff), F32)],
        compiler_params=_cparams(("arbitrary", "arbitrary")),
        name="mix_conv_ffn_prompt",
    )(*acts, *consts)
    conv_new = ug_last.reshape(batch, SUBLANES, d_ff)[:, SUBLANES - (CONV_W - 1):]
    return y, conv_new


def _ffn_sample(x2d, att, rwo, gates, batch, seq, conv0, lw, final_g, bt):
    n, d = x2d.shape
    d_ff = lw["conv_b"].shape[0]
    nrows = bt * seq
    e = jnp.pad(conv0, ((0, 0), (0, seq - (CONV_W - 1)), (0, 0))).reshape(n, d_ff)
    rows = lambda a: pl.BlockSpec((nrows, a.shape[1]), lambda i: (i, 0))
    full = lambda a: _resident(a.shape)
    consts = [lw["w_pa"], lw["w_pb"], lw["w_o"], lw["norm2_g"].reshape(1, d), lw["w_up"], lw["conv_w"],
              lw["conv_b"].reshape(1, d_ff), lw["w_down"], final_g.reshape(1, d)]
    acts = [x2d, att, rwo, gates, e]
    y, ug = pl.pallas_call(
        functools.partial(_ffn_sample_kernel, seq=seq),
        grid=(n // nrows,),
        in_specs=[rows(a) for a in acts] + [full(c) for c in consts],
        out_specs=[
            pl.BlockSpec((nrows, d), lambda i: (i, 0)),
            pl.BlockSpec((nrows, d_ff), lambda i: (i, 0)),
        ],
        out_shape=[jax.ShapeDtypeStruct((n, d), F32), jax.ShapeDtypeStruct((n, d_ff), F32)],
        compiler_params=_cparams(("parallel",)),
        name="mix_conv_ffn_sample",
    )(*acts, *consts)
    conv_new = ug.reshape(batch, seq, d_ff)[:, seq - (CONV_W - 1):]
    return y, conv_new


def _row_tile(n, want):
    t = min(want, n)
    while n % t:
        t //= 2
    return t


def _layer(x, is_prompt, state, rel_bias, lw, final_g):
    batch, seq, d = x.shape
    n = batch * seq
    x2d = x.reshape(n, d)
    n_qkv = Q_W + 2 * KV_W
    n_rw = lw["mu_shift"].shape[0]
    tm = _row_tile(n, min(1024, n // 4))
    qkv, rw, gates = _in_proj(x2d, lw["norm1_g"], lw["w_in"], n_qkv, n_rw, tm)
    if is_prompt:
        att, k_rows, v_rows = _attention_prompt(qkv, batch, seq, rel_bias, lw["sinks"], state["win_buf"])
        shift0 = wkv0 = None
    else:
        cache_k, cache_v = state["cache_k"], state["cache_v"]
        wb = cache_k.shape[1]
        assert seq <= wb
        att, k_rows, v_rows = _attention_sample(qkv, batch, seq, cache_k, cache_v, rel_bias, lw["sinks"],
                                                _row_tile(batch, 16))
        shift0, wkv0 = state["shift"], state["wkv"]
    k_rows = k_rows.reshape(batch, -1, N_KV, HEAD_DIM)
    v_rows = v_rows.reshape(batch, -1, N_KV, HEAD_DIM)
    rwo, wkv_new = _rwkv_mixer(rw, batch, seq, shift0, wkv0, lw["mu_shift"], lw["w0"], lw["w2"], lw["a0"],
                               lw["a2"], lw["g2"], lw["k_k"], lw["k_a"], lw["r_k"], lw["lnx_g"], lw["lnx_b"])
    shift_new = rw.reshape(batch, seq, n_rw)[:, seq - 1]
    if is_prompt:
        y, conv_new = _ffn_prompt(x2d, att, rwo, gates, batch, seq, lw, final_g, _row_tile(seq, FFN_ROWS))
    else:
        y, conv_new = _ffn_sample(x2d, att, rwo, gates, batch, seq, state["conv"], lw, final_g,
                                  _row_tile(batch, FFN_ROWS_WITH_STATE // seq))
    return y.reshape(batch, seq, d), (k_rows, v_rows, shift_new, wkv_new, conv_new)


def kernel(x_prompt, x_sample, cache_win_k, cache_win_v, state_shift, state_wkv, state_conv, rel_bias, norm1_g,
           w_in, sinks, mu_shift, w0, w2, a0, a2, g2, k_k, k_a, r_k, lnx_g, lnx_b, w_pa, w_pb, w_o, norm2_g,
           w_up, conv_w, conv_b, w_down, final_g):
    depth = w_in.shape[0]
    assert depth == 1, "the final norm is fused into the layer's last kernel"
    l = 0
    lw = dict(norm1_g=norm1_g[l], w_in=w_in[l].astype(BF16), sinks=sinks[l], mu_shift=mu_shift[l], w0=w0[l],
              w2=w2[l], a0=a0[l], a2=a2[l], g2=g2[l], k_k=k_k[l], k_a=k_a[l], r_k=r_k[l].reshape(-1),
              lnx_g=lnx_g[l], lnx_b=lnx_b[l], w_pa=w_pa[l].astype(BF16), w_pb=w_pb[l].astype(BF16),
              w_o=w_o[l].astype(BF16), norm2_g=norm2_g[l], w_up=w_up[l].astype(BF16), conv_w=conv_w[l],
              conv_b=conv_b[l], w_down=w_down[l].astype(BF16))
    win_buf = cache_win_k.shape[2]
    y_p, st_p = _layer(x_prompt, True, dict(win_buf=win_buf), rel_bias, lw, final_g)
    y_s, st_s = _layer(x_sample, False,
                       dict(cache_k=cache_win_k[l], cache_v=cache_win_v[l], shift=state_shift[l],
                            wkv=state_wkv[l], conv=state_conv[l]), rel_bias, lw, final_g)
    stack = lambda t: t[None]
    return (y_p, y_s) + tuple(stack(t) for t in st_p) + tuple(stack(t) for t in st_s)
```

```python
import functools
import math

import numpy as np
import jax
import jax.numpy as jnp
from jax import lax
from jax.experimental import pallas as pl
from jax.experimental.pallas import tpu as pltpu

F32 = jnp.float32
BF16 = jnp.bfloat16

HEAD_DIM = 64
N_HEADS = 8
N_KV = 2
WINDOW = 128
N_BUCKETS = 32
MAX_EXACT = N_BUCKETS // 2
REL_MAX_DIST = 128
RW_N = 64
RW_HEADS = 8
RW = RW_HEADS * RW_N
NORM_EPS = 1e-6
GN_EPS = 64e-5
NEG = -1e30
CONV_W = 3

Q_W = N_HEADS * HEAD_DIM
KV_W = N_KV * HEAD_DIM
LANES = 128
SUBLANES = 8
MXU_WIDTH = 256
CHUNK = 64
VMEM_LIMIT = 56 * 1024 * 1024
FFN_COL_CHUNKS = 3
FFN_ROWS = 512
FFN_ROWS_WITH_STATE = 256
ATTN_BLOCKS = 8
WKV_UNITS = 8
WKV_UNITS_WITH_STATE = 4


def _resident(shape):
    return pl.BlockSpec(shape, lambda *_: (0,) * len(shape), pipeline_mode=pl.Buffered(1))


def _cparams(sem):
    return pltpu.CompilerParams(dimension_semantics=sem, vmem_limit_bytes=VMEM_LIMIT)


def _sigmoid(x):
    return 1.0 / (1.0 + jnp.exp(-x))


def _dg(a, b, kind):
    if kind == "nn":
        dn = (((1,), (0,)), ((), ()))
    else:
        dn = (((1,), (1,)), ((), ()))
    return lax.dot_general(a, b, dn, preferred_element_type=F32)


def _mm(a, b, kind="nn"):
    return _dg(a.astype(BF16), b.astype(BF16), kind)


def _mm_exact_lhs(a_bf16, b, n_terms):
    out = None
    rem = b
    for _ in range(n_terms):
        piece = rem.astype(BF16)
        term = _dg(a_bf16, piece, "nn")
        out = term if out is None else out + term
        rem = rem - piece.astype(F32)
    return out


def _inproj_kernel(x_ref, g_ref, w_ref, qkv_ref, rw_ref, gate_ref, *, n_qkv, n_rw):
    x = x_ref[...]
    ms = jnp.mean(x * x, axis=-1, keepdims=True)
    h = (x * lax.rsqrt(ms + NORM_EPS) * g_ref[...]).astype(BF16)
    n_gate = gate_ref.shape[1]
    step = 2 * MXU_WIDTH
    plain = [(qkv_ref, c, c, min(step, n_qkv - c)) for c in range(0, n_qkv, step)]
    plain += [(rw_ref, c, n_qkv + c, min(step, n_rw - c)) for c in range(0, n_rw, step)]
    gated = [(gate_ref, c, n_qkv + n_rw + c, min(step, n_gate - c)) for c in range(0, n_gate, step)]
    pieces = []
    while plain or gated:
        pieces += [gated.pop(0)] if gated else []
        pieces += [plain.pop(0)] if plain else []
    dot = lambda p: jnp.dot(h, w_ref[:, p[2]:p[2] + p[3]], preferred_element_type=F32)
    pending = dot(pieces[0])
    for j, (ref, c0, _, width) in enumerate(pieces):
        out = pending
        if j + 1 < len(pieces):
            pending = dot(pieces[j + 1])
        if ref is gate_ref:
            out = _sigmoid(out)
        ref[:, c0:c0 + width] = out.astype(ref.dtype)


def _in_proj(x2d, g, w_bf16, n_qkv, n_rw, tm):
    n, d = x2d.shape
    n_gate = w_bf16.shape[1] - n_qkv - n_rw
    return pl.pallas_call(
        functools.partial(_inproj_kernel, n_qkv=n_qkv, n_rw=n_rw),
        grid=(n // tm,),
        in_specs=[
            pl.BlockSpec((tm, d), lambda i: (i, 0)),
            _resident((1, d)),
            _resident(w_bf16.shape),
        ],
        out_specs=[
            pl.BlockSpec((tm, n_qkv), lambda i: (i, 0)),
            pl.BlockSpec((tm, n_rw), lambda i: (i, 0)),
            pl.BlockSpec((tm, n_gate), lambda i: (i, 0)),
        ],
        out_shape=[
            jax.ShapeDtypeStruct((n, n_qkv), F32),
            jax.ShapeDtypeStruct((n, n_rw), F32),
            jax.ShapeDtypeStruct((n, n_gate), BF16),
        ],
        compiler_params=_cparams(("parallel",)),
        name="in_proj",
    )(x2d, g.reshape(1, d), w_bf16)


def _t5_bucket_np(dist):
    n = np.maximum(dist, 0)
    nf = np.maximum(n, 1).astype(np.float32)
    large = MAX_EXACT + (np.log(nf / MAX_EXACT) / math.log(REL_MAX_DIST / MAX_EXACT)
                         * (N_BUCKETS - MAX_EXACT)).astype(np.int32)
    return np.where(n < MAX_EXACT, n, np.minimum(large, N_BUCKETS - 1)).astype(np.int32)


def _attn_kernel(q_ref, k1_ref, k2_ref, v1_ref, v2_ref, bucket_ref, relb_ref, sink_ref, o_ref, *rest,
                 nq, nk, n_blocks, first_block_axis, emit_window):
    bt = q_ref.shape[0]
    bias_ref = rest[-1]
    if emit_window == "tail":
        @pl.when(pl.program_id(1) == pl.num_programs(1) - 1)
        def _():
            n_new, n_win = k2_ref.shape[1], rest[0].shape[1]
            rest[0][...] = k2_ref[:, n_new - n_win:]
            rest[1][...] = v2_ref[:, n_new - n_win:]
    else:
        n_old, n_new = k1_ref.shape[1], k2_ref.shape[1]
        for w_ref, old_ref, new_ref in ((rest[0], k1_ref, k2_ref), (rest[1], v1_ref, v2_ref)):
            w_ref[:, :n_old - n_new] = old_ref[:, n_new:]
            w_ref[:, n_old - n_new:] = new_ref[...]
    first_step = pl.program_id(0) == 0
    if first_block_axis is not None:
        first_step = jnp.logical_and(first_step, pl.program_id(1) == 0)

    @pl.when(first_step)
    def _():
        bucket = bucket_ref[...]
        prev_key = lax.broadcasted_iota(jnp.int32, (1, nk), 1) < (nk // 2)
        for n in range(N_HEADS):
            acc = jnp.full((nq, nk), NEG, F32)
            for b in range(N_BUCKETS):
                acc = jnp.where(bucket == b, relb_ref[b, n], acc)
            c, half = divmod(n, 2)
            bias_ref[0, c, :, half * nk:(half + 1) * nk] = acc
            if first_block_axis is not None:
                bias_ref[1, c, :, half * nk:(half + 1) * nk] = jnp.where(prev_key, NEG, acc)

    n_keys = nk + (n_blocks - 1) * nq

    def padded(a_ref, b_ref):
        parts = [a_ref[...], b_ref[...]]
        n_now = a_ref.shape[1] + b_ref.shape[1]
        if n_now < n_keys:
            parts.append(jnp.zeros((bt, n_keys - n_now, LANES), F32))
        return jnp.concatenate(parts, axis=1)

    kk = padded(k1_ref, k2_ref)
    vv = padded(v1_ref, v2_ref)
    kk_r = pltpu.roll(kk, HEAD_DIM, 2)
    vv_r = pltpu.roll(vv, HEAD_DIM, 2)
    lane = lax.broadcasted_iota(jnp.int32, (1, 1, LANES), 2)
    lo = lane < HEAD_DIM

    def halves(x, x_r, kvh):
        src_lo, src_hi = (x, x_r) if kvh == 0 else (x_r, x)
        even = jnp.where(lo, src_lo, 0.0).astype(BF16)
        odd = jnp.where(lo, 0.0, src_hi).astype(BF16)
        return even, odd

    k_eo = [halves(kk, kk_r, h) for h in range(N_KV)]
    v_eo = [halves(vv, vv_r, h) for h in range(N_KV)]

    def window(eo, j):
        return jnp.concatenate([eo[0][:, j * nq:j * nq + nk], eo[1][:, j * nq:j * nq + nk]], axis=1)

    n_cols = N_HEADS // 2
    units = [(j, c) for j in range(n_blocks) for c in range(n_cols)]
    kvh_of = lambda c: (2 * c) // (N_HEADS // N_KV)
    k_win = {(j, h): window(k_eo[h], j) for j in range(n_blocks) for h in range(N_KV)}
    v_win = {(j, h): window(v_eo[h], j) for j in range(n_blocks) for h in range(N_KV)}
    sums_on_mxu = bt == 1
    if sums_on_mxu:
        key_row = lax.broadcasted_iota(jnp.int32, (1, 2 * nk, 1), 1)
        ones_cols = jnp.where((key_row < nk) == lo, 1.0, 0.0).astype(BF16)
        v_win = {jh: jnp.concatenate([v, ones_cols], axis=2) for jh, v in v_win.items()}

    scale = HEAD_DIM ** -0.5
    first_table = 0
    if first_block_axis is not None:
        first_table = jnp.where(pl.program_id(first_block_axis) == 0, 1, 0)
    s_all = {}
    for j, c in units:
        qc = (q_ref[:, j * nq:(j + 1) * nq, c * LANES:(c + 1) * LANES] * scale).astype(BF16)
        s_all[j, c] = jnp.einsum("bqd,bkd->bqk", qc, k_win[j, kvh_of(c)], preferred_element_type=F32)
    e_all, sink_all = {}, {}
    for j, c in units:
        s = s_all[j, c] + bias_ref[first_table if j == 0 else 0, c][None]
        es, sink_terms = [], []
        for half in range(2):
            sh = s[:, :, half * nk:(half + 1) * nk]
            sink = sink_ref[2 * c + half]
            m = jnp.maximum(jnp.max(sh, axis=-1, keepdims=True), sink)
            e = jnp.exp(sh - m)
            es.append(e.astype(BF16))
            sink_terms.append(jnp.exp(sink - m) + (0.0 if sums_on_mxu else jnp.sum(e, axis=-1, keepdims=True)))
        e_all[j, c] = jnp.concatenate(es, axis=2)
        sink_all[j, c] = jnp.where(lo, sink_terms[0], sink_terms[1])
    for j, c in units:
        o = jnp.einsum("bqk,bkd->bqd", e_all[j, c], v_win[j, kvh_of(c)], preferred_element_type=F32)
        denom = sink_all[j, c] + (o[:, :, LANES:] if sums_on_mxu else 0.0)
        o = (o[:, :, :LANES] / denom)
        if len(o_ref.shape) == 2:
            o_ref[:, c * LANES:(c + 1) * LANES] = o.reshape(bt * nq, LANES).astype(o_ref.dtype)
        else:
            o_ref[:, j * nq:(j + 1) * nq, c * LANES:(c + 1) * LANES] = o.astype(o_ref.dtype)


def _attention_prompt(qkv, batch, seq, rel_bias, sinks, win_buf):
    nblk = seq // WINDOW
    nb = ATTN_BLOCKS if nblk % ATTN_BLOCKS == 0 else 1
    nsteps = nblk // nb
    width = qkv.shape[1]
    q_blk = qkv.reshape(batch * nblk, WINDOW, width)
    q_step = qkv.reshape(batch * nsteps, nb * WINDOW, width)
    kcol = Q_W // LANES
    vcol = (Q_W + KV_W) // LANES
    nk = 2 * WINDOW
    qi = np.arange(WINDOW)[:, None] + WINDOW
    kj = np.arange(nk)[None, :]
    dist = qi - kj
    bucket = np.where((dist >= 0) & (dist < WINDOW), _t5_bucket_np(dist), -1).astype(np.int32)
    cur = lambda c: (lambda b, i: (b * nsteps + i, 0, c))
    prev = lambda c: (lambda b, i: (b * nblk + jnp.maximum(i * nb - 1, 0), 0, c))
    assert win_buf <= nb * WINDOW
    win_spec = pl.BlockSpec((1, win_buf, LANES), lambda b, i: (b, 0, 0))
    win_shape = jax.ShapeDtypeStruct((batch, win_buf, KV_W), F32)
    out, win_k, win_v = pl.pallas_call(
        functools.partial(_attn_kernel, nq=WINDOW, nk=nk, n_blocks=nb, first_block_axis=1, emit_window="tail"),
        grid=(batch, nsteps),
        in_specs=[
            pl.BlockSpec((1, nb * WINDOW, Q_W), cur(0)),
            pl.BlockSpec((1, WINDOW, LANES), prev(kcol)),
            pl.BlockSpec((1, nb * WINDOW, LANES), cur(kcol)),
            pl.BlockSpec((1, WINDOW, LANES), prev(vcol)),
            pl.BlockSpec((1, nb * WINDOW, LANES), cur(vcol)),
            _resident(bucket.shape),
            pl.BlockSpec(memory_space=pltpu.SMEM),
            pl.BlockSpec(memory_space=pltpu.SMEM),
        ],
        out_specs=[pl.BlockSpec((1, nb * WINDOW, Q_W), cur(0)), win_spec, win_spec],
        out_shape=[jax.ShapeDtypeStruct((batch * nsteps, nb * WINDOW, Q_W), BF16), win_shape, win_shape],
        scratch_shapes=[pltpu.VMEM((2, N_HEADS // 2, WINDOW, 2 * nk), F32)],
        compiler_params=_cparams(("arbitrary", "arbitrary")),
        name="attn_prompt",
    )(q_step, q_blk, q_step, q_blk, q_step, jnp.asarray(bucket), rel_bias, sinks)
    return out.reshape(batch * seq, Q_W), win_k, win_v


def _attention_sample(qkv, batch, seq, cache_k, cache_v, rel_bias, sinks, bt):
    wb = cache_k.shape[1]
    nk = 2 * WINDOW
    q3 = qkv.reshape(batch, seq, qkv.shape[1])
    ck = cache_k.reshape(batch, wb, KV_W)
    cv = cache_v.reshape(batch, wb, KV_W)
    kcol = Q_W // LANES
    vcol = (Q_W + KV_W) // LANES
    tq = np.arange(seq)[:, None]
    j = np.arange(nk)[None, :]
    dist = np.where(j < wb, tq + wb - j, tq - (j - wb))
    ok = (dist >= 0) & (dist < WINDOW) & (j < wb + seq)
    bucket = np.where(ok, _t5_bucket_np(dist), -1).astype(np.int32)
    win_spec = pl.BlockSpec((bt, wb, LANES), lambda b: (b, 0, 0))
    win_shape = jax.ShapeDtypeStruct((batch, wb, KV_W), F32)
    out, win_k, win_v = pl.pallas_call(
        functools.partial(_attn_kernel, nq=seq, nk=nk, n_blocks=1, first_block_axis=None, emit_window="shift"),
        grid=(batch // bt,),
        in_specs=[
            pl.BlockSpec((bt, seq, Q_W), lambda b: (b, 0, 0)),
            pl.BlockSpec((bt, wb, LANES), lambda b: (b, 0, 0)),
            pl.BlockSpec((bt, seq, LANES), lambda b: (b, 0, kcol)),
            pl.BlockSpec((bt, wb, LANES), lambda b: (b, 0, 0)),
            pl.BlockSpec((bt, seq, LANES), lambda b: (b, 0, vcol)),
            _resident(bucket.shape),
            pl.BlockSpec(memory_space=pltpu.SMEM),
            pl.BlockSpec(memory_space=pltpu.SMEM),
        ],
        out_specs=[pl.BlockSpec((bt * seq, Q_W), lambda b: (b, 0)), win_spec, win_spec],
        out_shape=[jax.ShapeDtypeStruct((batch * seq, Q_W), BF16), win_shape, win_shape],
        scratch_shapes=[pltpu.VMEM((1, N_HEADS // 2, seq, 2 * nk), F32)],
        compiler_params=_cparams(("arbitrary",)),
        name="attn_sample",
    )(q3, ck, q3, cv, q3, jnp.asarray(bucket), rel_bias, sinks)
    return out, win_k, win_v


def _wkv_kernel(*refs, n_units, n_seg, has_state, lora_w):
    C = CHUNK
    seg_len = C // n_seg
    n_pairs = RW // LANES
    it = iter(refs)
    p_ref = next(it)
    p0_ref, s0_ref = (next(it), next(it)) if has_state else (None, None)
    (mu_ref, w2a2_ref, w0_ref, a0_ref, g2_ref, kk_ref, ka_ref, rk_ref, lng_ref, lnb_ref, ltri_ref) = (
        next(it) for _ in range(11))
    elast_ref = next(it) if n_seg > 1 else None
    ones_ref, out_ref, sout_ref = next(it), next(it), next(it)
    last_ref, sbd_ref = (None, None) if has_state else (next(it), next(it))

    if not has_state:
        @pl.when(pl.program_id(1) == 0)
        def _():
            last_ref[...] = jnp.zeros_like(last_ref)
            sbd_ref[...] = jnp.zeros_like(sbd_ref)

    row = lax.broadcasted_iota(jnp.int32, (C, 1), 0)
    rows_of = lambda u: slice(u * C, (u + 1) * C)
    lane = lax.broadcasted_iota(jnp.int32, (1, LANES), 1)
    lo = lane < RW_N
    ones_bd = ones_ref[...]

    def headsum(x):
        xb = x.astype(BF16)
        return jnp.concatenate([_dg(xb[:, c:c + MXU_WIDTH], ones_bd, "nn") for c in range(0, RW, MXU_WIDTH)],
                               axis=1)

    def prepare(us):
        xs_parts = []
        for u in us:
            p = p_ref[u, 0]
            rolled = pltpu.roll(p, 1, 0)
            if has_state:
                prev = jnp.where(row % seg_len == 0, p0_ref[u, 0], rolled)
            else:
                prev = jnp.where(row == 0, last_ref[u], rolled)
                last_ref[u] = p_ref[u, 0, C - 1:C, :]
            xs_parts.append(p + mu_ref[...] * (prev - p))
        xs = jnp.concatenate(xs_parts, axis=0)
        r = xs[:, 0:RW]
        k = xs[:, RW:2 * RW]
        v = xs[:, 2 * RW:3 * RW]
        lwla = xs[:, 3 * RW:3 * RW + LANES]
        lg = xs[:, 3 * RW + LANES:3 * RW + 2 * LANES]
        lwla = jnp.where(lane < lora_w, jnp.tanh(lwla), lwla)
        wa = jnp.dot(lwla.astype(BF16), w2a2_ref[...], preferred_element_type=F32)
        logw = -math.exp(-0.5) * _sigmoid(w0_ref[...] + wa[:, :RW])
        a_sig = _sigmoid(a0_ref[...] + wa[:, RW:])
        g = jnp.dot(_sigmoid(lg).astype(BF16), g2_ref[...], preferred_element_type=F32)
        kk = k * kk_ref[...]
        kk = kk * (1.0 / jnp.maximum(jnp.sqrt(headsum(kk * kk)), 1e-12))
        k = k * (1.0 + (a_sig - 1.0) * ka_ref[...])
        a = -kk
        b = kk * a_sig
        bonus = headsum(r * k * rk_ref[...]) * v
        cws, cwl = [], []
        for i in range(len(us)):
            cw_u = _mm_exact_lhs(ltri_ref[...], logw[rows_of(i)], 3)
            cws.append(cw_u)
            if n_seg == 1:
                cwl.append(jnp.broadcast_to(cw_u[C - 1:C, :], (C, RW)))
            else:
                cwl.append(_mm_exact_lhs(elast_ref[...], cw_u, 3))
        cw = jnp.concatenate(cws, axis=0)
        cw_last = jnp.concatenate(cwl, axis=0)
        w_inv = jnp.exp(-cw)
        b_t, k_t = b * w_inv, k * w_inv
        w_tail = jnp.exp(cw_last - cw)
        b_h, k_h = b * w_tail, k * w_tail
        a_t, r_t = a * jnp.exp(cw - logw), r * jnp.exp(cw)
        if n_seg == 1:
            w_c = jnp.exp(jnp.concatenate([cw_u[C - 1:C, :] for cw_u in cws], axis=0))
        else:
            w_c = jnp.exp(cw_last)
        return a_t, r_t, b_t, k_t, b_h, k_h, v, w_c, bonus, g

    def bd(y):
        return jnp.concatenate([jnp.where(lo, y, 0.0), jnp.where(lo, 0.0, y)], axis=0)

    zeros_head = jnp.zeros((RW_N, RW_N), F32)

    def pack_pair(s_even, s_odd):
        return jnp.concatenate([jnp.concatenate([s_even, zeros_head], axis=1),
                                jnp.concatenate([zeros_head, s_odd], axis=1)], axis=0)

    def store_pair(ref, i, q, s_pair):
        ref[i, 2 * q] = s_pair[:RW_N, :RW_N]
        ref[i, 2 * q + 1] = s_pair[RW_N:, RW_N:]

    s_idx = lane % C
    strict = s_idx < row
    incl = s_idx <= row
    if n_seg > 1:
        same_seg = (s_idx // seg_len) == (row // seg_len)
        strict = jnp.logical_and(strict, same_seg)
        incl = jnp.logical_and(incl, same_seg)
    row2 = lax.broadcasted_iota(jnp.int32, (2 * C, 1), 0)
    same_head = (row2 < RW_N) == lo

    part = lambda x, i, q: x[i * C:(i + 1) * C, q * LANES:(q + 1) * LANES]

    def recurrence(us, ops):
        a_t, r_t, b_t, k_t, b_h, k_h, v, w_c = ops[:8]
        units = [(i, q) for i in range(len(us)) for q in range(n_pairs)]
        sc = {uq: _mm(jnp.concatenate([part(a_t, *uq), part(r_t, *uq)], axis=0),
                      jnp.concatenate([bd(part(b_t, *uq)), bd(part(k_t, *uq))], axis=0),
                      "nt") for uq in units}
        pw = {uq: jnp.where(strict, sc[uq][:C, :LANES], 0.0) for uq in units}
        m_rb = {uq: jnp.where(incl, sc[uq][C:, :LANES], 0.0) for uq in units}
        lm_v = {uq: _mm(jnp.concatenate([jnp.where(strict, sc[uq][:C, LANES:], 0.0),
                                         jnp.where(incl, sc[uq][C:, LANES:], 0.0)], axis=0),
                        bd(part(v, *uq)), "nn") for uq in units}
        tinv = {uq: pw[uq] + jnp.where(s_idx == row, 1.0, 0.0) for uq in units}
        n_lvl = int(math.log2(seg_len))
        for lvl in range(1, n_lvl):
            last = lvl + 1 == n_lvl
            if lvl == 1:
                for uq in units:
                    pw[uq] = _mm(pw[uq], bd(pw[uq]), "nn")
            for uq in units:
                rhs = bd(tinv[uq]) if last else jnp.concatenate([bd(tinv[uq]), bd(pw[uq])], axis=1)
                upd = _mm(pw[uq], rhs, "nn")
                tinv[uq] = tinv[uq] + upd[:, :LANES]
                if not last:
                    pw[uq] = upd[:, LANES:]
        x = {uq: _mm(tinv[uq], jnp.concatenate([bd(part(a_t, *uq)), bd(lm_v[uq][:C])], axis=1), "nn")
             for uq in units}
        a_hat = {uq: x[uq][:, :LANES] for uq in units}
        v_hat = {uq: x[uq][:, LANES:] for uq in units}
        z = {uq: _mm(m_rb[uq], jnp.concatenate([bd(a_hat[uq]), bd(v_hat[uq])], axis=1), "nn")
             for uq in units}
        r_hat = {uq: part(r_t, *uq) + z[uq][:, :LANES] for uq in units}
        y_intra = {uq: z[uq][:, LANES:] + lm_v[uq][C:] for uq in units}
        ys = {}
        if n_seg == 1:
            s_old = {(i, q): sbd_ref[us[i], q] for i, q in units}
            t1 = {uq: _mm(jnp.concatenate([a_hat[uq], r_hat[uq]], axis=0), s_old[uq], "nt")
                  for uq in units}
            for uq in units:
                ys[uq] = t1[uq][C:] + y_intra[uq]
                uv = jnp.concatenate([t1[uq][:C] + v_hat[uq], part(v, *uq)], axis=0)
                bkh = jnp.concatenate([part(b_h, *uq), part(k_h, *uq)], axis=0)
                ds = _mm(uv.T, bkh, "nn")
                w_cq = w_c[uq[0]:uq[0] + 1, uq[1] * LANES:(uq[1] + 1) * LANES]
                sbd_ref[us[uq[0]], uq[1]] = s_old[uq] * w_cq + jnp.where(same_head, ds, 0.0)
        else:
            row_seg = (row2 % C) // seg_len
            for uq in units:
                i, q = uq
                first_seq = us[i] * n_seg
                u_parts, y_parts, s_olds = [], [], []
                for sg in range(n_seg):
                    rs = slice(sg * seg_len, (sg + 1) * seg_len)
                    s_sg = pack_pair(s0_ref[first_seq + sg, 2 * q], s0_ref[first_seq + sg, 2 * q + 1])
                    t1 = _mm(jnp.concatenate([a_hat[uq][rs], r_hat[uq][rs]], axis=0), s_sg, "nt")
                    u_parts.append(t1[:seg_len] + v_hat[uq][rs])
                    y_parts.append(t1[seg_len:] + y_intra[uq][rs])
                    s_olds.append(s_sg)
                ys[uq] = jnp.concatenate(y_parts, axis=0)
                uv_t = jnp.concatenate(u_parts + [part(v, *uq)], axis=0).T
                bkh = jnp.concatenate([part(b_h, *uq), part(k_h, *uq)], axis=0)
                w_cq = part(w_c, *uq)
                for sg in range(n_seg):
                    ds = _mm(uv_t, jnp.where(row_seg == sg, bkh, 0.0), "nn")
                    store_pair(sout_ref, first_seq + sg, q,
                               s_olds[sg] * w_cq[sg * seg_len:sg * seg_len + 1] + jnp.where(same_head, ds, 0.0))
        return ys

    def finish(us, ys, ops):
        bonus, g = ops[8:]
        y = jnp.concatenate([jnp.concatenate([ys[i, q] for q in range(n_pairs)], axis=1)
                             for i in range(len(us))], axis=0)
        mean = headsum(y) * (1.0 / RW_N)
        d = y - mean
        var = headsum(d * d) * (1.0 / RW_N)
        y = d * lax.rsqrt(var + GN_EPS) * lng_ref[...] + lnb_ref[...]
        y = ((y + bonus) * g).astype(out_ref.dtype)
        for i, u in enumerate(us):
            out_ref[u, 0] = y[rows_of(i)]

    all_units = list(range(n_units))
    ops = prepare(all_units)
    finish(all_units, recurrence(all_units, ops), ops)
    units = [(u, q) for u in range(n_units) for q in range(n_pairs)]

    if not has_state:
        @pl.when(pl.program_id(1) == pl.num_programs(1) - 1)
        def _():
            for u, q in units:
                store_pair(sout_ref, u, q, sbd_ref[u, q])


def _rwkv_mixer(rw, batch, seq, shift0, wkv0, mu, w0, w2, a0, a2, g2, k_k, k_a, r_k, lnx_g, lnx_b):
    n_shift = rw.shape[1]
    lora_w, lora_a = w2.shape[0], a2.shape[0]
    assert lora_w + lora_a == LANES and g2.shape[0] == LANES and n_shift == 3 * RW + 2 * LANES
    has_state = shift0 is not None
    C = CHUNK
    n_pairs = RW // LANES
    if has_state:
        assert C % seq == 0 and batch % (C // seq) == 0
        n_seg, n_chunks, n_groups = C // seq, 1, batch * seq // C
    else:
        assert seq % C == 0
        n_seg, n_chunks, n_groups = 1, seq // C, batch
    want = WKV_UNITS_WITH_STATE if has_state else WKV_UNITS
    nu = want if n_groups % want == 0 else 1
    seg_len = C // n_seg
    w2a2 = jnp.zeros((LANES, 2 * RW), F32).at[:lora_w, :RW].set(w2).at[lora_w:, RW:].set(a2).astype(BF16)
    t = np.arange(C)
    same_seg = (t[:, None] // seg_len) == (t[None, :] // seg_len)
    ltri = jnp.asarray(((t[:, None] >= t[None, :]) & same_seg).astype(np.float32), BF16)
    elast = jnp.asarray((t[None, :] == (t[:, None] // seg_len) * seg_len + seg_len - 1).astype(np.float32), BF16)
    ones_bd = jnp.asarray(np.kron(np.eye(MXU_WIDTH // RW_N, dtype=np.float32),
                                  np.ones((RW_N, RW_N), np.float32)), BF16)
    row = lambda x: x.reshape(1, -1).astype(F32)
    rw4 = rw.reshape(n_groups, n_chunks, C, n_shift)
    blk = lambda w: pl.BlockSpec((nu, 1, C, w), lambda i, c: (i, c, 0, 0))
    st_blk = pl.BlockSpec((nu * n_seg, RW_HEADS, RW_N, RW_N), lambda i, c: (i, 0, 0, 0))
    args, specs = [rw4], [blk(n_shift)]
    if has_state:
        p0 = jnp.pad(shift0[:, None, :], ((0, 0), (0, seq - 1), (0, 0))).reshape(n_groups, 1, C, n_shift)
        args += [p0, wkv0]
        specs += [blk(n_shift), st_blk]
    consts = [row(mu), w2a2, row(w0), row(a0), g2.astype(BF16), row(k_k), row(k_a), row(r_k), row(lnx_g),
              row(lnx_b), ltri] + ([elast] if n_seg > 1 else []) + [ones_bd]
    args += consts
    specs += [_resident(c.shape) for c in consts]
    scratch = [] if has_state else [pltpu.VMEM((nu, 1, n_shift), F32),
                                    pltpu.VMEM((nu, n_pairs, LANES, LANES), F32)]
    out, s_new = pl.pallas_call(
        functools.partial(_wkv_kernel, n_units=nu, n_seg=n_seg, has_state=has_state, lora_w=lora_w),
        grid=(n_groups // nu, n_chunks),
        in_specs=specs,
        out_specs=[blk(RW), st_blk],
        out_shape=[
            jax.ShapeDtypeStruct((n_groups, n_chunks, C, RW), BF16),
            jax.ShapeDtypeStruct((batch, RW_HEADS, RW_N, RW_N), F32),
        ],
        scratch_shapes=scratch,
        compiler_params=_cparams(("arbitrary", "arbitrary")),
        name="rwkv7",
    )(*args)
    return out.reshape(batch * seq, RW), s_new


def _branch_mix(x_ref, att_ref, rwo_ref, gate_ref, wpa_ref, wpb_ref, wo_ref):
    d = x_ref.shape[1]
    pa = jnp.dot(att_ref[...], wpa_ref[...], preferred_element_type=F32)
    pb = jnp.dot(rwo_ref[...], wpb_ref[...], preferred_element_type=F32)
    mix = gate_ref[:, :d] * pa + gate_ref[:, d:] * pb
    return x_ref[...] + jnp.dot(mix.astype(BF16), wo_ref[...], preferred_element_type=F32)


def _rms(x, g):
    ms = jnp.mean(x * x, axis=-1, keepdims=True)
    return x * lax.rsqrt(ms + NORM_EPS) * g


def _gelu_tanh(c):
    return c * (0.5 * (1.0 + jnp.tanh(math.sqrt(2.0 / math.pi) * (c + 0.044715 * (c * c * c)))))


def _conv_ffn_cols(h, x, wup_ref, cw_ref, cb_ref, wdown_ref, shifted, ug_sink):
    d_ff = cb_ref.shape[1]
    tiles = -(-d_ff // MXU_WIDTH)
    edges = [min(d_ff, MXU_WIDTH * ((tiles * j + FFN_COL_CHUNKS - 1) // FFN_COL_CHUNKS))
             for j in range(FFN_COL_CHUNKS + 1)]
    spans = list(zip(edges[:-1], edges[1:]))

    def up(span):
        lo_c, hi_c = span
        return (jnp.dot(h, wup_ref[:, lo_c:hi_c], preferred_element_type=F32),
                jnp.dot(h, wup_ref[:, d_ff + lo_c:d_ff + hi_c], preferred_element_type=F32))

    acc = x
    pending = up(spans[0])
    for j, (lo_c, hi_c) in enumerate(spans):
        cs = slice(lo_c, hi_c)
        ug_full, uv_full = pending
        if j + 1 < len(spans):
            pending = up(spans[j + 1])
        ug, ug_m1, ug_m2, uv = shifted(ug_full, uv_full, cs)
        c = cb_ref[:, cs] + cw_ref[0:1, cs] * ug_m2 + cw_ref[1:2, cs] * ug_m1 + cw_ref[2:3, cs] * ug
        act = (_gelu_tanh(c) * uv).astype(BF16)
        acc = acc + jnp.dot(act, wdown_ref[lo_c:hi_c, :], preferred_element_type=F32)
        ug_sink(ug, cs)
    return acc


def _ffn_prompt_kernel(x_ref, att_ref, rwo_ref, gate_ref, wpa_ref, wpb_ref, wo_ref, g2_ref, wup_ref, cw_ref,
                       cb_ref, wdown_ref, gf_ref, y_ref, ug_ref, carry_ref):
    tm = x_ref.shape[0]
    x = _branch_mix(x_ref, att_ref, rwo_ref, gate_ref, wpa_ref, wpb_ref, wo_ref)
    h = _rms(x, g2_ref[...]).astype(BF16)
    seq_start = pl.program_id(1) == 0

    def shifted(ug, uv, cs):
        before = jnp.where(seq_start, 0.0, carry_ref[:, cs])
        carry_ref[:, cs] = ug[tm - SUBLANES:]
        ug_e = jnp.concatenate([before, ug], axis=0)
        return ug, pltpu.roll(ug_e, 1, 0)[SUBLANES:], pltpu.roll(ug_e, 2, 0)[SUBLANES:], uv

    def ug_sink(ug, cs):
        ug_ref[:, cs] = ug[tm - SUBLANES:]

    x2 = _conv_ffn_cols(h, x, wup_ref, cw_ref, cb_ref, wdown_ref, shifted, ug_sink)
    y_ref[...] = _rms(x2, gf_ref[...])


def _ffn_sample_kernel(x_ref, att_ref, rwo_ref, gate_ref, e_ref, wpa_ref, wpb_ref, wo_ref, g2_ref, wup_ref,
                       cw_ref, cb_ref, wdown_ref, gf_ref, y_ref, ug_ref, *, seq):
    rows = x_ref.shape[0]
    x = _branch_mix(x_ref, att_ref, rwo_ref, gate_ref, wpa_ref, wpb_ref, wo_ref)
    h = _rms(x, g2_ref[...]).astype(BF16)
    t = lax.broadcasted_iota(jnp.int32, (rows, 1), 0) % seq

    def shifted(ug, uv, cs):
        e = e_ref[:, cs]
        ug_m1 = jnp.where(t == 0, pltpu.roll(e, rows - 1, 0), pltpu.roll(ug, 1, 0))
        ug_m2 = jnp.where(t < 2, e, pltpu.roll(ug, 2, 0))
        return ug, ug_m1, ug_m2, uv

    def ug_sink(ug, cs):
        ug_ref[:, cs] = ug

    x2 = _conv_ffn_cols(h, x, wup_ref, cw_ref, cb_ref, wdown_ref, shifted, ug_sink)
    y_ref[...] = _rms(x2, gf_ref[...])


def _ffn_prompt(x2d, att, rwo, gates, batch, seq, lw, final_g, tm):
    n, d = x2d.shape
    d_ff = lw["conv_b"].shape[0]
    nt = seq // tm
    rows = lambda a: pl.BlockSpec((tm, a.shape[1]), lambda b, i: (b * nt + i, 0))
    full = lambda a: _resident(a.shape)
    consts = [lw["w_pa"], lw["w_pb"], lw["w_o"], lw["norm2_g"].reshape(1, d), lw["w_up"], lw["conv_w"],
              lw["conv_b"].reshape(1, d_ff), lw["w_down"], final_g.reshape(1, d)]
    acts = [x2d, att, rwo, gates]
    y, ug_last = pl.pallas_call(
        _ffn_prompt_kernel,
        grid=(batch, nt),
        in_specs=[rows(a) for a in acts] + [full(c) for c in consts],
        out_specs=[
            pl.BlockSpec((tm, d), lambda b, i: (b * nt + i, 0)),
            pl.BlockSpec((SUBLANES, d_ff), lambda b, i: (b, 0)),
        ],
        out_shape=[
            jax.ShapeDtypeStruct((n, d), F32),
            jax.ShapeDtypeStruct((batch * SUBLANES, d_ff), F32),
        ],
        scratch_shapes=[pltpu.VMEM((SUBLANES, d_ff), F32)],
        compiler_params=_cparams(("arbitrary", "arbitrary")),
        name="mix_conv_ffn_prompt",
    )(*acts, *consts)
    conv_new = ug_last.reshape(batch, SUBLANES, d_ff)[:, SUBLANES - (CONV_W - 1):]
    return y, conv_new


def _ffn_sample(x2d, att, rwo, gates, batch, seq, conv0, lw, final_g, bt):
    n, d = x2d.shape
    d_ff = lw["conv_b"].shape[0]
    nrows = bt * seq
    e = jnp.pad(conv0, ((0, 0), (0, seq - (CONV_W - 1)), (0, 0))).reshape(n, d_ff)
    rows = lambda a: pl.BlockSpec((nrows, a.shape[1]), lambda i: (i, 0))
    full = lambda a: _resident(a.shape)
    consts = [lw["w_pa"], lw["w_pb"], lw["w_o"], lw["norm2_g"].reshape(1, d), lw["w_up"], lw["conv_w"],
              lw["conv_b"].reshape(1, d_ff), lw["w_down"], final_g.reshape(1, d)]
    acts = [x2d, att, rwo, gates, e]
    y, ug = pl.pallas_call(
        functools.partial(_ffn_sample_kernel, seq=seq),
        grid=(n // nrows,),
        in_specs=[rows(a) for a in acts] + [full(c) for c in consts],
        out_specs=[
            pl.BlockSpec((nrows, d), lambda i: (i, 0)),
            pl.BlockSpec((nrows, d_ff), lambda i: (i, 0)),
        ],
        out_shape=[jax.ShapeDtypeStruct((n, d), F32), jax.ShapeDtypeStruct((n, d_ff), F32)],
        compiler_params=_cparams(("parallel",)),
        name="mix_conv_ffn_sample",
    )(*acts, *consts)
    conv_new = ug.reshape(batch, seq, d_ff)[:, seq - (CONV_W - 1):]
    return y, conv_new


def _row_tile(n, want):
    t = min(want, n)
    while n % t:
        t //= 2
    return t


def _layer(x, is_prompt, state, rel_bias, lw, final_g):
    batch, seq, d = x.shape
    n = batch * seq
    x2d = x.reshape(n, d)
    n_qkv = Q_W + 2 * KV_W
    n_rw = lw["mu_shift"].shape[0]
    tm = _row_tile(n, min(1024, n // 4))
    qkv, rw, gates = _in_proj(x2d, lw["norm1_g"], lw["w_in"], n_qkv, n_rw, tm)
    if is_prompt:
        att, k_rows, v_rows = _attention_prompt(qkv, batch, seq, rel_bias, lw["sinks"], state["win_buf"])
        shift0 = wkv0 = None
    else:
        cache_k, cache_v = state["cache_k"], state["cache_v"]
        wb = cache_k.shape[1]
        assert seq <= wb
        att, k_rows, v_rows = _attention_sample(qkv, batch, seq, cache_k, cache_v, rel_bias, lw["sinks"],
                                                _row_tile(batch, 16))
        shift0, wkv0 = state["shift"], state["wkv"]
    k_rows = k_rows.reshape(batch, -1, N_KV, HEAD_DIM)
    v_rows = v_rows.reshape(batch, -1, N_KV, HEAD_DIM)
    rwo, wkv_new = _rwkv_mixer(rw, batch, seq, shift0, wkv0, lw["mu_shift"], lw["w0"], lw["w2"], lw["a0"],
                               lw["a2"], lw["g2"], lw["k_k"], lw["k_a"], lw["r_k"], lw["lnx_g"], lw["lnx_b"])
    shift_new = rw.reshape(batch, seq, n_rw)[:, seq - 1]
    if is_prompt:
        y, conv_new = _ffn_prompt(x2d, att, rwo, gates, batch, seq, lw, final_g, _row_tile(seq, FFN_ROWS))
    else:
        y, conv_new = _ffn_sample(x2d, att, rwo, gates, batch, seq, state["conv"], lw, final_g,
                                  _row_tile(batch, FFN_ROWS_WITH_STATE // seq))
    return y.reshape(batch, seq, d), (k_rows, v_rows, shift_new, wkv_new, conv_new)


def kernel(x_prompt, x_sample, cache_win_k, cache_win_v, state_shift, state_wkv, state_conv, rel_bias, norm1_g,
           w_in, sinks, mu_shift, w0, w2, a0, a2, g2, k_k, k_a, r_k, lnx_g, lnx_b, w_pa, w_pb, w_o, norm2_g,
           w_up, conv_w, conv_b, w_down, final_g):
    depth = w_in.shape[0]
    assert depth == 1, "the final norm is fused into the layer's last kernel"
    l = 0
    lw = dict(norm1_g=norm1_g[l], w_in=w_in[l].astype(BF16), sinks=sinks[l], mu_shift=mu_shift[l], w0=w0[l],
              w2=w2[l], a0=a0[l], a2=a2[l], g2=g2[l], k_k=k_k[l], k_a=k_a[l], r_k=r_k[l].reshape(-1),
              lnx_g=lnx_g[l], lnx_b=lnx_b[l], w_pa=w_pa[l].astype(BF16), w_pb=w_pb[l].astype(BF16),
              w_o=w_o[l].astype(BF16), norm2_g=norm2_g[l], w_up=w_up[l].astype(BF16), conv_w=conv_w[l],
              conv_b=conv_b[l], w_down=w_down[l].astype(BF16))
    win_buf = cache_win_k.shape[2]
    y_p, st_p = _layer(x_prompt, True, dict(win_buf=win_buf), rel_bias, lw, final_g)
    y_s, st_s = _layer(x_sample, False,
                       dict(cache_k=cache_win_k[l], cache_v=cache_win_v[l], shift=state_shift[l],
                            wkv=state_wkv[l], conv=state_conv[l]), rel_bias, lw, final_g)
    stack = lambda t: t[None]
    return (y_p, y_s) + tuple(stack(t) for t in st_p) + tuple(stack(t) for t in st_s)
```

```python
import functools
import math

import numpy as np
import jax
import jax.numpy as jnp
from jax import lax
from jax.experimental import pallas as pl
from jax.experimental.pallas import tpu as pltpu

F32 = jnp.float32
BF16 = jnp.bfloat16

HEAD_DIM = 64
N_HEADS = 8
N_KV = 2
WINDOW = 128
N_BUCKETS = 32
MAX_EXACT = N_BUCKETS // 2
REL_MAX_DIST = 128
RW_N = 64
RW_HEADS = 8
RW = RW_HEADS * RW_N
NORM_EPS = 1e-6
GN_EPS = 64e-5
NEG = -1e30
CONV_W = 3

Q_W = N_HEADS * HEAD_DIM
KV_W = N_KV * HEAD_DIM
LANES = 128
SUBLANES = 8
MXU_WIDTH = 256
CHUNK = 64
VMEM_LIMIT = 56 * 1024 * 1024
FFN_COL_CHUNKS = 3
FFN_ROWS = 512
FFN_ROWS_WITH_STATE = 256
ATTN_BLOCKS = 8
WKV_UNITS = 8
WKV_UNITS_WITH_STATE = 4


def _resident(shape):
    return pl.BlockSpec(shape, lambda *_: (0,) * len(shape), pipeline_mode=pl.Buffered(1))


def _cparams(sem):
    return pltpu.CompilerParams(dimension_semantics=sem, vmem_limit_bytes=VMEM_LIMIT)


def _sigmoid(x):
    return 1.0 / (1.0 + jnp.exp(-x))


def _dg(a, b, kind):
    if kind == "nn":
        dn = (((1,), (0,)), ((), ()))
    else:
        dn = (((1,), (1,)), ((), ()))
    return lax.dot_general(a, b, dn, preferred_element_type=F32)


def _mm(a, b, kind="nn"):
    return _dg(a.astype(BF16), b.astype(BF16), kind)


def _mm_exact_lhs(a_bf16, b, n_terms):
    out = None
    rem = b
    for _ in range(n_terms):
        piece = rem.astype(BF16)
        term = _dg(a_bf16, piece, "nn")
        out = term if out is None else out + term
        rem = rem - piece.astype(F32)
    return out


def _inproj_kernel(x_ref, g_ref, w_ref, qkv_ref, rw_ref, gate_ref, *, n_qkv, n_rw):
    x = x_ref[...]
    ms = jnp.mean(x * x, axis=-1, keepdims=True)
    h = (x * lax.rsqrt(ms + NORM_EPS) * g_ref[...]).astype(BF16)
    n_gate = gate_ref.shape[1]
    step = MXU_WIDTH
    plain = [(qkv_ref, c, c, min(step, n_qkv - c)) for c in range(0, n_qkv, step)]
    plain += [(rw_ref, c, n_qkv + c, min(step, n_rw - c)) for c in range(0, n_rw, step)]
    gated = [(gate_ref, c, n_qkv + n_rw + c, min(step, n_gate - c)) for c in range(0, n_gate, step)]
    pieces = []
    while plain or gated:
        pieces += [gated.pop(0)] if gated else []
        pieces += [plain.pop(0)] if plain else []
    dot = lambda p: jnp.dot(h, w_ref[:, p[2]:p[2] + p[3]], preferred_element_type=F32)
    pending = dot(pieces[0])
    for j, (ref, c0, _, width) in enumerate(pieces):
        out = pending
        if j + 1 < len(pieces):
            pending = dot(pieces[j + 1])
        if ref is gate_ref:
            out = _sigmoid(out)
        ref[:, c0:c0 + width] = out.astype(ref.dtype)


def _in_proj(x2d, g, w_bf16, n_qkv, n_rw, tm):
    n, d = x2d.shape
    n_gate = w_bf16.shape[1] - n_qkv - n_rw
    return pl.pallas_call(
        functools.partial(_inproj_kernel, n_qkv=n_qkv, n_rw=n_rw),
        grid=(n // tm,),
        in_specs=[
            pl.BlockSpec((tm, d), lambda i: (i, 0)),
            _resident((1, d)),
            _resident(w_bf16.shape),
        ],
        out_specs=[
            pl.BlockSpec((tm, n_qkv), lambda i: (i, 0)),
            pl.BlockSpec((tm, n_rw), lambda i: (i, 0)),
            pl.BlockSpec((tm, n_gate), lambda i: (i, 0)),
        ],
        out_shape=[
            jax.ShapeDtypeStruct((n, n_qkv), F32),
            jax.ShapeDtypeStruct((n, n_rw), F32),
            jax.ShapeDtypeStruct((n, n_gate), BF16),
        ],
        compiler_params=_cparams(("parallel",)),
        name="in_proj",
    )(x2d, g.reshape(1, d), w_bf16)


def _t5_bucket_np(dist):
    n = np.maximum(dist, 0)
    nf = np.maximum(n, 1).astype(np.float32)
    large = MAX_EXACT + (np.log(nf / MAX_EXACT) / math.log(REL_MAX_DIST / MAX_EXACT)
                         * (N_BUCKETS - MAX_EXACT)).astype(np.int32)
    return np.where(n < MAX_EXACT, n, np.minimum(large, N_BUCKETS - 1)).astype(np.int32)


def _attn_kernel(q_ref, k1_ref, k2_ref, v1_ref, v2_ref, bucket_ref, relb_ref, sink_ref, o_ref, *rest,
                 nq, nk, n_blocks, first_block_axis, emit_window):
    bt = q_ref.shape[0]
    bias_ref = rest[-1]
    if emit_window == "tail":
        @pl.when(pl.program_id(1) == pl.num_programs(1) - 1)
        def _():
            n_new, n_win = k2_ref.shape[1], rest[0].shape[1]
            rest[0][...] = k2_ref[:, n_new - n_win:]
            rest[1][...] = v2_ref[:, n_new - n_win:]
    else:
        n_old, n_new = k1_ref.shape[1], k2_ref.shape[1]
        for w_ref, old_ref, new_ref in ((rest[0], k1_ref, k2_ref), (rest[1], v1_ref, v2_ref)):
            w_ref[:, :n_old - n_new] = old_ref[:, n_new:]
            w_ref[:, n_old - n_new:] = new_ref[...]
    first_step = pl.program_id(0) == 0
    if first_block_axis is not None:
        first_step = jnp.logical_and(first_step, pl.program_id(1) == 0)

    @pl.when(first_step)
    def _():
        bucket = bucket_ref[...]
        prev_key = lax.broadcasted_iota(jnp.int32, (1, nk), 1) < (nk // 2)
        for n in range(N_HEADS):
            acc = jnp.full((nq, nk), NEG, F32)
            for b in range(N_BUCKETS):
                acc = jnp.where(bucket == b, relb_ref[b, n], acc)
            c, half = divmod(n, 2)
            bias_ref[0, c, :, half * nk:(half + 1) * nk] = acc
            if first_block_axis is not None:
                bias_ref[1, c, :, half * nk:(half + 1) * nk] = jnp.where(prev_key, NEG, acc)

    n_keys = nk + (n_blocks - 1) * nq

    def padded(a_ref, b_ref):
        parts = [a_ref[...], b_ref[...]]
        n_now = a_ref.shape[1] + b_ref.shape[1]
        if n_now < n_keys:
            parts.append(jnp.zeros((bt, n_keys - n_now, LANES), F32))
        return jnp.concatenate(parts, axis=1)

    kk = padded(k1_ref, k2_ref)
    vv = padded(v1_ref, v2_ref)
    kk_r = pltpu.roll(kk, HEAD_DIM, 2)
    vv_r = pltpu.roll(vv, HEAD_DIM, 2)
    lane = lax.broadcasted_iota(jnp.int32, (1, 1, LANES), 2)
    lo = lane < HEAD_DIM

    def halves(x, x_r, kvh):
        src_lo, src_hi = (x, x_r) if kvh == 0 else (x_r, x)
        even = jnp.where(lo, src_lo, 0.0).astype(BF16)
        odd = jnp.where(lo, 0.0, src_hi).astype(BF16)
        return even, odd

    k_eo = [halves(kk, kk_r, h) for h in range(N_KV)]
    v_eo = [halves(vv, vv_r, h) for h in range(N_KV)]

    def window(eo, j):
        return jnp.concatenate([eo[0][:, j * nq:j * nq + nk], eo[1][:, j * nq:j * nq + nk]], axis=1)

    n_cols = N_HEADS // 2
    units = [(j, c) for j in range(n_blocks) for c in range(n_cols)]
    kvh_of = lambda c: (2 * c) // (N_HEADS // N_KV)
    k_win = {(j, h): window(k_eo[h], j) for j in range(n_blocks) for h in range(N_KV)}
    v_win = {(j, h): window(v_eo[h], j) for j in range(n_blocks) for h in range(N_KV)}
    sums_on_mxu = bt == 1
    if sums_on_mxu:
        key_row = lax.broadcasted_iota(jnp.int32, (1, 2 * nk, 1), 1)
        ones_cols = jnp.where((key_row < nk) == lo, 1.0, 0.0).astype(BF16)
        v_win = {jh: jnp.concatenate([v, ones_cols], axis=2) for jh, v in v_win.items()}

    scale = HEAD_DIM ** -0.5
    first_table = 0
    if first_block_axis is not None:
        first_table = jnp.where(pl.program_id(first_block_axis) == 0, 1, 0)
    s_all = {}
    for j, c in units:
        qc = (q_ref[:, j * nq:(j + 1) * nq, c * LANES:(c + 1) * LANES] * scale).astype(BF16)
        s_all[j, c] = jnp.einsum("bqd,bkd->bqk", qc, k_win[j, kvh_of(c)], preferred_element_type=F32)
    e_all, sink_all = {}, {}
    for j, c in units:
        s = s_all[j, c] + bias_ref[first_table if j == 0 else 0, c][None]
        es, sink_terms = [], []
        for half in range(2):
            sh = s[:, :, half * nk:(half + 1) * nk]
            sink = sink_ref[2 * c + half]
            m = jnp.maximum(jnp.max(sh, axis=-1, keepdims=True), sink)
            e = jnp.exp(sh - m)
            es.append(e.astype(BF16))
            sink_terms.append(jnp.exp(sink - m) + (0.0 if sums_on_mxu else jnp.sum(e, axis=-1, keepdims=True)))
        e_all[j, c] = jnp.concatenate(es, axis=2)
        sink_all[j, c] = jnp.where(lo, sink_terms[0], sink_terms[1])
    for j, c in units:
        o = jnp.einsum("bqk,bkd->bqd", e_all[j, c], v_win[j, kvh_of(c)], preferred_element_type=F32)
        denom = sink_all[j, c] + (o[:, :, LANES:] if sums_on_mxu else 0.0)
        o = (o[:, :, :LANES] / denom)
        if len(o_ref.shape) == 2:
            o_ref[:, c * LANES:(c + 1) * LANES] = o.reshape(bt * nq, LANES).astype(o_ref.dtype)
        else:
            o_ref[:, j * nq:(j + 1) * nq, c * LANES:(c + 1) * LANES] = o.astype(o_ref.dtype)


def _attention_prompt(qkv, batch, seq, rel_bias, sinks, win_buf):
    nblk = seq // WINDOW
    nb = ATTN_BLOCKS if nblk % ATTN_BLOCKS == 0 else 1
    nsteps = nblk // nb
    width = qkv.shape[1]
    q_blk = qkv.reshape(batch * nblk, WINDOW, width)
    q_step = qkv.reshape(batch * nsteps, nb * WINDOW, width)
    kcol = Q_W // LANES
    vcol = (Q_W + KV_W) // LANES
    nk = 2 * WINDOW
    qi = np.arange(WINDOW)[:, None] + WINDOW
    kj = np.arange(nk)[None, :]
    dist = qi - kj
    bucket = np.where((dist >= 0) & (dist < WINDOW), _t5_bucket_np(dist), -1).astype(np.int32)
    cur = lambda c: (lambda b, i: (b * nsteps + i, 0, c))
    prev = lambda c: (lambda b, i: (b * nblk + jnp.maximum(i * nb - 1, 0), 0, c))
    assert win_buf <= nb * WINDOW
    win_spec = pl.BlockSpec((1, win_buf, LANES), lambda b, i: (b, 0, 0))
    win_shape = jax.ShapeDtypeStruct((batch, win_buf, KV_W), F32)
    out, win_k, win_v = pl.pallas_call(
        functools.partial(_attn_kernel, nq=WINDOW, nk=nk, n_blocks=nb, first_block_axis=1, emit_window="tail"),
        grid=(batch, nsteps),
        in_specs=[
            pl.BlockSpec((1, nb * WINDOW, Q_W), cur(0)),
            pl.BlockSpec((1, WINDOW, LANES), prev(kcol)),
            pl.BlockSpec((1, nb * WINDOW, LANES), cur(kcol)),
            pl.BlockSpec((1, WINDOW, LANES), prev(vcol)),
            pl.BlockSpec((1, nb * WINDOW, LANES), cur(vcol)),
            _resident(bucket.shape),
            pl.BlockSpec(memory_space=pltpu.SMEM),
            pl.BlockSpec(memory_space=pltpu.SMEM),
        ],
        out_specs=[pl.BlockSpec((1, nb * WINDOW, Q_W), cur(0)), win_spec, win_spec],
        out_shape=[jax.ShapeDtypeStruct((batch * nsteps, nb * WINDOW, Q_W), BF16), win_shape, win_shape],
        scratch_shapes=[pltpu.VMEM((2, N_HEADS // 2, WINDOW, 2 * nk), F32)],
        compiler_params=_cparams(("arbitrary", "arbitrary")),
        name="attn_prompt",
    )(q_step, q_blk, q_step, q_blk, q_step, jnp.asarray(bucket), rel_bias, sinks)
    return out.reshape(batch * seq, Q_W), win_k, win_v


def _attention_sample(qkv, batch, seq, cache_k, cache_v, rel_bias, sinks, bt):
    wb = cache_k.shape[1]
    nk = 2 * WINDOW
    q3 = qkv.reshape(batch, seq, qkv.shape[1])
    ck = cache_k.reshape(batch, wb, KV_W)
    cv = cache_v.reshape(batch, wb, KV_W)
    kcol = Q_W // LANES
    vcol = (Q_W + KV_W) // LANES
    tq = np.arange(seq)[:, None]
    j = np.arange(nk)[None, :]
    dist = np.where(j < wb, tq + wb - j, tq - (j - wb))
    ok = (dist >= 0) & (dist < WINDOW) & (j < wb + seq)
    bucket = np.where(ok, _t5_bucket_np(dist), -1).astype(np.int32)
    win_spec = pl.BlockSpec((bt, wb, LANES), lambda b: (b, 0, 0))
    win_shape = jax.ShapeDtypeStruct((batch, wb, KV_W), F32)
    out, win_k, win_v = pl.pallas_call(
        functools.partial(_attn_kernel, nq=seq, nk=nk, n_blocks=1, first_block_axis=None, emit_window="shift"),
        grid=(batch // bt,),
        in_specs=[
            pl.BlockSpec((bt, seq, Q_W), lambda b: (b, 0, 0)),
            pl.BlockSpec((bt, wb, LANES), lambda b: (b, 0, 0)),
            pl.BlockSpec((bt, seq, LANES), lambda b: (b, 0, kcol)),
            pl.BlockSpec((bt, wb, LANES), lambda b: (b, 0, 0)),
            pl.BlockSpec((bt, seq, LANES), lambda b: (b, 0, vcol)),
            _resident(bucket.shape),
            pl.BlockSpec(memory_space=pltpu.SMEM),
            pl.BlockSpec(memory_space=pltpu.SMEM),
        ],
        out_specs=[pl.BlockSpec((bt * seq, Q_W), lambda b: (b, 0)), win_spec, win_spec],
        out_shape=[jax.ShapeDtypeStruct((batch * seq, Q_W), BF16), win_shape, win_shape],
        scratch_shapes=[pltpu.VMEM((1, N_HEADS // 2, seq, 2 * nk), F32)],
        compiler_params=_cparams(("arbitrary",)),
        name="attn_sample",
    )(q3, ck, q3, cv, q3, jnp.asarray(bucket), rel_bias, sinks)
    return out, win_k, win_v


def _wkv_kernel(*refs, n_units, n_seg, has_state, lora_w):
    C = CHUNK
    seg_len = C // n_seg
    n_pairs = RW // LANES
    it = iter(refs)
    p_ref = next(it)
    p0_ref, s0_ref = (next(it), next(it)) if has_state else (None, None)
    (mu_ref, w2a2_ref, w0_ref, a0_ref, g2_ref, kk_ref, ka_ref, rk_ref, lng_ref, lnb_ref, ltri_ref) = (
        next(it) for _ in range(11))
    elast_ref = next(it) if n_seg > 1 else None
    ones_ref, out_ref, sout_ref = next(it), next(it), next(it)
    last_ref, sbd_ref = (None, None) if has_state else (next(it), next(it))

    if not has_state:
        @pl.when(pl.program_id(1) == 0)
        def _():
            last_ref[...] = jnp.zeros_like(last_ref)
            sbd_ref[...] = jnp.zeros_like(sbd_ref)

    row = lax.broadcasted_iota(jnp.int32, (C, 1), 0)
    rows_of = lambda u: slice(u * C, (u + 1) * C)
    lane = lax.broadcasted_iota(jnp.int32, (1, LANES), 1)
    lo = lane < RW_N
    ones_bd = ones_ref[...]

    def headsum(x):
        xb = x.astype(BF16)
        return jnp.concatenate([_dg(xb[:, c:c + MXU_WIDTH], ones_bd, "nn") for c in range(0, RW, MXU_WIDTH)],
                               axis=1)

    def prepare(us):
        xs_parts = []
        for u in us:
            p = p_ref[u, 0]
            rolled = pltpu.roll(p, 1, 0)
            if has_state:
                prev = jnp.where(row % seg_len == 0, p0_ref[u, 0], rolled)
            else:
                prev = jnp.where(row == 0, last_ref[u], rolled)
                last_ref[u] = p_ref[u, 0, C - 1:C, :]
            xs_parts.append(p + mu_ref[...] * (prev - p))
        xs = jnp.concatenate(xs_parts, axis=0)
        r = xs[:, 0:RW]
        k = xs[:, RW:2 * RW]
        v = xs[:, 2 * RW:3 * RW]
        lwla = xs[:, 3 * RW:3 * RW + LANES]
        lg = xs[:, 3 * RW + LANES:3 * RW + 2 * LANES]
        lwla = jnp.where(lane < lora_w, jnp.tanh(lwla), lwla)
        wa = jnp.dot(lwla.astype(BF16), w2a2_ref[...], preferred_element_type=F32)
        logw = -math.exp(-0.5) * _sigmoid(w0_ref[...] + wa[:, :RW])
        a_sig = _sigmoid(a0_ref[...] + wa[:, RW:])
        g = jnp.dot(_sigmoid(lg).astype(BF16), g2_ref[...], preferred_element_type=F32)
        kk = k * kk_ref[...]
        kk = kk * (1.0 / jnp.maximum(jnp.sqrt(headsum(kk * kk)), 1e-12))
        k = k * (1.0 + (a_sig - 1.0) * ka_ref[...])
        a = -kk
        b = kk * a_sig
        bonus = headsum(r * k * rk_ref[...]) * v
        cws, cwl = [], []
        for i in range(len(us)):
            cw_u = _mm_exact_lhs(ltri_ref[...], logw[rows_of(i)], 3)
            cws.append(cw_u)
            if n_seg == 1:
                cwl.append(jnp.broadcast_to(cw_u[C - 1:C, :], (C, RW)))
            else:
                cwl.append(_mm_exact_lhs(elast_ref[...], cw_u, 3))
        cw = jnp.concatenate(cws, axis=0)
        cw_last = jnp.concatenate(cwl, axis=0)
        w_inv = jnp.exp(-cw)
        b_t, k_t = b * w_inv, k * w_inv
        w_tail = jnp.exp(cw_last - cw)
        b_h, k_h = b * w_tail, k * w_tail
        a_t, r_t = a * jnp.exp(cw - logw), r * jnp.exp(cw)
        if n_seg == 1:
            w_c = jnp.exp(jnp.concatenate([cw_u[C - 1:C, :] for cw_u in cws], axis=0))
        else:
            w_c = jnp.exp(cw_last)
        return a_t, r_t, b_t, k_t, b_h, k_h, v, w_c, bonus, g

    def bd(y):
        return jnp.concatenate([jnp.where(lo, y, 0.0), jnp.where(lo, 0.0, y)], axis=0)

    zeros_head = jnp.zeros((RW_N, RW_N), F32)

    def pack_pair(s_even, s_odd):
        return jnp.concatenate([jnp.concatenate([s_even, zeros_head], axis=1),
                                jnp.concatenate([zeros_head, s_odd], axis=1)], axis=0)

    def store_pair(ref, i, q, s_pair):
        ref[i, 2 * q] = s_pair[:RW_N, :RW_N]
        ref[i, 2 * q + 1] = s_pair[RW_N:, RW_N:]

    s_idx = lane % C
    strict = s_idx < row
    incl = s_idx <= row
    if n_seg > 1:
        same_seg = (s_idx // seg_len) == (row // seg_len)
        strict = jnp.logical_and(strict, same_seg)
        incl = jnp.logical_and(incl, same_seg)
    row2 = lax.broadcasted_iota(jnp.int32, (2 * C, 1), 0)
    same_head = (row2 < RW_N) == lo

    part = lambda x, i, q: x[i * C:(i + 1) * C, q * LANES:(q + 1) * LANES]

    def recurrence(us, ops):
        a_t, r_t, b_t, k_t, b_h, k_h, v, w_c = ops[:8]
        units = [(i, q) for i in range(len(us)) for q in range(n_pairs)]
        sc = {uq: _mm(jnp.concatenate([part(a_t, *uq), part(r_t, *uq)], axis=0),
                      jnp.concatenate([bd(part(b_t, *uq)), bd(part(k_t, *uq))], axis=0),
                      "nt") for uq in units}
        pw = {uq: jnp.where(strict, sc[uq][:C, :LANES], 0.0) for uq in units}
        m_rb = {uq: jnp.where(incl, sc[uq][C:, :LANES], 0.0) for uq in units}
        lm_v = {uq: _mm(jnp.concatenate([jnp.where(strict, sc[uq][:C, LANES:], 0.0),
                                         jnp.where(incl, sc[uq][C:, LANES:], 0.0)], axis=0),
                        bd(part(v, *uq)), "nn") for uq in units}
        tinv = {uq: pw[uq] + jnp.where(s_idx == row, 1.0, 0.0) for uq in units}
        n_lvl = int(math.log2(seg_len))
        for lvl in range(1, n_lvl):
            last = lvl + 1 == n_lvl
            if lvl == 1:
                for uq in units:
                    pw[uq] = _mm(pw[uq], bd(pw[uq]), "nn")
            for uq in units:
                rhs = bd(tinv[uq]) if last else jnp.concatenate([bd(tinv[uq]), bd(pw[uq])], axis=1)
                upd = _mm(pw[uq], rhs, "nn")
                tinv[uq] = tinv[uq] + upd[:, :LANES]
                if not last:
                    pw[uq] = upd[:, LANES:]
        x = {uq: _mm(tinv[uq], jnp.concatenate([bd(part(a_t, *uq)), bd(lm_v[uq][:C])], axis=1), "nn")
             for uq in units}
        a_hat = {uq: x[uq][:, :LANES] for uq in units}
        v_hat = {uq: x[uq][:, LANES:] for uq in units}
        z = {uq: _mm(m_rb[uq], jnp.concatenate([bd(a_hat[uq]), bd(v_hat[uq])], axis=1), "nn")
             for uq in units}
        r_hat = {uq: part(r_t, *uq) + z[uq][:, :LANES] for uq in units}
        y_intra = {uq: z[uq][:, LANES:] + lm_v[uq][C:] for uq in units}
        ys = {}
        if n_seg == 1:
            s_old = {(i, q): sbd_ref[us[i], q] for i, q in units}
            t1 = {uq: _mm(jnp.concatenate([a_hat[uq], r_hat[uq]], axis=0), s_old[uq], "nt")
                  for uq in units}
            for uq in units:
                ys[uq] = t1[uq][C:] + y_intra[uq]
                uv = jnp.concatenate([t1[uq][:C] + v_hat[uq], part(v, *uq)], axis=0)
                bkh = jnp.concatenate([part(b_h, *uq), part(k_h, *uq)], axis=0)
                ds = _mm(uv.T, bkh, "nn")
                w_cq = w_c[uq[0]:uq[0] + 1, uq[1] * LANES:(uq[1] + 1) * LANES]
                sbd_ref[us[uq[0]], uq[1]] = s_old[uq] * w_cq + jnp.where(same_head, ds, 0.0)
        else:
            row_seg = (row2 % C) // seg_len
            for uq in units:
                i, q = uq
                first_seq = us[i] * n_seg
                u_parts, y_parts, s_olds = [], [], []
                for sg in range(n_seg):
                    rs = slice(sg * seg_len, (sg + 1) * seg_len)
                    s_sg = pack_pair(s0_ref[first_seq + sg, 2 * q], s0_ref[first_seq + sg, 2 * q + 1])
                    t1 = _mm(jnp.concatenate([a_hat[uq][rs], r_hat[uq][rs]], axis=0), s_sg, "nt")
                    u_parts.append(t1[:seg_len] + v_hat[uq][rs])
                    y_parts.append(t1[seg_len:] + y_intra[uq][rs])
                    s_olds.append(s_sg)
                ys[uq] = jnp.concatenate(y_parts, axis=0)
                uv_t = jnp.concatenate(u_parts + [part(v, *uq)], axis=0).T
                bkh = jnp.concatenate([part(b_h, *uq), part(k_h, *uq)], axis=0)
                w_cq = part(w_c, *uq)
                for sg in range(n_seg):
                    ds = _mm(uv_t, jnp.where(row_seg == sg, bkh, 0.0), "nn")
                    store_pair(sout_ref, first_seq + sg, q,
                               s_olds[sg] * w_cq[sg * seg_len:sg * seg_len + 1] + jnp.where(same_head, ds, 0.0))
        return ys

    def finish(us, ys, ops):
        bonus, g = ops[8:]
        y = jnp.concatenate([jnp.concatenate([ys[i, q] for q in range(n_pairs)], axis=1)
                             for i in range(len(us))], axis=0)
        mean = headsum(y) * (1.0 / RW_N)
        d = y - mean
        var = headsum(d * d) * (1.0 / RW_N)
        y = d * lax.rsqrt(var + GN_EPS) * lng_ref[...] + lnb_ref[...]
        y = ((y + bonus) * g).astype(out_ref.dtype)
        for i, u in enumerate(us):
            out_ref[u, 0] = y[rows_of(i)]

    all_units = list(range(n_units))
    ops = prepare(all_units)
    finish(all_units, recurrence(all_units, ops), ops)
    units = [(u, q) for u in range(n_units) for q in range(n_pairs)]

    if not has_state:
        @pl.when(pl.program_id(1) == pl.num_programs(1) - 1)
        def _():
            for u, q in units:
                store_pair(sout_ref, u, q, sbd_ref[u, q])


def _rwkv_mixer(rw, batch, seq, shift0, wkv0, mu, w0, w2, a0, a2, g2, k_k, k_a, r_k, lnx_g, lnx_b):
    n_shift = rw.shape[1]
    lora_w, lora_a = w2.shape[0], a2.shape[0]
    assert lora_w + lora_a == LANES and g2.shape[0] == LANES and n_shift == 3 * RW + 2 * LANES
    has_state = shift0 is not None
    C = CHUNK
    n_pairs = RW // LANES
    if has_state:
        assert C % seq == 0 and batch % (C // seq) == 0
        n_seg, n_chunks, n_groups = C // seq, 1, batch * seq // C
    else:
        assert seq % C == 0
        n_seg, n_chunks, n_groups = 1, seq // C, batch
    want = WKV_UNITS_WITH_STATE if has_state else WKV_UNITS
    nu = want if n_groups % want == 0 else 1
    seg_len = C // n_seg
    w2a2 = jnp.zeros((LANES, 2 * RW), F32).at[:lora_w, :RW].set(w2).at[lora_w:, RW:].set(a2).astype(BF16)
    t = np.arange(C)
    same_seg = (t[:, None] // seg_len) == (t[None, :] // seg_len)
    ltri = jnp.asarray(((t[:, None] >= t[None, :]) & same_seg).astype(np.float32), BF16)
    elast = jnp.asarray((t[None, :] == (t[:, None] // seg_len) * seg_len + seg_len - 1).astype(np.float32), BF16)
    ones_bd = jnp.asarray(np.kron(np.eye(MXU_WIDTH // RW_N, dtype=np.float32),
                                  np.ones((RW_N, RW_N), np.float32)), BF16)
    row = lambda x: x.reshape(1, -1).astype(F32)
    rw4 = rw.reshape(n_groups, n_chunks, C, n_shift)
    blk = lambda w: pl.BlockSpec((nu, 1, C, w), lambda i, c: (i, c, 0, 0))
    st_blk = pl.BlockSpec((nu * n_seg, RW_HEADS, RW_N, RW_N), lambda i, c: (i, 0, 0, 0))
    args, specs = [rw4], [blk(n_shift)]
    if has_state:
        p0 = jnp.pad(shift0[:, None, :], ((0, 0), (0, seq - 1), (0, 0))).reshape(n_groups, 1, C, n_shift)
        args += [p0, wkv0]
        specs += [blk(n_shift), st_blk]
    consts = [row(mu), w2a2, row(w0), row(a0), g2.astype(BF16), row(k_k), row(k_a), row(r_k), row(lnx_g),
              row(lnx_b), ltri] + ([elast] if n_seg > 1 else []) + [ones_bd]
    args += consts
    specs += [_resident(c.shape) for c in consts]
    scratch = [] if has_state else [pltpu.VMEM((nu, 1, n_shift), F32),
                                    pltpu.VMEM((nu, n_pairs, LANES, LANES), F32)]
    out, s_new = pl.pallas_call(
        functools.partial(_wkv_kernel, n_units=nu, n_seg=n_seg, has_state=has_state, lora_w=lora_w),
        grid=(n_groups // nu, n_chunks),
        in_specs=specs,
        out_specs=[blk(RW), st_blk],
        out_shape=[
            jax.ShapeDtypeStruct((n_groups, n_chunks, C, RW), BF16),
            jax.ShapeDtypeStruct((batch, RW_HEADS, RW_N, RW_N), F32),
        ],
        scratch_shapes=scratch,
        compiler_params=_cparams(("arbitrary", "arbitrary")),
        name="rwkv7",
    )(*args)
    return out.reshape(batch * seq, RW), s_new


def _branch_mix(x_ref, att_ref, rwo_ref, gate_ref, wpa_ref, wpb_ref, wo_ref):
    d = x_ref.shape[1]
    att, rwo = att_ref[...], rwo_ref[...]

    def branch_dots(c):
        return (jnp.dot(att, wpa_ref[:, c:c + MXU_WIDTH], preferred_element_type=F32),
                jnp.dot(rwo, wpb_ref[:, c:c + MXU_WIDTH], preferred_element_type=F32))

    mixes = []
    pending = branch_dots(0)
    for c in range(0, d, MXU_WIDTH):
        pa, pb = pending
        if c + MXU_WIDTH < d:
            pending = branch_dots(c + MXU_WIDTH)
        mixes.append((gate_ref[:, c:c + MXU_WIDTH] * pa + gate_ref[:, d + c:d + c + MXU_WIDTH] * pb).astype(BF16))
    mix = jnp.concatenate(mixes, axis=1)
    return x_ref[...] + jnp.dot(mix, wo_ref[...], preferred_element_type=F32)


def _rms(x, g):
    ms = jnp.mean(x * x, axis=-1, keepdims=True)
    return x * lax.rsqrt(ms + NORM_EPS) * g


def _gelu_tanh(c):
    return c * (0.5 * (1.0 + jnp.tanh(math.sqrt(2.0 / math.pi) * (c + 0.044715 * (c * c * c)))))


def _conv_ffn_cols(h, x, wup_ref, cw_ref, cb_ref, wdown_ref, shifted, ug_sink):
    d_ff = cb_ref.shape[1]
    tiles = -(-d_ff // MXU_WIDTH)
    edges = [min(d_ff, MXU_WIDTH * ((tiles * j + FFN_COL_CHUNKS - 1) // FFN_COL_CHUNKS))
             for j in range(FFN_COL_CHUNKS + 1)]
    spans = list(zip(edges[:-1], edges[1:]))

    def up(span):
        lo_c, hi_c = span
        return (jnp.dot(h, wup_ref[:, lo_c:hi_c], preferred_element_type=F32),
                jnp.dot(h, wup_ref[:, d_ff + lo_c:d_ff + hi_c], preferred_element_type=F32))

    acc = x
    pending = up(spans[0])
    for j, (lo_c, hi_c) in enumerate(spans):
        cs = slice(lo_c, hi_c)
        ug_full, uv_full = pending
        if j + 1 < len(spans):
            pending = up(spans[j + 1])
        ug, ug_m1, ug_m2, uv = shifted(ug_full, uv_full, cs)
        c = cb_ref[:, cs] + cw_ref[0:1, cs] * ug_m2 + cw_ref[1:2, cs] * ug_m1 + cw_ref[2:3, cs] * ug
        act = (_gelu_tanh(c) * uv).astype(BF16)
        acc = acc + jnp.dot(act, wdown_ref[lo_c:hi_c, :], preferred_element_type=F32)
        ug_sink(ug, cs)
    return acc


def _ffn_prompt_kernel(x_ref, att_ref, rwo_ref, gate_ref, wpa_ref, wpb_ref, wo_ref, g2_ref, wup_ref, cw_ref,
                       cb_ref, wdown_ref, gf_ref, y_ref, ug_ref, carry_ref):
    tm = x_ref.shape[0]
    x = _branch_mix(x_ref, att_ref, rwo_ref, gate_ref, wpa_ref, wpb_ref, wo_ref)
    h = _rms(x, g2_ref[...]).astype(BF16)
    seq_start = pl.program_id(1) == 0

    def shifted(ug, uv, cs):
        before = jnp.where(seq_start, 0.0, carry_ref[:, cs])
        carry_ref[:, cs] = ug[tm - SUBLANES:]
        ug_e = jnp.concatenate([before, ug], axis=0)
        return ug, pltpu.roll(ug_e, 1, 0)[SUBLANES:], pltpu.roll(ug_e, 2, 0)[SUBLANES:], uv

    def ug_sink(ug, cs):
        ug_ref[:, cs] = ug[tm - SUBLANES:]

    x2 = _conv_ffn_cols(h, x, wup_ref, cw_ref, cb_ref, wdown_ref, shifted, ug_sink)
    y_ref[...] = _rms(x2, gf_ref[...])


def _ffn_sample_kernel(x_ref, att_ref, rwo_ref, gate_ref, e_ref, wpa_ref, wpb_ref, wo_ref, g2_ref, wup_ref,
                       cw_ref, cb_ref, wdown_ref, gf_ref, y_ref, ug_ref, *, seq):
    rows = x_ref.shape[0]
    x = _branch_mix(x_ref, att_ref, rwo_ref, gate_ref, wpa_ref, wpb_ref, wo_ref)
    h = _rms(x, g2_ref[...]).astype(BF16)
    t = lax.broadcasted_iota(jnp.int32, (rows, 1), 0) % seq

    def shifted(ug, uv, cs):
        e = e_ref[:, cs]
        ug_m1 = jnp.where(t == 0, pltpu.roll(e, rows - 1, 0), pltpu.roll(ug, 1, 0))
        ug_m2 = jnp.where(t < 2, e, pltpu.roll(ug, 2, 0))
        return ug, ug_m1, ug_m2, uv

    def ug_sink(ug, cs):
        ug_ref[:, cs] = ug

    x2 = _conv_ffn_cols(h, x, wup_ref, cw_ref, cb_ref, wdown_ref, shifted, ug_sink)
    y_ref[...] = _rms(x2, gf_ref[...])


def _ffn_prompt(x2d, att, rwo, gates, batch, seq, lw, final_g, tm):
    n, d = x2d.shape
    d_ff = lw["conv_b"].shape[0]
    nt = seq // tm
    rows = lambda a: pl.BlockSpec((tm, a.shape[1]), lambda b, i: (b * nt + i, 0))
    full = lambda a: _resident(a.shape)
    consts = [lw["w_pa"], lw["w_pb"], lw["w_o"], lw["norm2_g"].reshape(1, d), lw["w_up"], lw["conv_w"],
              lw["conv_b"].reshape(1, d_ff), lw["w_down"], final_g.reshape(1, d)]
    acts = [x2d, att, rwo, gates]
    y, ug_last = pl.pallas_call(
        _ffn_prompt_kernel,
        grid=(batch, nt),
        in_specs=[rows(a) for a in acts] + [full(c) for c in consts],
        out_specs=[
            pl.BlockSpec((tm, d), lambda b, i: (b * nt + i, 0)),
            pl.BlockSpec((SUBLANES, d_ff), lambda b, i: (b, 0)),
        ],
        out_shape=[
            jax.ShapeDtypeStruct((n, d), F32),
            jax.ShapeDtypeStruct((batch * SUBLANES, d_ff), F32),
        ],
        scratch_shapes=[pltpu.VMEM((SUBLANES, d_ff), F32)],
        compiler_params=_cparams(("arbitrary", "arbitrary")),
        name="mix_conv_ffn_prompt",
    )(*acts, *consts)
    conv_new = ug_last.reshape(batch, SUBLANES, d_ff)[:, SUBLANES - (CONV_W - 1):]
    return y, conv_new


def _ffn_sample(x2d, att, rwo, gates, batch, seq, conv0, lw, final_g, bt):
    n, d = x2d.shape
    d_ff = lw["conv_b"].shape[0]
    nrows = bt * seq
    e = jnp.pad(conv0, ((0, 0), (0, seq - (CONV_W - 1)), (0, 0))).reshape(n, d_ff)
    rows = lambda a: pl.BlockSpec((nrows, a.shape[1]), lambda i: (i, 0))
    full = lambda a: _resident(a.shape)
    consts = [lw["w_pa"], lw["w_pb"], lw["w_o"], lw["norm2_g"].reshape(1, d), lw["w_up"], lw["conv_w"],
              lw["conv_b"].reshape(1, d_ff), lw["w_down"], final_g.reshape(1, d)]
    acts = [x2d, att, rwo, gates, e]
    y, ug = pl.pallas_call(
        functools.partial(_ffn_sample_kernel, seq=seq),
        grid=(n // nrows,),
        in_specs=[rows(a) for a in acts] + [full(c) for c in consts],
        out_specs=[
            pl.BlockSpec((nrows, d), lambda i: (i, 0)),
            pl.BlockSpec((nrows, d_ff), lambda i: (i, 0)),
        ],
        out_shape=[jax.ShapeDtypeStruct((n, d), F32), jax.ShapeDtypeStruct((n, d_ff), F32)],
        compiler_params=_cparams(("parallel",)),
        name="mix_conv_ffn_sample",
    )(*acts, *consts)
    conv_new = ug.reshape(batch, seq, d_ff)[:, seq - (CONV_W - 1):]
    return y, conv_new


def _row_tile(n, want):
    t = min(want, n)
    while n % t:
        t //= 2
    return t


def _layer(x, is_prompt, state, rel_bias, lw, final_g):
    batch, seq, d = x.shape
    n = batch * seq
    x2d = x.reshape(n, d)
    n_qkv = Q_W + 2 * KV_W
    n_rw = lw["mu_shift"].shape[0]
    tm = _row_tile(n, min(1024, n // 4))
    qkv, rw, gates = _in_proj(x2d, lw["norm1_g"], lw["w_in"], n_qkv, n_rw, tm)
    if is_prompt:
        att, k_rows, v_rows = _attention_prompt(qkv, batch, seq, rel_bias, lw["sinks"], state["win_buf"])
        shift0 = wkv0 = None
    else:
        cache_k, cache_v = state["cache_k"], state["cache_v"]
        wb = cache_k.shape[1]
        assert seq <= wb
        att, k_rows, v_rows = _attention_sample(qkv, batch, seq, cache_k, cache_v, rel_bias, lw["sinks"],
                                                _row_tile(batch, 16))
        shift0, wkv0 = state["shift"], state["wkv"]
    k_rows = k_rows.reshape(batch, -1, N_KV, HEAD_DIM)
    v_rows = v_rows.reshape(batch, -1, N_KV, HEAD_DIM)
    rwo, wkv_new = _rwkv_mixer(rw, batch, seq, shift0, wkv0, lw["mu_shift"], lw["w0"], lw["w2"], lw["a0"],
                               lw["a2"], lw["g2"], lw["k_k"], lw["k_a"], lw["r_k"], lw["lnx_g"], lw["lnx_b"])
    shift_new = rw.reshape(batch, seq, n_rw)[:, seq - 1]
    if is_prompt:
        y, conv_new = _ffn_prompt(x2d, att, rwo, gates, batch, seq, lw, final_g, _row_tile(seq, FFN_ROWS))
    else:
        y, conv_new = _ffn_sample(x2d, att, rwo, gates, batch, seq, state["conv"], lw, final_g,
                                  _row_tile(batch, FFN_ROWS_WITH_STATE // seq))
    return y.reshape(batch, seq, d), (k_rows, v_rows, shift_new, wkv_new, conv_new)


def kernel(x_prompt, x_sample, cache_win_k, cache_win_v, state_shift, state_wkv, state_conv, rel_bias, norm1_g,
           w_in, sinks, mu_shift, w0, w2, a0, a2, g2, k_k, k_a, r_k, lnx_g, lnx_b, w_pa, w_pb, w_o, norm2_g,
           w_up, conv_w, conv_b, w_down, final_g):
    depth = w_in.shape[0]
    assert depth == 1, "the final norm is fused into the layer's last kernel"
    l = 0
    lw = dict(norm1_g=norm1_g[l], w_in=w_in[l].astype(BF16), sinks=sinks[l], mu_shift=mu_shift[l], w0=w0[l],
              w2=w2[l], a0=a0[l], a2=a2[l], g2=g2[l], k_k=k_k[l], k_a=k_a[l], r_k=r_k[l].reshape(-1),
              lnx_g=lnx_g[l], lnx_b=lnx_b[l], w_pa=w_pa[l].astype(BF16), w_pb=w_pb[l].astype(BF16),
              w_o=w_o[l].astype(BF16), norm2_g=norm2_g[l], w_up=w_up[l].astype(BF16), conv_w=conv_w[l],
              conv_b=conv_b[l], w_down=w_down[l].astype(BF16))
    win_buf = cache_win_k.shape[2]
    y_p, st_p = _layer(x_prompt, True, dict(win_buf=win_buf), rel_bias, lw, final_g)
    y_s, st_s = _layer(x_sample, False,
                       dict(cache_k=cache_win_k[l], cache_v=cache_win_v[l], shift=state_shift[l],
                            wkv=state_wkv[l], conv=state_conv[l]), rel_bias, lw, final_g)
    stack = lambda t: t[None]
    return (y_p, y_s) + tuple(stack(t) for t in st_p) + tuple(stack(t) for t in st_s)
```

```python
import functools
import math

import numpy as np
import jax
import jax.numpy as jnp
from jax import lax
from jax.experimental import pallas as pl
from jax.experimental.pallas import tpu as pltpu

F32 = jnp.float32
BF16 = jnp.bfloat16

HEAD_DIM = 64
N_HEADS = 8
N_KV = 2
WINDOW = 128
N_BUCKETS = 32
MAX_EXACT = N_BUCKETS // 2
REL_MAX_DIST = 128
RW_N = 64
RW_HEADS = 8
RW = RW_HEADS * RW_N
NORM_EPS = 1e-6
GN_EPS = 64e-5
NEG = -1e30
CONV_W = 3

Q_W = N_HEADS * HEAD_DIM
KV_W = N_KV * HEAD_DIM
LANES = 128
SUBLANES = 8
MXU_WIDTH = 256
CHUNK = 64
VMEM_LIMIT = 56 * 1024 * 1024
FFN_COL_CHUNKS = 3
FFN_ROWS = 512
FFN_ROWS_WITH_STATE = 256
ATTN_BLOCKS = 8
WKV_UNITS = 8
WKV_UNITS_WITH_STATE = 4


def _resident(shape):
    return pl.BlockSpec(shape, lambda *_: (0,) * len(shape), pipeline_mode=pl.Buffered(1))


def _cparams(sem):
    return pltpu.CompilerParams(dimension_semantics=sem, vmem_limit_bytes=VMEM_LIMIT)


def _sigmoid(x):
    return 1.0 / (1.0 + jnp.exp(-x))


def _dg(a, b, kind):
    if kind == "nn":
        dn = (((1,), (0,)), ((), ()))
    else:
        dn = (((1,), (1,)), ((), ()))
    return lax.dot_general(a, b, dn, preferred_element_type=F32)


def _mm(a, b, kind="nn"):
    return _dg(a.astype(BF16), b.astype(BF16), kind)


def _mm_exact_lhs(a_bf16, b, n_terms):
    out = None
    rem = b
    for _ in range(n_terms):
        piece = rem.astype(BF16)
        term = _dg(a_bf16, piece, "nn")
        out = term if out is None else out + term
        rem = rem - piece.astype(F32)
    return out


def _inproj_kernel(x_ref, g_ref, w_ref, qkv_ref, rw_ref, gate_ref, *, n_qkv, n_rw):
    x = x_ref[...]
    ms = jnp.mean(x * x, axis=-1, keepdims=True)
    h = (x * lax.rsqrt(ms + NORM_EPS) * g_ref[...]).astype(BF16)
    n_gate = gate_ref.shape[1]
    step = MXU_WIDTH
    plain = [(qkv_ref, c, c, min(step, n_qkv - c)) for c in range(0, n_qkv, step)]
    plain += [(rw_ref, c, n_qkv + c, min(step, n_rw - c)) for c in range(0, n_rw, step)]
    gated = [(gate_ref, c, n_qkv + n_rw + c, min(step, n_gate - c)) for c in range(0, n_gate, step)]
    pieces = []
    while plain or gated:
        pieces += [gated.pop(0)] if gated else []
        pieces += [plain.pop(0)] if plain else []
    dot = lambda p: jnp.dot(h, w_ref[:, p[2]:p[2] + p[3]], preferred_element_type=F32)
    pending = dot(pieces[0])
    for j, (ref, c0, _, width) in enumerate(pieces):
        out = pending
        if j + 1 < len(pieces):
            pending = dot(pieces[j + 1])
        if ref is gate_ref:
            out = _sigmoid(out)
        ref[:, c0:c0 + width] = out.astype(ref.dtype)


def _in_proj(x2d, g, w_bf16, n_qkv, n_rw, tm):
    n, d = x2d.shape
    n_gate = w_bf16.shape[1] - n_qkv - n_rw
    return pl.pallas_call(
        functools.partial(_inproj_kernel, n_qkv=n_qkv, n_rw=n_rw),
        grid=(n // tm,),
        in_specs=[
            pl.BlockSpec((tm, d), lambda i: (i, 0)),
            _resident((1, d)),
            _resident(w_bf16.shape),
        ],
        out_specs=[
            pl.BlockSpec((tm, n_qkv), lambda i: (i, 0)),
            pl.BlockSpec((tm, n_rw), lambda i: (i, 0)),
            pl.BlockSpec((tm, n_gate), lambda i: (i, 0)),
        ],
        out_shape=[
            jax.ShapeDtypeStruct((n, n_qkv), F32),
            jax.ShapeDtypeStruct((n, n_rw), F32),
            jax.ShapeDtypeStruct((n, n_gate), BF16),
        ],
        compiler_params=_cparams(("parallel",)),
        name="in_proj",
    )(x2d, g.reshape(1, d), w_bf16)


def _t5_bucket_np(dist):
    n = np.maximum(dist, 0)
    nf = np.maximum(n, 1).astype(np.float32)
    large = MAX_EXACT + (np.log(nf / MAX_EXACT) / math.log(REL_MAX_DIST / MAX_EXACT)
                         * (N_BUCKETS - MAX_EXACT)).astype(np.int32)
    return np.where(n < MAX_EXACT, n, np.minimum(large, N_BUCKETS - 1)).astype(np.int32)


def _attn_kernel(q_ref, k1_ref, k2_ref, v1_ref, v2_ref, bucket_ref, relb_ref, sink_ref, o_ref, *rest,
                 nq, nk, n_blocks, first_block_axis, emit_window):
    bt = q_ref.shape[0]
    bias_ref = rest[-1]
    if emit_window == "tail":
        @pl.when(pl.program_id(1) == pl.num_programs(1) - 1)
        def _():
            n_new, n_win = k2_ref.shape[1], rest[0].shape[1]
            rest[0][...] = k2_ref[:, n_new - n_win:]
            rest[1][...] = v2_ref[:, n_new - n_win:]
    else:
        n_old, n_new = k1_ref.shape[1], k2_ref.shape[1]
        for w_ref, old_ref, new_ref in ((rest[0], k1_ref, k2_ref), (rest[1], v1_ref, v2_ref)):
            w_ref[:, :n_old - n_new] = old_ref[:, n_new:]
            w_ref[:, n_old - n_new:] = new_ref[...]
    first_step = pl.program_id(0) == 0
    if first_block_axis is not None:
        first_step = jnp.logical_and(first_step, pl.program_id(1) == 0)

    @pl.when(first_step)
    def _():
        bucket = bucket_ref[...]
        prev_key = lax.broadcasted_iota(jnp.int32, (1, nk), 1) < (nk // 2)
        for n in range(N_HEADS):
            acc = jnp.full((nq, nk), NEG, F32)
            for b in range(N_BUCKETS):
                acc = jnp.where(bucket == b, relb_ref[b, n], acc)
            c, half = divmod(n, 2)
            bias_ref[0, c, :, half * nk:(half + 1) * nk] = acc
            if first_block_axis is not None:
                bias_ref[1, c, :, half * nk:(half + 1) * nk] = jnp.where(prev_key, NEG, acc)

    n_keys = nk + (n_blocks - 1) * nq

    def padded(a_ref, b_ref):
        parts = [a_ref[...], b_ref[...]]
        n_now = a_ref.shape[1] + b_ref.shape[1]
        if n_now < n_keys:
            parts.append(jnp.zeros((bt, n_keys - n_now, LANES), F32))
        return jnp.concatenate(parts, axis=1)

    kk = padded(k1_ref, k2_ref)
    vv = padded(v1_ref, v2_ref)
    kk_r = pltpu.roll(kk, HEAD_DIM, 2)
    vv_r = pltpu.roll(vv, HEAD_DIM, 2)
    lane = lax.broadcasted_iota(jnp.int32, (1, 1, LANES), 2)
    lo = lane < HEAD_DIM

    def halves(x, x_r, kvh):
        src_lo, src_hi = (x, x_r) if kvh == 0 else (x_r, x)
        even = jnp.where(lo, src_lo, 0.0).astype(BF16)
        odd = jnp.where(lo, 0.0, src_hi).astype(BF16)
        return even, odd

    k_eo = [halves(kk, kk_r, h) for h in range(N_KV)]
    v_eo = [halves(vv, vv_r, h) for h in range(N_KV)]

    def window(eo, j):
        return jnp.concatenate([eo[0][:, j * nq:j * nq + nk], eo[1][:, j * nq:j * nq + nk]], axis=1)

    n_cols = N_HEADS // 2
    units = [(j, c) for j in range(n_blocks) for c in range(n_cols)]
    kvh_of = lambda c: (2 * c) // (N_HEADS // N_KV)
    k_win = {(j, h): window(k_eo[h], j) for j in range(n_blocks) for h in range(N_KV)}
    v_win = {(j, h): window(v_eo[h], j) for j in range(n_blocks) for h in range(N_KV)}
    sums_on_mxu = bt == 1
    if sums_on_mxu:
        key_row = lax.broadcasted_iota(jnp.int32, (1, 2 * nk, 1), 1)
        ones_cols = jnp.where((key_row < nk) == lo, 1.0, 0.0).astype(BF16)
        v_win = {jh: jnp.concatenate([v, ones_cols], axis=2) for jh, v in v_win.items()}

    scale = HEAD_DIM ** -0.5
    first_table = 0
    if first_block_axis is not None:
        first_table = jnp.where(pl.program_id(first_block_axis) == 0, 1, 0)
    s_all = {}
    for j, c in units:
        qc = (q_ref[:, j * nq:(j + 1) * nq, c * LANES:(c + 1) * LANES] * scale).astype(BF16)
        s_all[j, c] = jnp.einsum("bqd,bkd->bqk", qc, k_win[j, kvh_of(c)], preferred_element_type=F32)
    e_all, sink_all = {}, {}
    for j, c in units:
        s = s_all[j, c] + bias_ref[first_table if j == 0 else 0, c][None]
        es, sink_terms = [], []
        for half in range(2):
            sh = s[:, :, half * nk:(half + 1) * nk]
            sink = sink_ref[2 * c + half]
            m = jnp.maximum(jnp.max(sh, axis=-1, keepdims=True), sink)
            e = jnp.exp(sh - m)
            es.append(e.astype(BF16))
            sink_terms.append(jnp.exp(sink - m) + (0.0 if sums_on_mxu else jnp.sum(e, axis=-1, keepdims=True)))
        e_all[j, c] = jnp.concatenate(es, axis=2)
        sink_all[j, c] = jnp.where(lo, sink_terms[0], sink_terms[1])
    for j, c in units:
        o = jnp.einsum("bqk,bkd->bqd", e_all[j, c], v_win[j, kvh_of(c)], preferred_element_type=F32)
        denom = sink_all[j, c] + (o[:, :, LANES:] if sums_on_mxu else 0.0)
        o = (o[:, :, :LANES] / denom)
        if len(o_ref.shape) == 2:
            o_ref[:, c * LANES:(c + 1) * LANES] = o.reshape(bt * nq, LANES).astype(o_ref.dtype)
        else:
            o_ref[:, j * nq:(j + 1) * nq, c * LANES:(c + 1) * LANES] = o.astype(o_ref.dtype)


def _attention_prompt(qkv, batch, seq, rel_bias, sinks, win_buf):
    nblk = seq // WINDOW
    nb = ATTN_BLOCKS if nblk % ATTN_BLOCKS == 0 else 1
    nsteps = nblk // nb
    width = qkv.shape[1]
    q_blk = qkv.reshape(batch * nblk, WINDOW, width)
    q_step = qkv.reshape(batch * nsteps, nb * WINDOW, width)
    kcol = Q_W // LANES
    vcol = (Q_W + KV_W) // LANES
    nk = 2 * WINDOW
    qi = np.arange(WINDOW)[:, None] + WINDOW
    kj = np.arange(nk)[None, :]
    dist = qi - kj
    bucket = np.where((dist >= 0) & (dist < WINDOW), _t5_bucket_np(dist), -1).astype(np.int32)
    cur = lambda c: (lambda b, i: (b * nsteps + i, 0, c))
    prev = lambda c: (lambda b, i: (b * nblk + jnp.maximum(i * nb - 1, 0), 0, c))
    assert win_buf <= nb * WINDOW
    win_spec = pl.BlockSpec((1, win_buf, LANES), lambda b, i: (b, 0, 0))
    win_shape = jax.ShapeDtypeStruct((batch, win_buf, KV_W), F32)
    out, win_k, win_v = pl.pallas_call(
        functools.partial(_attn_kernel, nq=WINDOW, nk=nk, n_blocks=nb, first_block_axis=1, emit_window="tail"),
        grid=(batch, nsteps),
        in_specs=[
            pl.BlockSpec((1, nb * WINDOW, Q_W), cur(0)),
            pl.BlockSpec((1, WINDOW, LANES), prev(kcol)),
            pl.BlockSpec((1, nb * WINDOW, LANES), cur(kcol)),
            pl.BlockSpec((1, WINDOW, LANES), prev(vcol)),
            pl.BlockSpec((1, nb * WINDOW, LANES), cur(vcol)),
            _resident(bucket.shape),
            pl.BlockSpec(memory_space=pltpu.SMEM),
            pl.BlockSpec(memory_space=pltpu.SMEM),
        ],
        out_specs=[pl.BlockSpec((1, nb * WINDOW, Q_W), cur(0)), win_spec, win_spec],
        out_shape=[jax.ShapeDtypeStruct((batch * nsteps, nb * WINDOW, Q_W), BF16), win_shape, win_shape],
        scratch_shapes=[pltpu.VMEM((2, N_HEADS // 2, WINDOW, 2 * nk), F32)],
        compiler_params=_cparams(("arbitrary", "arbitrary")),
        name="attn_prompt",
    )(q_step, q_blk, q_step, q_blk, q_step, jnp.asarray(bucket), rel_bias, sinks)
    return out.reshape(batch * seq, Q_W), win_k, win_v


def _attention_sample(qkv, batch, seq, cache_k, cache_v, rel_bias, sinks, bt):
    wb = cache_k.shape[1]
    nk = 2 * WINDOW
    q3 = qkv.reshape(batch, seq, qkv.shape[1])
    ck = cache_k.reshape(batch, wb, KV_W)
    cv = cache_v.reshape(batch, wb, KV_W)
    kcol = Q_W // LANES
    vcol = (Q_W + KV_W) // LANES
    tq = np.arange(seq)[:, None]
    j = np.arange(nk)[None, :]
    dist = np.where(j < wb, tq + wb - j, tq - (j - wb))
    ok = (dist >= 0) & (dist < WINDOW) & (j < wb + seq)
    bucket = np.where(ok, _t5_bucket_np(dist), -1).astype(np.int32)
    win_spec = pl.BlockSpec((bt, wb, LANES), lambda b: (b, 0, 0))
    win_shape = jax.ShapeDtypeStruct((batch, wb, KV_W), F32)
    out, win_k, win_v = pl.pallas_call(
        functools.partial(_attn_kernel, nq=seq, nk=nk, n_blocks=1, first_block_axis=None, emit_window="shift"),
        grid=(batch // bt,),
        in_specs=[
            pl.BlockSpec((bt, seq, Q_W), lambda b: (b, 0, 0)),
            pl.BlockSpec((bt, wb, LANES), lambda b: (b, 0, 0)),
            pl.BlockSpec((bt, seq, LANES), lambda b: (b, 0, kcol)),
            pl.BlockSpec((bt, wb, LANES), lambda b: (b, 0, 0)),
            pl.BlockSpec((bt, seq, LANES), lambda b: (b, 0, vcol)),
            _resident(bucket.shape),
            pl.BlockSpec(memory_space=pltpu.SMEM),
            pl.BlockSpec(memory_space=pltpu.SMEM),
        ],
        out_specs=[pl.BlockSpec((bt * seq, Q_W), lambda b: (b, 0)), win_spec, win_spec],
        out_shape=[jax.ShapeDtypeStruct((batch * seq, Q_W), BF16), win_shape, win_shape],
        scratch_shapes=[pltpu.VMEM((1, N_HEADS // 2, seq, 2 * nk), F32)],
        compiler_params=_cparams(("arbitrary",)),
        name="attn_sample",
    )(q3, ck, q3, cv, q3, jnp.asarray(bucket), rel_bias, sinks)
    return out, win_k, win_v


def _wkv_kernel(*refs, n_units, n_seg, has_state, lora_w):
    C = CHUNK
    seg_len = C // n_seg
    n_pairs = RW // LANES
    it = iter(refs)
    p_ref = next(it)
    p0_ref, s0_ref = (next(it), next(it)) if has_state else (None, None)
    (mu_ref, w2a2_ref, w0_ref, a0_ref, g2_ref, kk_ref, ka_ref, rk_ref, lng_ref, lnb_ref, ltri_ref) = (
        next(it) for _ in range(11))
    elast_ref = next(it) if n_seg > 1 else None
    ones_ref, out_ref, sout_ref = next(it), next(it), next(it)
    last_ref, sbd_ref = (None, None) if has_state else (next(it), next(it))

    if not has_state:
        @pl.when(pl.program_id(1) == 0)
        def _():
            last_ref[...] = jnp.zeros_like(last_ref)
            sbd_ref[...] = jnp.zeros_like(sbd_ref)

    row = lax.broadcasted_iota(jnp.int32, (C, 1), 0)
    rows_of = lambda u: slice(u * C, (u + 1) * C)
    lane = lax.broadcasted_iota(jnp.int32, (1, LANES), 1)
    lo = lane < RW_N
    ones_bd = ones_ref[...]

    def headsum(x):
        xb = x.astype(BF16)
        return jnp.concatenate([_dg(xb[:, c:c + MXU_WIDTH], ones_bd, "nn") for c in range(0, RW, MXU_WIDTH)],
                               axis=1)

    def prepare(us):
        xs_parts = []
        for u in us:
            p = p_ref[u, 0]
            rolled = pltpu.roll(p, 1, 0)
            if has_state:
                prev = jnp.where(row % seg_len == 0, p0_ref[u, 0], rolled)
            else:
                prev = jnp.where(row == 0, last_ref[u], rolled)
                last_ref[u] = p_ref[u, 0, C - 1:C, :]
            xs_parts.append(p + mu_ref[...] * (prev - p))
        xs = jnp.concatenate(xs_parts, axis=0)
        r = xs[:, 0:RW]
        k = xs[:, RW:2 * RW]
        v = xs[:, 2 * RW:3 * RW]
        lwla = xs[:, 3 * RW:3 * RW + LANES]
        lg = xs[:, 3 * RW + LANES:3 * RW + 2 * LANES]
        lwla = jnp.where(lane < lora_w, jnp.tanh(lwla), lwla)
        wa = jnp.dot(lwla.astype(BF16), w2a2_ref[...], preferred_element_type=F32)
        logw = -math.exp(-0.5) * _sigmoid(w0_ref[...] + wa[:, :RW])
        a_sig = _sigmoid(a0_ref[...] + wa[:, RW:])
        g = jnp.dot(_sigmoid(lg).astype(BF16), g2_ref[...], preferred_element_type=F32)
        kk = k * kk_ref[...]
        kk = kk * (1.0 / jnp.maximum(jnp.sqrt(headsum(kk * kk)), 1e-12))
        k = k * (1.0 + (a_sig - 1.0) * ka_ref[...])
        a = -kk
        b = kk * a_sig
        bonus = headsum(r * k * rk_ref[...]) * v
        cws, cwl = [], []
        for i in range(len(us)):
            cw_u = _mm_exact_lhs(ltri_ref[...], logw[rows_of(i)], 3)
            cws.append(cw_u)
            if n_seg == 1:
                cwl.append(jnp.broadcast_to(cw_u[C - 1:C, :], (C, RW)))
            else:
                cwl.append(_mm_exact_lhs(elast_ref[...], cw_u, 3))
        cw = jnp.concatenate(cws, axis=0)
        cw_last = jnp.concatenate(cwl, axis=0)
        w_inv = jnp.exp(-cw)
        b_t, k_t = b * w_inv, k * w_inv
        w_tail = jnp.exp(cw_last - cw)
        b_h, k_h = b * w_tail, k * w_tail
        a_t, r_t = a * jnp.exp(cw - logw), r * jnp.exp(cw)
        if n_seg == 1:
            w_c = jnp.exp(jnp.concatenate([cw_u[C - 1:C, :] for cw_u in cws], axis=0))
        else:
            w_c = jnp.exp(cw_last)
        return a_t, r_t, b_t, k_t, b_h, k_h, v, w_c, bonus, g

    def bd(y):
        return jnp.concatenate([jnp.where(lo, y, 0.0), jnp.where(lo, 0.0, y)], axis=0)

    zeros_head = jnp.zeros((RW_N, RW_N), F32)

    def pack_pair(s_even, s_odd):
        return jnp.concatenate([jnp.concatenate([s_even, zeros_head], axis=1),
                                jnp.concatenate([zeros_head, s_odd], axis=1)], axis=0)

    def store_pair(ref, i, q, s_pair):
        ref[i, 2 * q] = s_pair[:RW_N, :RW_N]
        ref[i, 2 * q + 1] = s_pair[RW_N:, RW_N:]

    s_idx = lane % C
    strict = s_idx < row
    incl = s_idx <= row
    if n_seg > 1:
        same_seg = (s_idx // seg_len) == (row // seg_len)
        strict = jnp.logical_and(strict, same_seg)
        incl = jnp.logical_and(incl, same_seg)
    row2 = lax.broadcasted_iota(jnp.int32, (2 * C, 1), 0)
    same_head = (row2 < RW_N) == lo

    part = lambda x, i, q: x[i * C:(i + 1) * C, q * LANES:(q + 1) * LANES]

    def recurrence(us, ops):
        a_t, r_t, b_t, k_t, b_h, k_h, v, w_c = ops[:8]
        units = [(i, q) for i in range(len(us)) for q in range(n_pairs)]
        sc = {uq: _mm(jnp.concatenate([part(a_t, *uq), part(r_t, *uq)], axis=0),
                      jnp.concatenate([bd(part(b_t, *uq)), bd(part(k_t, *uq))], axis=0),
                      "nt") for uq in units}
        pw = {uq: jnp.where(strict, sc[uq][:C, :LANES], 0.0) for uq in units}
        m_rb = {uq: jnp.where(incl, sc[uq][C:, :LANES], 0.0) for uq in units}
        lm_v = {uq: _mm(jnp.concatenate([jnp.where(strict, sc[uq][:C, LANES:], 0.0),
                                         jnp.where(incl, sc[uq][C:, LANES:], 0.0)], axis=0),
                        bd(part(v, *uq)), "nn") for uq in units}
        tinv = {uq: pw[uq] + jnp.where(s_idx == row, 1.0, 0.0) for uq in units}
        n_lvl = int(math.log2(seg_len))
        for lvl in range(1, n_lvl):
            last = lvl + 1 == n_lvl
            if lvl == 1:
                for uq in units:
                    pw[uq] = _mm(pw[uq], bd(pw[uq]), "nn")
            for uq in units:
                rhs = bd(tinv[uq]) if last else jnp.concatenate([bd(tinv[uq]), bd(pw[uq])], axis=1)
                upd = _mm(pw[uq], rhs, "nn")
                tinv[uq] = tinv[uq] + upd[:, :LANES]
                if not last:
                    pw[uq] = upd[:, LANES:]
        x = {uq: _mm(tinv[uq], jnp.concatenate([bd(part(a_t, *uq)), bd(lm_v[uq][:C])], axis=1), "nn")
             for uq in units}
        a_hat = {uq: x[uq][:, :LANES] for uq in units}
        v_hat = {uq: x[uq][:, LANES:] for uq in units}
        z = {uq: _mm(m_rb[uq], jnp.concatenate([bd(a_hat[uq]), bd(v_hat[uq])], axis=1), "nn")
             for uq in units}
        r_hat = {uq: part(r_t, *uq) + z[uq][:, :LANES] for uq in units}
        y_intra = {uq: z[uq][:, LANES:] + lm_v[uq][C:] for uq in units}
        ys = {}
        if n_seg == 1:
            s_old = {(i, q): sbd_ref[us[i], q] for i, q in units}
            t1 = {uq: _mm(jnp.concatenate([a_hat[uq], r_hat[uq]], axis=0), s_old[uq], "nt")
                  for uq in units}
            for uq in units:
                ys[uq] = t1[uq][C:] + y_intra[uq]
                uv = jnp.concatenate([t1[uq][:C] + v_hat[uq], part(v, *uq)], axis=0)
                bkh = jnp.concatenate([part(b_h, *uq), part(k_h, *uq)], axis=0)
                ds = _mm(uv.T, bkh, "nn")
                w_cq = w_c[uq[0]:uq[0] + 1, uq[1] * LANES:(uq[1] + 1) * LANES]
                sbd_ref[us[uq[0]], uq[1]] = s_old[uq] * w_cq + jnp.where(same_head, ds, 0.0)
        else:
            row_seg = (row2 % C) // seg_len
            for uq in units:
                i, q = uq
                first_seq = us[i] * n_seg
                u_parts, y_parts, s_olds = [], [], []
                for sg in range(n_seg):
                    rs = slice(sg * seg_len, (sg + 1) * seg_len)
                    s_sg = pack_pair(s0_ref[first_seq + sg, 2 * q], s0_ref[first_seq + sg, 2 * q + 1])
                    t1 = _mm(jnp.concatenate([a_hat[uq][rs], r_hat[uq][rs]], axis=0), s_sg, "nt")
                    u_parts.append(t1[:seg_len] + v_hat[uq][rs])
                    y_parts.append(t1[seg_len:] + y_intra[uq][rs])
                    s_olds.append(s_sg)
                ys[uq] = jnp.concatenate(y_parts, axis=0)
                uv_t = jnp.concatenate(u_parts + [part(v, *uq)], axis=0).T
                bkh = jnp.concatenate([part(b_h, *uq), part(k_h, *uq)], axis=0)
                w_cq = part(w_c, *uq)
                by_seg = jnp.concatenate([jnp.where(row_seg == sg, bkh, 0.0) for sg in range(n_seg)], axis=1)
                ds_all = _mm(uv_t, by_seg, "nn")
                for sg in range(n_seg):
                    ds = ds_all[:, sg * LANES:(sg + 1) * LANES]
                    store_pair(sout_ref, first_seq + sg, q,
                               s_olds[sg] * w_cq[sg * seg_len:sg * seg_len + 1] + jnp.where(same_head, ds, 0.0))
        return ys

    def finish(us, ys, ops):
        bonus, g = ops[8:]
        y = jnp.concatenate([jnp.concatenate([ys[i, q] for q in range(n_pairs)], axis=1)
                             for i in range(len(us))], axis=0)
        mean = headsum(y) * (1.0 / RW_N)
        d = y - mean
        var = headsum(d * d) * (1.0 / RW_N)
        y = d * lax.rsqrt(var + GN_EPS) * lng_ref[...] + lnb_ref[...]
        y = ((y + bonus) * g).astype(out_ref.dtype)
        for i, u in enumerate(us):
            out_ref[u, 0] = y[rows_of(i)]

    all_units = list(range(n_units))
    ops = prepare(all_units)
    finish(all_units, recurrence(all_units, ops), ops)
    units = [(u, q) for u in range(n_units) for q in range(n_pairs)]

    if not has_state:
        @pl.when(pl.program_id(1) == pl.num_programs(1) - 1)
        def _():
            for u, q in units:
                store_pair(sout_ref, u, q, sbd_ref[u, q])


def _rwkv_mixer(rw, batch, seq, shift0, wkv0, mu, w0, w2, a0, a2, g2, k_k, k_a, r_k, lnx_g, lnx_b):
    n_shift = rw.shape[1]
    lora_w, lora_a = w2.shape[0], a2.shape[0]
    assert lora_w + lora_a == LANES and g2.shape[0] == LANES and n_shift == 3 * RW + 2 * LANES
    has_state = shift0 is not None
    C = CHUNK
    n_pairs = RW // LANES
    if has_state:
        assert C % seq == 0 and batch % (C // seq) == 0
        n_seg, n_chunks, n_groups = C // seq, 1, batch * seq // C
    else:
        assert seq % C == 0
        n_seg, n_chunks, n_groups = 1, seq // C, batch
    want = WKV_UNITS_WITH_STATE if has_state else WKV_UNITS
    nu = want if n_groups % want == 0 else 1
    seg_len = C // n_seg
    w2a2 = jnp.zeros((LANES, 2 * RW), F32).at[:lora_w, :RW].set(w2).at[lora_w:, RW:].set(a2).astype(BF16)
    t = np.arange(C)
    same_seg = (t[:, None] // seg_len) == (t[None, :] // seg_len)
    ltri = jnp.asarray(((t[:, None] >= t[None, :]) & same_seg).astype(np.float32), BF16)
    elast = jnp.asarray((t[None, :] == (t[:, None] // seg_len) * seg_len + seg_len - 1).astype(np.float32), BF16)
    ones_bd = jnp.asarray(np.kron(np.eye(MXU_WIDTH // RW_N, dtype=np.float32),
                                  np.ones((RW_N, RW_N), np.float32)), BF16)
    row = lambda x: x.reshape(1, -1).astype(F32)
    rw4 = rw.reshape(n_groups, n_chunks, C, n_shift)
    blk = lambda w: pl.BlockSpec((nu, 1, C, w), lambda i, c: (i, c, 0, 0))
    st_blk = pl.BlockSpec((nu * n_seg, RW_HEADS, RW_N, RW_N), lambda i, c: (i, 0, 0, 0))
    args, specs = [rw4], [blk(n_shift)]
    if has_state:
        p0 = jnp.pad(shift0[:, None, :], ((0, 0), (0, seq - 1), (0, 0))).reshape(n_groups, 1, C, n_shift)
        args += [p0, wkv0]
        specs += [blk(n_shift), st_blk]
    consts = [row(mu), w2a2, row(w0), row(a0), g2.astype(BF16), row(k_k), row(k_a), row(r_k), row(lnx_g),
              row(lnx_b), ltri] + ([elast] if n_seg > 1 else []) + [ones_bd]
    args += consts
    specs += [_resident(c.shape) for c in consts]
    scratch = [] if has_state else [pltpu.VMEM((nu, 1, n_shift), F32),
                                    pltpu.VMEM((nu, n_pairs, LANES, LANES), F32)]
    out, s_new = pl.pallas_call(
        functools.partial(_wkv_kernel, n_units=nu, n_seg=n_seg, has_state=has_state, lora_w=lora_w),
        grid=(n_groups // nu, n_chunks),
        in_specs=specs,
        out_specs=[blk(RW), st_blk],
        out_shape=[
            jax.ShapeDtypeStruct((n_groups, n_chunks, C, RW), BF16),
            jax.ShapeDtypeStruct((batch, RW_HEADS, RW_N, RW_N), F32),
        ],
        scratch_shapes=scratch,
        compiler_params=_cparams(("arbitrary", "arbitrary")),
        name="rwkv7",
    )(*args)
    return out.reshape(batch * seq, RW), s_new


def _branch_mix(x_ref, att_ref, rwo_ref, gate_ref, wpa_ref, wpb_ref, wo_ref):
    d = x_ref.shape[1]
    att, rwo = att_ref[...], rwo_ref[...]

    def branch_dots(c):
        return (jnp.dot(att, wpa_ref[:, c:c + MXU_WIDTH], preferred_element_type=F32),
                jnp.dot(rwo, wpb_ref[:, c:c + MXU_WIDTH], preferred_element_type=F32))

    mixes = []
    pending = branch_dots(0)
    for c in range(0, d, MXU_WIDTH):
        pa, pb = pending
        if c + MXU_WIDTH < d:
            pending = branch_dots(c + MXU_WIDTH)
        mixes.append((gate_ref[:, c:c + MXU_WIDTH] * pa + gate_ref[:, d + c:d + c + MXU_WIDTH] * pb).astype(BF16))
    mix = jnp.concatenate(mixes, axis=1)
    return x_ref[...] + jnp.dot(mix, wo_ref[...], preferred_element_type=F32)


def _rms(x, g):
    ms = jnp.mean(x * x, axis=-1, keepdims=True)
    return x * lax.rsqrt(ms + NORM_EPS) * g


def _gelu_tanh(c):
    return c * (0.5 * (1.0 + jnp.tanh(math.sqrt(2.0 / math.pi) * (c + 0.044715 * (c * c * c)))))


def _conv_ffn_cols(h, x, wup_ref, cw_ref, cb_ref, wdown_ref, shifted, ug_sink):
    d_ff = cb_ref.shape[1]
    tiles = -(-d_ff // MXU_WIDTH)
    edges = [min(d_ff, MXU_WIDTH * ((tiles * j + FFN_COL_CHUNKS - 1) // FFN_COL_CHUNKS))
             for j in range(FFN_COL_CHUNKS + 1)]
    spans = list(zip(edges[:-1], edges[1:]))

    def up(span):
        lo_c, hi_c = span
        return (jnp.dot(h, wup_ref[:, lo_c:hi_c], preferred_element_type=F32),
                jnp.dot(h, wup_ref[:, d_ff + lo_c:d_ff + hi_c], preferred_element_type=F32))

    acc = x
    pending = up(spans[0])
    for j, (lo_c, hi_c) in enumerate(spans):
        cs = slice(lo_c, hi_c)
        ug_full, uv_full = pending
        if j + 1 < len(spans):
            pending = up(spans[j + 1])
        ug, ug_m1, ug_m2, uv = shifted(ug_full, uv_full, cs)
        c = cb_ref[:, cs] + cw_ref[0:1, cs] * ug_m2 + cw_ref[1:2, cs] * ug_m1 + cw_ref[2:3, cs] * ug
        act = (_gelu_tanh(c) * uv).astype(BF16)
        acc = acc + jnp.dot(act, wdown_ref[lo_c:hi_c, :], preferred_element_type=F32)
        ug_sink(ug, cs)
    return acc


def _ffn_prompt_kernel(x_ref, att_ref, rwo_ref, gate_ref, wpa_ref, wpb_ref, wo_ref, g2_ref, wup_ref, cw_ref,
                       cb_ref, wdown_ref, gf_ref, y_ref, ug_ref, carry_ref):
    tm = x_ref.shape[0]
    x = _branch_mix(x_ref, att_ref, rwo_ref, gate_ref, wpa_ref, wpb_ref, wo_ref)
    h = _rms(x, g2_ref[...]).astype(BF16)
    seq_start = pl.program_id(1) == 0

    def shifted(ug, uv, cs):
        before = jnp.where(seq_start, 0.0, carry_ref[:, cs])
        carry_ref[:, cs] = ug[tm - SUBLANES:]
        ug_e = jnp.concatenate([before, ug], axis=0)
        return ug, pltpu.roll(ug_e, 1, 0)[SUBLANES:], pltpu.roll(ug_e, 2, 0)[SUBLANES:], uv

    def ug_sink(ug, cs):
        ug_ref[:, cs] = ug[tm - SUBLANES:]

    x2 = _conv_ffn_cols(h, x, wup_ref, cw_ref, cb_ref, wdown_ref, shifted, ug_sink)
    y_ref[...] = _rms(x2, gf_ref[...])


def _ffn_sample_kernel(x_ref, att_ref, rwo_ref, gate_ref, e_ref, wpa_ref, wpb_ref, wo_ref, g2_ref, wup_ref,
                       cw_ref, cb_ref, wdown_ref, gf_ref, y_ref, ug_ref, *, seq):
    rows = x_ref.shape[0]
    x = _branch_mix(x_ref, att_ref, rwo_ref, gate_ref, wpa_ref, wpb_ref, wo_ref)
    h = _rms(x, g2_ref[...]).astype(BF16)
    t = lax.broadcasted_iota(jnp.int32, (rows, 1), 0) % seq

    def shifted(ug, uv, cs):
        e = e_ref[:, cs]
        ug_m1 = jnp.where(t == 0, pltpu.roll(e, rows - 1, 0), pltpu.roll(ug, 1, 0))
        ug_m2 = jnp.where(t < 2, e, pltpu.roll(ug, 2, 0))
        return ug, ug_m1, ug_m2, uv

    def ug_sink(ug, cs):
        ug_ref[:, cs] = ug

    x2 = _conv_ffn_cols(h, x, wup_ref, cw_ref, cb_ref, wdown_ref, shifted, ug_sink)
    y_ref[...] = _rms(x2, gf_ref[...])


def _ffn_prompt(x2d, att, rwo, gates, batch, seq, lw, final_g, tm):
    n, d = x2d.shape
    d_ff = lw["conv_b"].shape[0]
    nt = seq // tm
    rows = lambda a: pl.BlockSpec((tm, a.shape[1]), lambda b, i: (b * nt + i, 0))
    full = lambda a: _resident(a.shape)
    consts = [lw["w_pa"], lw["w_pb"], lw["w_o"], lw["norm2_g"].reshape(1, d), lw["w_up"], lw["conv_w"],
              lw["conv_b"].reshape(1, d_ff), lw["w_down"], final_g.reshape(1, d)]
    acts = [x2d, att, rwo, gates]
    y, ug_last = pl.pallas_call(
        _ffn_prompt_kernel,
        grid=(batch, nt),
        in_specs=[rows(a) for a in acts] + [full(c) for c in consts],
        out_specs=[
            pl.BlockSpec((tm, d), lambda b, i: (b * nt + i, 0)),
            pl.BlockSpec((SUBLANES, d_ff), lambda b, i: (b, 0)),
        ],
        out_shape=[
            jax.ShapeDtypeStruct((n, d), F32),
            jax.ShapeDtypeStruct((batch * SUBLANES, d_ff), F32),
        ],
        scratch_shapes=[pltpu.VMEM((SUBLANES, d_ff), F32)],
        compiler_params=_cparams(("arbitrary", "arbitrary")),
        name="mix_conv_ffn_prompt",
    )(*acts, *consts)
    conv_new = ug_last.reshape(batch, SUBLANES, d_ff)[:, SUBLANES - (CONV_W - 1):]
    return y, conv_new


def _ffn_sample(x2d, att, rwo, gates, batch, seq, conv0, lw, final_g, bt):
    n, d = x2d.shape
    d_ff = lw["conv_b"].shape[0]
    nrows = bt * seq
    e = jnp.pad(conv0, ((0, 0), (0, seq - (CONV_W - 1)), (0, 0))).reshape(n, d_ff)
    rows = lambda a: pl.BlockSpec((nrows, a.shape[1]), lambda i: (i, 0))
    full = lambda a: _resident(a.shape)
    consts = [lw["w_pa"], lw["w_pb"], lw["w_o"], lw["norm2_g"].reshape(1, d), lw["w_up"], lw["conv_w"],
              lw["conv_b"].reshape(1, d_ff), lw["w_down"], final_g.reshape(1, d)]
    acts = [x2d, att, rwo, gates, e]
    y, ug = pl.pallas_call(
        functools.partial(_ffn_sample_kernel, seq=seq),
        grid=(n // nrows,),
        in_specs=[rows(a) for a in acts] + [full(c) for c in consts],
        out_specs=[
            pl.BlockSpec((nrows, d), lambda i: (i, 0)),
            pl.BlockSpec((nrows, d_ff), lambda i: (i, 0)),
        ],
        out_shape=[jax.ShapeDtypeStruct((n, d), F32), jax.ShapeDtypeStruct((n, d_ff), F32)],
        compiler_params=_cparams(("parallel",)),
        name="mix_conv_ffn_sample",
    )(*acts, *consts)
    conv_new = ug.reshape(batch, seq, d_ff)[:, seq - (CONV_W - 1):]
    return y, conv_new


def _row_tile(n, want):
    t = min(want, n)
    while n % t:
        t //= 2
    return t


def _layer(x, is_prompt, state, rel_bias, lw, final_g):
    batch, seq, d = x.shape
    n = batch * seq
    x2d = x.reshape(n, d)
    n_qkv = Q_W + 2 * KV_W
    n_rw = lw["mu_shift"].shape[0]
    tm = _row_tile(n, min(1024, n // 4))
    qkv, rw, gates = _in_proj(x2d, lw["norm1_g"], lw["w_in"], n_qkv, n_rw, tm)
    if is_prompt:
        att, k_rows, v_rows = _attention_prompt(qkv, batch, seq, rel_bias, lw["sinks"], state["win_buf"])
        shift0 = wkv0 = None
    else:
        cache_k, cache_v = state["cache_k"], state["cache_v"]
        wb = cache_k.shape[1]
        assert seq <= wb
        att, k_rows, v_rows = _attention_sample(qkv, batch, seq, cache_k, cache_v, rel_bias, lw["sinks"],
                                                _row_tile(batch, 16))
        shift0, wkv0 = state["shift"], state["wkv"]
    k_rows = k_rows.reshape(batch, -1, N_KV, HEAD_DIM)
    v_rows = v_rows.reshape(batch, -1, N_KV, HEAD_DIM)
    rwo, wkv_new = _rwkv_mixer(rw, batch, seq, shift0, wkv0, lw["mu_shift"], lw["w0"], lw["w2"], lw["a0"],
                               lw["a2"], lw["g2"], lw["k_k"], lw["k_a"], lw["r_k"], lw["lnx_g"], lw["lnx_b"])
    shift_new = rw.reshape(batch, seq, n_rw)[:, seq - 1]
    if is_prompt:
        y, conv_new = _ffn_prompt(x2d, att, rwo, gates, batch, seq, lw, final_g, _row_tile(seq, FFN_ROWS))
    else:
        y, conv_new = _ffn_sample(x2d, att, rwo, gates, batch, seq, state["conv"], lw, final_g,
                                  _row_tile(batch, FFN_ROWS_WITH_STATE // seq))
    return y.reshape(batch, seq, d), (k_rows, v_rows, shift_new, wkv_new, conv_new)


def kernel(x_prompt, x_sample, cache_win_k, cache_win_v, state_shift, state_wkv, state_conv, rel_bias, norm1_g,
           w_in, sinks, mu_shift, w0, w2, a0, a2, g2, k_k, k_a, r_k, lnx_g, lnx_b, w_pa, w_pb, w_o, norm2_g,
           w_up, conv_w, conv_b, w_down, final_g):
    depth = w_in.shape[0]
    assert depth == 1, "the final norm is fused into the layer's last kernel"
    l = 0
    lw = dict(norm1_g=norm1_g[l], w_in=w_in[l].astype(BF16), sinks=sinks[l], mu_shift=mu_shift[l], w0=w0[l],
              w2=w2[l], a0=a0[l], a2=a2[l], g2=g2[l], k_k=k_k[l], k_a=k_a[l], r_k=r_k[l].reshape(-1),
              lnx_g=lnx_g[l], lnx_b=lnx_b[l], w_pa=w_pa[l].astype(BF16), w_pb=w_pb[l].astype(BF16),
              w_o=w_o[l].astype(BF16), norm2_g=norm2_g[l], w_up=w_up[l].astype(BF16), conv_w=conv_w[l],
              conv_b=conv_b[l], w_down=w_down[l].astype(BF16))
    win_buf = cache_win_k.shape[2]
    y_p, st_p = _layer(x_prompt, True, dict(win_buf=win_buf), rel_bias, lw, final_g)
    y_s, st_s = _layer(x_sample, False,
                       dict(cache_k=cache_win_k[l], cache_v=cache_win_v[l], shift=state_shift[l],
                            wkv=state_wkv[l], conv=state_conv[l]), rel_bias, lw, final_g)
    stack = lambda t: t[None]
    return (y_p, y_s) + tuple(stack(t) for t in st_p) + tuple(stack(t) for t in st_s)
```

```python
import functools
import math

import numpy as np
import jax
import jax.numpy as jnp
from jax import lax
from jax.experimental import pallas as pl
from jax.experimental.pallas import tpu as pltpu

F32 = jnp.float32
BF16 = jnp.bfloat16

HEAD_DIM = 64
N_HEADS = 8
N_KV = 2
WINDOW = 128
N_BUCKETS = 32
MAX_EXACT = N_BUCKETS // 2
REL_MAX_DIST = 128
RW_N = 64
RW_HEADS = 8
RW = RW_HEADS * RW_N
NORM_EPS = 1e-6
GN_EPS = 64e-5
NEG = -1e30
CONV_W = 3

Q_W = N_HEADS * HEAD_DIM
KV_W = N_KV * HEAD_DIM
LANES = 128
SUBLANES = 8
MXU_WIDTH = 256
CHUNK = 64
VMEM_LIMIT = 56 * 1024 * 1024
FFN_COL_CHUNKS = 3
FFN_ROWS = 512
FFN_ROWS_WITH_STATE = 256
ATTN_BLOCKS = 8
WKV_UNITS = 8
WKV_UNITS_WITH_STATE = 4


def _resident(shape):
    return pl.BlockSpec(shape, lambda *_: (0,) * len(shape), pipeline_mode=pl.Buffered(1))


def _cparams(sem):
    return pltpu.CompilerParams(dimension_semantics=sem, vmem_limit_bytes=VMEM_LIMIT)


def _sigmoid(x):
    return 1.0 / (1.0 + jnp.exp(-x))


def _dg(a, b, kind):
    if kind == "nn":
        dn = (((1,), (0,)), ((), ()))
    else:
        dn = (((1,), (1,)), ((), ()))
    return lax.dot_general(a, b, dn, preferred_element_type=F32)


def _mm(a, b, kind="nn"):
    return _dg(a.astype(BF16), b.astype(BF16), kind)


def _mm_exact_lhs(a_bf16, b, n_terms):
    out = None
    rem = b
    for _ in range(n_terms):
        piece = rem.astype(BF16)
        term = _dg(a_bf16, piece, "nn")
        out = term if out is None else out + term
        rem = rem - piece.astype(F32)
    return out


def _inproj_kernel(x_ref, g_ref, w_ref, qkv_ref, rw_ref, gate_ref, *, n_qkv, n_rw):
    x = x_ref[...]
    ms = jnp.mean(x * x, axis=-1, keepdims=True)
    h = (x * lax.rsqrt(ms + NORM_EPS) * g_ref[...]).astype(BF16)
    n_gate = gate_ref.shape[1]
    step = MXU_WIDTH
    plain = [(qkv_ref, c, c, min(step, n_qkv - c)) for c in range(0, n_qkv, step)]
    plain += [(rw_ref, c, n_qkv + c, min(step, n_rw - c)) for c in range(0, n_rw, step)]
    gated = [(gate_ref, c, n_qkv + n_rw + c, min(step, n_gate - c)) for c in range(0, n_gate, step)]
    pieces = []
    while plain or gated:
        pieces += [gated.pop(0)] if gated else []
        pieces += [plain.pop(0)] if plain else []
    dot = lambda p: jnp.dot(h, w_ref[:, p[2]:p[2] + p[3]], preferred_element_type=F32)
    pending = dot(pieces[0])
    for j, (ref, c0, _, width) in enumerate(pieces):
        out = pending
        if j + 1 < len(pieces):
            pending = dot(pieces[j + 1])
        if ref is gate_ref:
            out = _sigmoid(out)
        ref[:, c0:c0 + width] = out.astype(ref.dtype)


def _in_proj(x2d, g, w_bf16, n_qkv, n_rw, tm):
    n, d = x2d.shape
    n_gate = w_bf16.shape[1] - n_qkv - n_rw
    return pl.pallas_call(
        functools.partial(_inproj_kernel, n_qkv=n_qkv, n_rw=n_rw),
        grid=(n // tm,),
        in_specs=[
            pl.BlockSpec((tm, d), lambda i: (i, 0)),
            _resident((1, d)),
            _resident(w_bf16.shape),
        ],
        out_specs=[
            pl.BlockSpec((tm, n_qkv), lambda i: (i, 0)),
            pl.BlockSpec((tm, n_rw), lambda i: (i, 0)),
            pl.BlockSpec((tm, n_gate), lambda i: (i, 0)),
        ],
        out_shape=[
            jax.ShapeDtypeStruct((n, n_qkv), F32),
            jax.ShapeDtypeStruct((n, n_rw), F32),
            jax.ShapeDtypeStruct((n, n_gate), BF16),
        ],
        compiler_params=_cparams(("parallel",)),
        name="in_proj",
    )(x2d, g.reshape(1, d), w_bf16)


def _t5_bucket_np(dist):
    n = np.maximum(dist, 0)
    nf = np.maximum(n, 1).astype(np.float32)
    large = MAX_EXACT + (np.log(nf / MAX_EXACT) / math.log(REL_MAX_DIST / MAX_EXACT)
                         * (N_BUCKETS - MAX_EXACT)).astype(np.int32)
    return np.where(n < MAX_EXACT, n, np.minimum(large, N_BUCKETS - 1)).astype(np.int32)


def _attn_kernel(q_ref, k1_ref, k2_ref, v1_ref, v2_ref, bucket_ref, relb_ref, sink_ref, o_ref, *rest,
                 nq, nk, n_blocks, first_block_axis, emit_window):
    bt = q_ref.shape[0]
    bias_ref = rest[-1]
    if emit_window == "tail":
        @pl.when(pl.program_id(1) == pl.num_programs(1) - 1)
        def _():
            n_new, n_win = k2_ref.shape[1], rest[0].shape[1]
            rest[0][...] = k2_ref[:, n_new - n_win:]
            rest[1][...] = v2_ref[:, n_new - n_win:]
    else:
        n_old, n_new = k1_ref.shape[1], k2_ref.shape[1]
        for w_ref, old_ref, new_ref in ((rest[0], k1_ref, k2_ref), (rest[1], v1_ref, v2_ref)):
            w_ref[:, :n_old - n_new] = old_ref[:, n_new:]
            w_ref[:, n_old - n_new:] = new_ref[...]
    first_step = pl.program_id(0) == 0
    if first_block_axis is not None:
        first_step = jnp.logical_and(first_step, pl.program_id(1) == 0)

    @pl.when(first_step)
    def _():
        bucket = bucket_ref[...]
        prev_key = lax.broadcasted_iota(jnp.int32, (1, nk), 1) < (nk // 2)
        for n in range(N_HEADS):
            acc = jnp.full((nq, nk), NEG, F32)
            for b in range(N_BUCKETS):
                acc = jnp.where(bucket == b, relb_ref[b, n], acc)
            c, half = divmod(n, 2)
            bias_ref[0, c, :, half * nk:(half + 1) * nk] = acc
            if first_block_axis is not None:
                bias_ref[1, c, :, half * nk:(half + 1) * nk] = jnp.where(prev_key, NEG, acc)

    n_keys = nk + (n_blocks - 1) * nq

    def padded(a_ref, b_ref):
        parts = [a_ref[...], b_ref[...]]
        n_now = a_ref.shape[1] + b_ref.shape[1]
        if n_now < n_keys:
            parts.append(jnp.zeros((bt, n_keys - n_now, LANES), F32))
        return jnp.concatenate(parts, axis=1)

    kk = padded(k1_ref, k2_ref)
    vv = padded(v1_ref, v2_ref)
    kk_r = pltpu.roll(kk, HEAD_DIM, 2)
    vv_r = pltpu.roll(vv, HEAD_DIM, 2)
    lane = lax.broadcasted_iota(jnp.int32, (1, 1, LANES), 2)
    lo = lane < HEAD_DIM

    def halves(x, x_r, kvh):
        src_lo, src_hi = (x, x_r) if kvh == 0 else (x_r, x)
        even = jnp.where(lo, src_lo, 0.0).astype(BF16)
        odd = jnp.where(lo, 0.0, src_hi).astype(BF16)
        return even, odd

    k_eo = [halves(kk, kk_r, h) for h in range(N_KV)]
    v_eo = [halves(vv, vv_r, h) for h in range(N_KV)]

    def window(eo, j):
        return jnp.concatenate([eo[0][:, j * nq:j * nq + nk], eo[1][:, j * nq:j * nq + nk]], axis=1)

    n_cols = N_HEADS // 2
    units = [(j, c) for j in range(n_blocks) for c in range(n_cols)]
    kvh_of = lambda c: (2 * c) // (N_HEADS // N_KV)
    k_win = {(j, h): window(k_eo[h], j) for j in range(n_blocks) for h in range(N_KV)}
    v_win = {(j, h): window(v_eo[h], j) for j in range(n_blocks) for h in range(N_KV)}
    sums_on_mxu = bt == 1
    if sums_on_mxu:
        key_row = lax.broadcasted_iota(jnp.int32, (1, 2 * nk, 1), 1)
        ones_cols = jnp.where((key_row < nk) == lo, 1.0, 0.0).astype(BF16)
        v_win = {jh: jnp.concatenate([v, ones_cols], axis=2) for jh, v in v_win.items()}

    scale = HEAD_DIM ** -0.5
    first_table = 0
    if first_block_axis is not None:
        first_table = jnp.where(pl.program_id(first_block_axis) == 0, 1, 0)
    s_all = {}
    for j, c in units:
        qc = (q_ref[:, j * nq:(j + 1) * nq, c * LANES:(c + 1) * LANES] * scale).astype(BF16)
        s_all[j, c] = jnp.einsum("bqd,bkd->bqk", qc, k_win[j, kvh_of(c)], preferred_element_type=F32)
    e_all, sink_all = {}, {}
    for j, c in units:
        s = s_all[j, c] + bias_ref[first_table if j == 0 else 0, c][None]
        es, sink_terms = [], []
        for half in range(2):
            sh = s[:, :, half * nk:(half + 1) * nk]
            sink = sink_ref[2 * c + half]
            m = jnp.maximum(jnp.max(sh, axis=-1, keepdims=True), sink)
            e = jnp.exp(sh - m)
            es.append(e.astype(BF16))
            sink_terms.append(jnp.exp(sink - m) + (0.0 if sums_on_mxu else jnp.sum(e, axis=-1, keepdims=True)))
        e_all[j, c] = jnp.concatenate(es, axis=2)
        sink_all[j, c] = jnp.where(lo, sink_terms[0], sink_terms[1])
    for j, c in units:
        o = jnp.einsum("bqk,bkd->bqd", e_all[j, c], v_win[j, kvh_of(c)], preferred_element_type=F32)
        denom = sink_all[j, c] + (o[:, :, LANES:] if sums_on_mxu else 0.0)
        o = (o[:, :, :LANES] / denom)
        if len(o_ref.shape) == 2:
            o_ref[:, c * LANES:(c + 1) * LANES] = o.reshape(bt * nq, LANES).astype(o_ref.dtype)
        else:
            o_ref[:, j * nq:(j + 1) * nq, c * LANES:(c + 1) * LANES] = o.astype(o_ref.dtype)


def _attention_prompt(qkv, batch, seq, rel_bias, sinks, win_buf):
    nblk = seq // WINDOW
    nb = ATTN_BLOCKS if nblk % ATTN_BLOCKS == 0 else 1
    nsteps = nblk // nb
    width = qkv.shape[1]
    q_blk = qkv.reshape(batch * nblk, WINDOW, width)
    q_step = qkv.reshape(batch * nsteps, nb * WINDOW, width)
    kcol = Q_W // LANES
    vcol = (Q_W + KV_W) // LANES
    nk = 2 * WINDOW
    qi = np.arange(WINDOW)[:, None] + WINDOW
    kj = np.arange(nk)[None, :]
    dist = qi - kj
    bucket = np.where((dist >= 0) & (dist < WINDOW), _t5_bucket_np(dist), -1).astype(np.int32)
    cur = lambda c: (lambda b, i: (b * nsteps + i, 0, c))
    prev = lambda c: (lambda b, i: (b * nblk + jnp.maximum(i * nb - 1, 0), 0, c))
    assert win_buf <= nb * WINDOW
    win_spec = pl.BlockSpec((1, win_buf, LANES), lambda b, i: (b, 0, 0))
    win_shape = jax.ShapeDtypeStruct((batch, win_buf, KV_W), F32)
    out, win_k, win_v = pl.pallas_call(
        functools.partial(_attn_kernel, nq=WINDOW, nk=nk, n_blocks=nb, first_block_axis=1, emit_window="tail"),
        grid=(batch, nsteps),
        in_specs=[
            pl.BlockSpec((1, nb * WINDOW, Q_W), cur(0)),
            pl.BlockSpec((1, WINDOW, LANES), prev(kcol)),
            pl.BlockSpec((1, nb * WINDOW, LANES), cur(kcol)),
            pl.BlockSpec((1, WINDOW, LANES), prev(vcol)),
            pl.BlockSpec((1, nb * WINDOW, LANES), cur(vcol)),
            _resident(bucket.shape),
            pl.BlockSpec(memory_space=pltpu.SMEM),
            pl.BlockSpec(memory_space=pltpu.SMEM),
        ],
        out_specs=[pl.BlockSpec((1, nb * WINDOW, Q_W), cur(0)), win_spec, win_spec],
        out_shape=[jax.ShapeDtypeStruct((batch * nsteps, nb * WINDOW, Q_W), BF16), win_shape, win_shape],
        scratch_shapes=[pltpu.VMEM((2, N_HEADS // 2, WINDOW, 2 * nk), F32)],
        compiler_params=_cparams(("arbitrary", "arbitrary")),
        name="attn_prompt",
    )(q_step, q_blk, q_step, q_blk, q_step, jnp.asarray(bucket), rel_bias, sinks)
    return out.reshape(batch * seq, Q_W), win_k, win_v


def _attention_sample(qkv, batch, seq, cache_k, cache_v, rel_bias, sinks, bt):
    wb = cache_k.shape[1]
    nk = 2 * WINDOW
    q3 = qkv.reshape(batch, seq, qkv.shape[1])
    ck = cache_k.reshape(batch, wb, KV_W)
    cv = cache_v.reshape(batch, wb, KV_W)
    kcol = Q_W // LANES
    vcol = (Q_W + KV_W) // LANES
    tq = np.arange(seq)[:, None]
    j = np.arange(nk)[None, :]
    dist = np.where(j < wb, tq + wb - j, tq - (j - wb))
    ok = (dist >= 0) & (dist < WINDOW) & (j < wb + seq)
    bucket = np.where(ok, _t5_bucket_np(dist), -1).astype(np.int32)
    win_spec = pl.BlockSpec((bt, wb, LANES), lambda b: (b, 0, 0))
    win_shape = jax.ShapeDtypeStruct((batch, wb, KV_W), F32)
    out, win_k, win_v = pl.pallas_call(
        functools.partial(_attn_kernel, nq=seq, nk=nk, n_blocks=1, first_block_axis=None, emit_window="shift"),
        grid=(batch // bt,),
        in_specs=[
            pl.BlockSpec((bt, seq, Q_W), lambda b: (b, 0, 0)),
            pl.BlockSpec((bt, wb, LANES), lambda b: (b, 0, 0)),
            pl.BlockSpec((bt, seq, LANES), lambda b: (b, 0, kcol)),
            pl.BlockSpec((bt, wb, LANES), lambda b: (b, 0, 0)),
            pl.BlockSpec((bt, seq, LANES), lambda b: (b, 0, vcol)),
            _resident(bucket.shape),
            pl.BlockSpec(memory_space=pltpu.SMEM),
            pl.BlockSpec(memory_space=pltpu.SMEM),
        ],
        out_specs=[pl.BlockSpec((bt * seq, Q_W), lambda b: (b, 0)), win_spec, win_spec],
        out_shape=[jax.ShapeDtypeStruct((batch * seq, Q_W), BF16), win_shape, win_shape],
        scratch_shapes=[pltpu.VMEM((1, N_HEADS // 2, seq, 2 * nk), F32)],
        compiler_params=_cparams(("arbitrary",)),
        name="attn_sample",
    )(q3, ck, q3, cv, q3, jnp.asarray(bucket), rel_bias, sinks)
    return out, win_k, win_v


def _wkv_kernel(*refs, n_units, n_seg, has_state, lora_w):
    C = CHUNK
    seg_len = C // n_seg
    n_pairs = RW // LANES
    it = iter(refs)
    p_ref = next(it)
    p0_ref, s0_ref = (next(it), next(it)) if has_state else (None, None)
    (mu_ref, w2a2_ref, w0_ref, a0_ref, g2_ref, kk_ref, ka_ref, rk_ref, lng_ref, lnb_ref, ltri_ref) = (
        next(it) for _ in range(11))
    elast_ref = next(it) if n_seg > 1 else None
    ones_ref, out_ref, sout_ref = next(it), next(it), next(it)
    last_ref, sbd_ref = (None, None) if has_state else (next(it), next(it))

    if not has_state:
        @pl.when(pl.program_id(1) == 0)
        def _():
            last_ref[...] = jnp.zeros_like(last_ref)
            sbd_ref[...] = jnp.zeros_like(sbd_ref)

    row = lax.broadcasted_iota(jnp.int32, (C, 1), 0)
    rows_of = lambda u: slice(u * C, (u + 1) * C)
    lane = lax.broadcasted_iota(jnp.int32, (1, LANES), 1)
    lo = lane < RW_N
    ones_bd = ones_ref[...]

    def headsum(x):
        xb = x.astype(BF16)
        return jnp.concatenate([_dg(xb[:, c:c + MXU_WIDTH], ones_bd, "nn") for c in range(0, RW, MXU_WIDTH)],
                               axis=1)

    def prepare(us):
        xs_parts = []
        for u in us:
            p = p_ref[u, 0]
            rolled = pltpu.roll(p, 1, 0)
            if has_state:
                prev = jnp.where(row % seg_len == 0, p0_ref[u, 0], rolled)
            else:
                prev = jnp.where(row == 0, last_ref[u], rolled)
                last_ref[u] = p_ref[u, 0, C - 1:C, :]
            xs_parts.append(p + mu_ref[...] * (prev - p))
        xs = jnp.concatenate(xs_parts, axis=0)
        r = xs[:, 0:RW]
        k = xs[:, RW:2 * RW]
        v = xs[:, 2 * RW:3 * RW]
        lwla = xs[:, 3 * RW:3 * RW + LANES]
        lg = xs[:, 3 * RW + LANES:3 * RW + 2 * LANES]
        lwla = jnp.where(lane < lora_w, jnp.tanh(lwla), lwla)
        wa = jnp.dot(lwla.astype(BF16), w2a2_ref[...], preferred_element_type=F32)
        logw = -math.exp(-0.5) * _sigmoid(w0_ref[...] + wa[:, :RW])
        a_sig = _sigmoid(a0_ref[...] + wa[:, RW:])
        g = jnp.dot(_sigmoid(lg).astype(BF16), g2_ref[...], preferred_element_type=F32)
        kk = k * kk_ref[...]
        kk = kk * lax.rsqrt(jnp.maximum(headsum(kk * kk), 1e-24))
        k = k * (1.0 + (a_sig - 1.0) * ka_ref[...])
        a = -kk
        b = kk * a_sig
        bonus = headsum(r * k * rk_ref[...]) * v
        cws, cwl = [], []
        for i in range(len(us)):
            cw_u = _mm_exact_lhs(ltri_ref[...], logw[rows_of(i)], 3)
            cws.append(cw_u)
            if n_seg == 1:
                cwl.append(jnp.broadcast_to(cw_u[C - 1:C, :], (C, RW)))
            else:
                cwl.append(_mm_exact_lhs(elast_ref[...], cw_u, 3))
        cw = jnp.concatenate(cws, axis=0)
        cw_last = jnp.concatenate(cwl, axis=0)
        w_inv = jnp.exp(-cw)
        b_t, k_t = b * w_inv, k * w_inv
        w_tail = jnp.exp(cw_last - cw)
        b_h, k_h = b * w_tail, k * w_tail
        a_t, r_t = a * jnp.exp(cw - logw), r * jnp.exp(cw)
        if n_seg == 1:
            w_c = jnp.exp(jnp.concatenate([cw_u[C - 1:C, :] for cw_u in cws], axis=0))
        else:
            w_c = jnp.exp(cw_last)
        return a_t, r_t, b_t, k_t, b_h, k_h, v, w_c, bonus, g

    def bd(y):
        return jnp.concatenate([jnp.where(lo, y, 0.0), jnp.where(lo, 0.0, y)], axis=0)

    zeros_head = jnp.zeros((RW_N, RW_N), F32)

    def pack_pair(s_even, s_odd):
        return jnp.concatenate([jnp.concatenate([s_even, zeros_head], axis=1),
                                jnp.concatenate([zeros_head, s_odd], axis=1)], axis=0)

    def store_pair(ref, i, q, s_pair):
        ref[i, 2 * q] = s_pair[:RW_N, :RW_N]
        ref[i, 2 * q + 1] = s_pair[RW_N:, RW_N:]

    s_idx = lane % C
    strict = s_idx < row
    incl = s_idx <= row
    if n_seg > 1:
        same_seg = (s_idx // seg_len) == (row // seg_len)
        strict = jnp.logical_and(strict, same_seg)
        incl = jnp.logical_and(incl, same_seg)
    row2 = lax.broadcasted_iota(jnp.int32, (2 * C, 1), 0)
    same_head = (row2 < RW_N) == lo

    part = lambda x, i, q: x[i * C:(i + 1) * C, q * LANES:(q + 1) * LANES]

    def recurrence(us, ops):
        a_t, r_t, b_t, k_t, b_h, k_h, v, w_c = ops[:8]
        units = [(i, q) for i in range(len(us)) for q in range(n_pairs)]
        sc = {uq: _mm(jnp.concatenate([part(a_t, *uq), part(r_t, *uq)], axis=0),
                      jnp.concatenate([bd(part(b_t, *uq)), bd(part(k_t, *uq))], axis=0),
                      "nt") for uq in units}
        pw = {uq: jnp.where(strict, sc[uq][:C, :LANES], 0.0) for uq in units}
        m_rb = {uq: jnp.where(incl, sc[uq][C:, :LANES], 0.0) for uq in units}
        lm_v = {uq: _mm(jnp.concatenate([jnp.where(strict, sc[uq][:C, LANES:], 0.0),
                                         jnp.where(incl, sc[uq][C:, LANES:], 0.0)], axis=0),
                        bd(part(v, *uq)), "nn") for uq in units}
        tinv = {uq: pw[uq] + jnp.where(s_idx == row, 1.0, 0.0) for uq in units}
        n_lvl = int(math.log2(seg_len))
        for lvl in range(1, n_lvl):
            last = lvl + 1 == n_lvl
            if lvl == 1:
                for uq in units:
                    pw[uq] = _mm(pw[uq], bd(pw[uq]), "nn")
            for uq in units:
                rhs = bd(tinv[uq]) if last else jnp.concatenate([bd(tinv[uq]), bd(pw[uq])], axis=1)
                upd = _mm(pw[uq], rhs, "nn")
                tinv[uq] = tinv[uq] + upd[:, :LANES]
                if not last:
                    pw[uq] = upd[:, LANES:]
        x = {uq: _mm(tinv[uq], jnp.concatenate([bd(part(a_t, *uq)), bd(lm_v[uq][:C])], axis=1), "nn")
             for uq in units}
        a_hat = {uq: x[uq][:, :LANES] for uq in units}
        v_hat = {uq: x[uq][:, LANES:] for uq in units}
        z = {uq: _mm(m_rb[uq], jnp.concatenate([bd(a_hat[uq]), bd(v_hat[uq])], axis=1), "nn")
             for uq in units}
        r_hat = {uq: part(r_t, *uq) + z[uq][:, :LANES] for uq in units}
        y_intra = {uq: z[uq][:, LANES:] + lm_v[uq][C:] for uq in units}
        ys = {}
        if n_seg == 1:
            s_old = {(i, q): sbd_ref[us[i], q] for i, q in units}
            t1 = {uq: _mm(jnp.concatenate([a_hat[uq], r_hat[uq]], axis=0), s_old[uq], "nt")
                  for uq in units}
            for uq in units:
                ys[uq] = t1[uq][C:] + y_intra[uq]
                uv = jnp.concatenate([t1[uq][:C] + v_hat[uq], part(v, *uq)], axis=0)
                bkh = jnp.concatenate([part(b_h, *uq), part(k_h, *uq)], axis=0)
                ds = _mm(uv.T, bkh, "nn")
                w_cq = w_c[uq[0]:uq[0] + 1, uq[1] * LANES:(uq[1] + 1) * LANES]
                sbd_ref[us[uq[0]], uq[1]] = s_old[uq] * w_cq + jnp.where(same_head, ds, 0.0)
        else:
            row_seg = (row2 % C) // seg_len
            for uq in units:
                i, q = uq
                first_seq = us[i] * n_seg
                u_parts, y_parts, s_olds = [], [], []
                for sg in range(n_seg):
                    rs = slice(sg * seg_len, (sg + 1) * seg_len)
                    s_sg = pack_pair(s0_ref[first_seq + sg, 2 * q], s0_ref[first_seq + sg, 2 * q + 1])
                    t1 = _mm(jnp.concatenate([a_hat[uq][rs], r_hat[uq][rs]], axis=0), s_sg, "nt")
                    u_parts.append(t1[:seg_len] + v_hat[uq][rs])
                    y_parts.append(t1[seg_len:] + y_intra[uq][rs])
                    s_olds.append(s_sg)
                ys[uq] = jnp.concatenate(y_parts, axis=0)
                uv_t = jnp.concatenate(u_parts + [part(v, *uq)], axis=0).T
                bkh = jnp.concatenate([part(b_h, *uq), part(k_h, *uq)], axis=0)
                w_cq = part(w_c, *uq)
                by_seg = jnp.concatenate([jnp.where(row_seg == sg, bkh, 0.0) for sg in range(n_seg)], axis=1)
                ds_all = _mm(uv_t, by_seg, "nn")
                for sg in range(n_seg):
                    ds = ds_all[:, sg * LANES:(sg + 1) * LANES]
                    store_pair(sout_ref, first_seq + sg, q,
                               s_olds[sg] * w_cq[sg * seg_len:sg * seg_len + 1] + jnp.where(same_head, ds, 0.0))
        return ys

    def finish(us, ys, ops):
        bonus, g = ops[8:]
        y = jnp.concatenate([jnp.concatenate([ys[i, q] for q in range(n_pairs)], axis=1)
                             for i in range(len(us))], axis=0)
        mean = headsum(y) * (1.0 / RW_N)
        d = y - mean
        var = headsum(d * d) * (1.0 / RW_N)
        y = d * lax.rsqrt(var + GN_EPS) * lng_ref[...] + lnb_ref[...]
        y = ((y + bonus) * g).astype(out_ref.dtype)
        for i, u in enumerate(us):
            out_ref[u, 0] = y[rows_of(i)]

    all_units = list(range(n_units))
    ops = prepare(all_units)
    finish(all_units, recurrence(all_units, ops), ops)
    units = [(u, q) for u in range(n_units) for q in range(n_pairs)]

    if not has_state:
        @pl.when(pl.program_id(1) == pl.num_programs(1) - 1)
        def _():
            for u, q in units:
                store_pair(sout_ref, u, q, sbd_ref[u, q])


def _rwkv_mixer(rw, batch, seq, shift0, wkv0, mu, w0, w2, a0, a2, g2, k_k, k_a, r_k, lnx_g, lnx_b):
    n_shift = rw.shape[1]
    lora_w, lora_a = w2.shape[0], a2.shape[0]
    assert lora_w + lora_a == LANES and g2.shape[0] == LANES and n_shift == 3 * RW + 2 * LANES
    has_state = shift0 is not None
    C = CHUNK
    n_pairs = RW // LANES
    if has_state:
        assert C % seq == 0 and batch % (C // seq) == 0
        n_seg, n_chunks, n_groups = C // seq, 1, batch * seq // C
    else:
        assert seq % C == 0
        n_seg, n_chunks, n_groups = 1, seq // C, batch
    want = WKV_UNITS_WITH_STATE if has_state else WKV_UNITS
    nu = want if n_groups % want == 0 else 1
    seg_len = C // n_seg
    w2a2 = jnp.zeros((LANES, 2 * RW), F32).at[:lora_w, :RW].set(w2).at[lora_w:, RW:].set(a2).astype(BF16)
    t = np.arange(C)
    same_seg = (t[:, None] // seg_len) == (t[None, :] // seg_len)
    ltri = jnp.asarray(((t[:, None] >= t[None, :]) & same_seg).astype(np.float32), BF16)
    elast = jnp.asarray((t[None, :] == (t[:, None] // seg_len) * seg_len + seg_len - 1).astype(np.float32), BF16)
    ones_bd = jnp.asarray(np.kron(np.eye(MXU_WIDTH // RW_N, dtype=np.float32),
                                  np.ones((RW_N, RW_N), np.float32)), BF16)
    row = lambda x: x.reshape(1, -1).astype(F32)
    rw4 = rw.reshape(n_groups, n_chunks, C, n_shift)
    blk = lambda w: pl.BlockSpec((nu, 1, C, w), lambda i, c: (i, c, 0, 0))
    st_blk = pl.BlockSpec((nu * n_seg, RW_HEADS, RW_N, RW_N), lambda i, c: (i, 0, 0, 0))
    args, specs = [rw4], [blk(n_shift)]
    if has_state:
        p0 = jnp.pad(shift0[:, None, :], ((0, 0), (0, seq - 1), (0, 0))).reshape(n_groups, 1, C, n_shift)
        args += [p0, wkv0]
        specs += [blk(n_shift), st_blk]
    consts = [row(mu), w2a2, row(w0), row(a0), g2.astype(BF16), row(k_k), row(k_a), row(r_k), row(lnx_g),
              row(lnx_b), ltri] + ([elast] if n_seg > 1 else []) + [ones_bd]
    args += consts
    specs += [_resident(c.shape) for c in consts]
    scratch = [] if has_state else [pltpu.VMEM((nu, 1, n_shift), F32),
                                    pltpu.VMEM((nu, n_pairs, LANES, LANES), F32)]
    out, s_new = pl.pallas_call(
        functools.partial(_wkv_kernel, n_units=nu, n_seg=n_seg, has_state=has_state, lora_w=lora_w),
        grid=(n_groups // nu, n_chunks),
        in_specs=specs,
        out_specs=[blk(RW), st_blk],
        out_shape=[
            jax.ShapeDtypeStruct((n_groups, n_chunks, C, RW), BF16),
            jax.ShapeDtypeStruct((batch, RW_HEADS, RW_N, RW_N), F32),
        ],
        scratch_shapes=scratch,
        compiler_params=_cparams(("arbitrary", "arbitrary")),
        name="rwkv7",
    )(*args)
    return out.reshape(batch * seq, RW), s_new


def _branch_mix(x_ref, att_ref, rwo_ref, gate_ref, wpa_ref, wpb_ref, wo_ref):
    d = x_ref.shape[1]
    att, rwo = att_ref[...], rwo_ref[...]

    def branch_dots(c):
        return (jnp.dot(att, wpa_ref[:, c:c + MXU_WIDTH], preferred_element_type=F32),
                jnp.dot(rwo, wpb_ref[:, c:c + MXU_WIDTH], preferred_element_type=F32))

    mixes = []
    pending = branch_dots(0)
    for c in range(0, d, MXU_WIDTH):
        pa, pb = pending
        if c + MXU_WIDTH < d:
            pending = branch_dots(c + MXU_WIDTH)
        mixes.append((gate_ref[:, c:c + MXU_WIDTH] * pa + gate_ref[:, d + c:d + c + MXU_WIDTH] * pb).astype(BF16))
    mix = jnp.concatenate(mixes, axis=1)
    return x_ref[...] + jnp.dot(mix, wo_ref[...], preferred_element_type=F32)


def _rms(x, g):
    ms = jnp.mean(x * x, axis=-1, keepdims=True)
    return x * lax.rsqrt(ms + NORM_EPS) * g


def _gelu_tanh(c):
    return c * (0.5 * (1.0 + jnp.tanh(math.sqrt(2.0 / math.pi) * (c + 0.044715 * (c * c * c)))))


def _conv_ffn_cols(h, x, wup_ref, cw_ref, cb_ref, wdown_ref, shifted, ug_sink):
    d_ff = cb_ref.shape[1]
    tiles = -(-d_ff // MXU_WIDTH)
    edges = [min(d_ff, MXU_WIDTH * ((tiles * j + FFN_COL_CHUNKS - 1) // FFN_COL_CHUNKS))
             for j in range(FFN_COL_CHUNKS + 1)]
    spans = list(zip(edges[:-1], edges[1:]))

    def up(span):
        lo_c, hi_c = span
        return (jnp.dot(h, wup_ref[:, lo_c:hi_c], preferred_element_type=F32),
                jnp.dot(h, wup_ref[:, d_ff + lo_c:d_ff + hi_c], preferred_element_type=F32))

    acc = x
    pending = up(spans[0])
    for j, (lo_c, hi_c) in enumerate(spans):
        cs = slice(lo_c, hi_c)
        ug_full, uv_full = pending
        if j + 1 < len(spans):
            pending = up(spans[j + 1])
        ug, ug_m1, ug_m2, uv = shifted(ug_full, uv_full, cs)
        c = cb_ref[:, cs] + cw_ref[0:1, cs] * ug_m2 + cw_ref[1:2, cs] * ug_m1 + cw_ref[2:3, cs] * ug
        act = (_gelu_tanh(c) * uv).astype(BF16)
        acc = acc + jnp.dot(act, wdown_ref[lo_c:hi_c, :], preferred_element_type=F32)
        ug_sink(ug, cs)
    return acc


def _ffn_prompt_kernel(x_ref, att_ref, rwo_ref, gate_ref, wpa_ref, wpb_ref, wo_ref, g2_ref, wup_ref, cw_ref,
                       cb_ref, wdown_ref, gf_ref, y_ref, ug_ref, carry_ref):
    tm = x_ref.shape[0]
    x = _branch_mix(x_ref, att_ref, rwo_ref, gate_ref, wpa_ref, wpb_ref, wo_ref)
    h = _rms(x, g2_ref[...]).astype(BF16)
    seq_start = pl.program_id(1) == 0

    def shifted(ug, uv, cs):
        before = jnp.where(seq_start, 0.0, carry_ref[:, cs])
        carry_ref[:, cs] = ug[tm - SUBLANES:]
        ug_e = jnp.concatenate([before, ug], axis=0)
        return ug, pltpu.roll(ug_e, 1, 0)[SUBLANES:], pltpu.roll(ug_e, 2, 0)[SUBLANES:], uv

    def ug_sink(ug, cs):
        ug_ref[:, cs] = ug[tm - SUBLANES:]

    x2 = _conv_ffn_cols(h, x, wup_ref, cw_ref, cb_ref, wdown_ref, shifted, ug_sink)
    y_ref[...] = _rms(x2, gf_ref[...])


def _ffn_sample_kernel(x_ref, att_ref, rwo_ref, gate_ref, e_ref, wpa_ref, wpb_ref, wo_ref, g2_ref, wup_ref,
                       cw_ref, cb_ref, wdown_ref, gf_ref, y_ref, ug_ref, *, seq):
    rows = x_ref.shape[0]
    x = _branch_mix(x_ref, att_ref, rwo_ref, gate_ref, wpa_ref, wpb_ref, wo_ref)
    h = _rms(x, g2_ref[...]).astype(BF16)
    t = lax.broadcasted_iota(jnp.int32, (rows, 1), 0) % seq

    def shifted(ug, uv, cs):
        e = e_ref[:, cs]
        ug_m1 = jnp.where(t == 0, pltpu.roll(e, rows - 1, 0), pltpu.roll(ug, 1, 0))
        ug_m2 = jnp.where(t < 2, e, pltpu.roll(ug, 2, 0))
        return ug, ug_m1, ug_m2, uv

    def ug_sink(ug, cs):
        ug_ref[:, cs] = ug

    x2 = _conv_ffn_cols(h, x, wup_ref, cw_ref, cb_ref, wdown_ref, shifted, ug_sink)
    y_ref[...] = _rms(x2, gf_ref[...])


def _ffn_prompt(x2d, att, rwo, gates, batch, seq, lw, final_g, tm):
    n, d = x2d.shape
    d_ff = lw["conv_b"].shape[0]
    nt = seq // tm
    rows = lambda a: pl.BlockSpec((tm, a.shape[1]), lambda b, i: (b * nt + i, 0))
    full = lambda a: _resident(a.shape)
    consts = [lw["w_pa"], lw["w_pb"], lw["w_o"], lw["norm2_g"].reshape(1, d), lw["w_up"], lw["conv_w"],
              lw["conv_b"].reshape(1, d_ff), lw["w_down"], final_g.reshape(1, d)]
    acts = [x2d, att, rwo, gates]
    y, ug_last = pl.pallas_call(
        _ffn_prompt_kernel,
        grid=(batch, nt),
        in_specs=[rows(a) for a in acts] + [full(c) for c in consts],
        out_specs=[
            pl.BlockSpec((tm, d), lambda b, i: (b * nt + i, 0)),
            pl.BlockSpec((SUBLANES, d_ff), lambda b, i: (b, 0)),
        ],
        out_shape=[
            jax.ShapeDtypeStruct((n, d), F32),
            jax.ShapeDtypeStruct((batch * SUBLANES, d_ff), F32),
        ],
        scratch_shapes=[pltpu.VMEM((SUBLANES, d_ff), F32)],
        compiler_params=_cparams(("arbitrary", "arbitrary")),
        name="mix_conv_ffn_prompt",
    )(*acts, *consts)
    conv_new = ug_last.reshape(batch, SUBLANES, d_ff)[:, SUBLANES - (CONV_W - 1):]
    return y, conv_new


def _ffn_sample(x2d, att, rwo, gates, batch, seq, conv0, lw, final_g, bt):
    n, d = x2d.shape
    d_ff = lw["conv_b"].shape[0]
    nrows = bt * seq
    e = jnp.pad(conv0, ((0, 0), (0, seq - (CONV_W - 1)), (0, 0))).reshape(n, d_ff)
    rows = lambda a: pl.BlockSpec((nrows, a.shape[1]), lambda i: (i, 0))
    full = lambda a: _resident(a.shape)
    consts = [lw["w_pa"], lw["w_pb"], lw["w_o"], lw["norm2_g"].reshape(1, d), lw["w_up"], lw["conv_w"],
              lw["conv_b"].reshape(1, d_ff), lw["w_down"], final_g.reshape(1, d)]
    acts = [x2d, att, rwo, gates, e]
    y, ug = pl.pallas_call(
        functools.partial(_ffn_sample_kernel, seq=seq),
        grid=(n // nrows,),
        in_specs=[rows(a) for a in acts] + [full(c) for c in consts],
        out_specs=[
            pl.BlockSpec((nrows, d), lambda i: (i, 0)),
            pl.BlockSpec((nrows, d_ff), lambda i: (i, 0)),
        ],
        out_shape=[jax.ShapeDtypeStruct((n, d), F32), jax.ShapeDtypeStruct((n, d_ff), F32)],
        compiler_params=_cparams(("parallel",)),
        name="mix_conv_ffn_sample",
    )(*acts, *consts)
    conv_new = ug.reshape(batch, seq, d_ff)[:, seq - (CONV_W - 1):]
    return y, conv_new


def _row_tile(n, want):
    t = min(want, n)
    while n % t:
        t //= 2
    return t


def _layer(x, is_prompt, state, rel_bias, lw, final_g):
    batch, seq, d = x.shape
    n = batch * seq
    x2d = x.reshape(n, d)
    n_qkv = Q_W + 2 * KV_W
    n_rw = lw["mu_shift"].shape[0]
    tm = _row_tile(n, min(1024, n // 4))
    qkv, rw, gates = _in_proj(x2d, lw["norm1_g"], lw["w_in"], n_qkv, n_rw, tm)
    if is_prompt:
        att, k_rows, v_rows = _attention_prompt(qkv, batch, seq, rel_bias, lw["sinks"], state["win_buf"])
        shift0 = wkv0 = None
    else:
        cache_k, cache_v = state["cache_k"], state["cache_v"]
        wb = cache_k.shape[1]
        assert seq <= wb
        att, k_rows, v_rows = _attention_sample(qkv, batch, seq, cache_k, cache_v, rel_bias, lw["sinks"],
                                                _row_tile(batch, 16))
        shift0, wkv0 = state["shift"], state["wkv"]
    k_rows = k_rows.reshape(batch, -1, N_KV, HEAD_DIM)
    v_rows = v_rows.reshape(batch, -1, N_KV, HEAD_DIM)
    rwo, wkv_new = _rwkv_mixer(rw, batch, seq, shift0, wkv0, lw["mu_shift"], lw["w0"], lw["w2"], lw["a0"],
                               lw["a2"], lw["g2"], lw["k_k"], lw["k_a"], lw["r_k"], lw["lnx_g"], lw["lnx_b"])
    shift_new = rw.reshape(batch, seq, n_rw)[:, seq - 1]
    if is_prompt:
        y, conv_new = _ffn_prompt(x2d, att, rwo, gates, batch, seq, lw, final_g, _row_tile(seq, FFN_ROWS))
    else:
        y, conv_new = _ffn_sample(x2d, att, rwo, gates, batch, seq, state["conv"], lw, final_g,
                                  _row_tile(batch, FFN_ROWS_WITH_STATE // seq))
    return y.reshape(batch, seq, d), (k_rows, v_rows, shift_new, wkv_new, conv_new)


def kernel(x_prompt, x_sample, cache_win_k, cache_win_v, state_shift, state_wkv, state_conv, rel_bias, norm1_g,
           w_in, sinks, mu_shift, w0, w2, a0, a2, g2, k_k, k_a, r_k, lnx_g, lnx_b, w_pa, w_pb, w_o, norm2_g,
           w_up, conv_w, conv_b, w_down, final_g):
    depth = w_in.shape[0]
    assert depth == 1, "the final norm is fused into the layer's last kernel"
    l = 0
    lw = dict(norm1_g=norm1_g[l], w_in=w_in[l].astype(BF16), sinks=sinks[l], mu_shift=mu_shift[l], w0=w0[l],
              w2=w2[l], a0=a0[l], a2=a2[l], g2=g2[l], k_k=k_k[l], k_a=k_a[l], r_k=r_k[l].reshape(-1),
              lnx_g=lnx_g[l], lnx_b=lnx_b[l], w_pa=w_pa[l].astype(BF16), w_pb=w_pb[l].astype(BF16),
              w_o=w_o[l].astype(BF16), norm2_g=norm2_g[l], w_up=w_up[l].astype(BF16), conv_w=conv_w[l],
              conv_b=conv_b[l], w_down=w_down[l].astype(BF16))
    win_buf = cache_win_k.shape[2]
    y_p, st_p = _layer(x_prompt, True, dict(win_buf=win_buf), rel_bias, lw, final_g)
    y_s, st_s = _layer(x_sample, False,
                       dict(cache_k=cache_win_k[l], cache_v=cache_win_v[l], shift=state_shift[l],
                            wkv=state_wkv[l], conv=state_conv[l]), rel_bias, lw, final_g)
    stack = lambda t: t[None]
    return (y_p, y_s) + tuple(stack(t) for t in st_p) + tuple(stack(t) for t in st_s)
```
